```python
import math
import jax, jax.numpy as jnp
from jax import lax
import numpy as np

D_MODEL = 1024
BATCH = 16
SEQ = 4096
DEPTH = 1
DEC_BATCH = 16
DEC_SEQ = 64
PAST_LEN = 4096

CHUNK = 64
EPS = 1e-6
NEG_INF = -1e30

RET_HEADS = 4
RET_DK = 256
RET_DV = D_MODEL // RET_HEADS
ROPE_BASE = 10000.0

SWA_HEADS = 16
SWA_KV_HEADS = 2
SWA_GROUP = SWA_HEADS // SWA_KV_HEADS
SWA_HD = 64
WINDOW = 128
WIN_CHUNKS = WINDOW // CHUNK

REL_BUCKETS = 32
REL_MAX_DIST = 128

N_MEM = 256
MEM_HEADS = 4
MEM_HD = D_MODEL // MEM_HEADS

D_FF = -(-(8 * D_MODEL) // (3 * 256)) * 256

SPLIT_SIZES = (RET_HEADS * RET_DK, RET_HEADS * RET_DK, RET_HEADS * RET_DV, RET_HEADS * RET_DV,
               SWA_HEADS * SWA_HD, SWA_KV_HEADS * SWA_HD, SWA_KV_HEADS * SWA_HD, D_MODEL, D_MODEL)
SPLIT_OFFS = tuple(int(o) for o in np.cumsum(SPLIT_SIZES)[:-1])
D_IN = sum(SPLIT_SIZES)

kernel_name = 'hybrid_retention_swa_stream_step'


def rms_norm(x, g=None):
    xf = x.astype(jnp.float32)
    y = xf * lax.rsqrt(jnp.mean(xf * xf, axis=-1, keepdims=True) + EPS)
    if g is not None:
        y = y * g.astype(jnp.float32)
    return y.astype(x.dtype)


def rope(x, pos):
    half = x.shape[-1] // 2
    inv = ROPE_BASE ** (-jnp.arange(half, dtype=jnp.float32) / half)
    ang = pos.astype(jnp.float32)[:, None] * inv[None, :]
    cos = jnp.cos(ang)[:, None, :]
    sin = jnp.sin(ang)[:, None, :]
    xf = x.astype(jnp.float32)
    x1, x2 = xf[..., :half], xf[..., half:]
    return jnp.concatenate([x1 * cos - x2 * sin, x2 * cos + x1 * sin], axis=-1).astype(x.dtype)


def retention_log_decay():
    return jnp.log(1.0 - 2.0 ** (-5.0 - jnp.arange(RET_HEADS, dtype=jnp.float32)))


def retention_block(state, q, k, v, log_gamma):
    C = q.shape[1]
    idx = jnp.arange(C, dtype=jnp.float32)
    lg = log_gamma[None, :]
    intra = jnp.exp(log_gamma[:, None, None] * jnp.abs(idx[:, None] - idx[None, :]))
    s = jnp.einsum('bihd,bjhd->bhij', q, k, preferred_element_type=jnp.float32) * intra
    o = jnp.einsum('bhij,bjhe->bihe', s, v.astype(jnp.float32))
    q_dec = jnp.exp(lg * (idx[:, None] + 1.0))
    o = o + jnp.einsum('bihd,bhde->bihe', q.astype(jnp.float32), state) * q_dec[None, :, :, None]
    k_dec = jnp.exp(lg * (C - 1.0 - idx[:, None]))
    kv = jnp.einsum('bjhd,bjhe->bhde', k.astype(jnp.float32) * k_dec[None, :, :, None],
                    v.astype(jnp.float32))
    new_state = state * jnp.exp(log_gamma * C)[None, :, None, None] + kv
    return new_state, o


def retention_qkv(rq, rk, rv, pos):
    B, S = rq.shape[:2]
    q = rope(rq.reshape(B, S, RET_HEADS, RET_DK), pos)
    k = rope(rk.reshape(B, S, RET_HEADS, RET_DK), pos) * (RET_DK ** -0.5)
    v = rv.reshape(B, S, RET_HEADS, RET_DV)
    return q, k, v


def rel_bias_block(table, n_q, n_past):
    i = jnp.arange(n_q, dtype=jnp.int32)[:, None]
    j = jnp.arange(n_past + n_q, dtype=jnp.int32)[None, :]
    rel = (j - n_past) - i
    half = REL_BUCKETS // 2
    max_exact = half // 2
    n = jnp.abs(rel)
    large = max_exact + (jnp.log(jnp.maximum(n, 1).astype(jnp.float32) / max_exact)
                         / math.log(REL_MAX_DIST / max_exact) * (half - max_exact)).astype(jnp.int32)
    large = jnp.minimum(large, half - 1)
    bucket = jnp.where(rel > 0, half, 0) + jnp.where(n < max_exact, n, large)
    b = table[bucket]
    return jnp.transpose(b, (2, 0, 1)).reshape(SWA_KV_HEADS, SWA_GROUP, n_q, n_past + n_q)


def band_blocks(t):
    B, S = t.shape[:2]
    NC = S // CHUNK
    tp = jnp.pad(t, ((0, 0), (WINDOW, 0), (0, 0), (0, 0))).reshape(B, NC + WIN_CHUNKS, CHUNK, *t.shape[2:])
    return jnp.concatenate([tp[:, w:w + NC] for w in range(WIN_CHUNKS + 1)], axis=2)


def swa_attend(q, k, v, bias, sinks, valid):
    s = jnp.einsum('bnqhgd,bnkhd->bnhgqk', q, k, preferred_element_type=jnp.float32) * (SWA_HD ** -0.5)
    s = s + bias.astype(jnp.float32)
    if valid is not None:
        s = jnp.where(valid, s, NEG_INF)
    sink = sinks.astype(jnp.float32)[:, :, None, None]
    m = jnp.maximum(jnp.max(s, axis=-1, keepdims=True), sink)
    e = jnp.exp(s - m)
    p = e / (jnp.sum(e, axis=-1, keepdims=True) + jnp.exp(sink - m))
    return jnp.einsum('bnhgqk,bnkhd->bnqhgd', p.astype(v.dtype), v)


def in_proj(x, g_attn, w_in):
    return jnp.split(rms_norm(x, g_attn) @ w_in, SPLIT_OFFS, axis=-1)


def merge_branches(ret_o, ret_g, swa_o, gate_a, gate_b, w_ret_out, w_swa_out, w_mix_out):
    B, S = ret_g.shape[:2]
    a = (rms_norm(ret_o).astype(ret_g.dtype).reshape(B, S, -1) * jax.nn.silu(ret_g)) @ w_ret_out
    b = swa_o.reshape(B, S, -1) @ w_swa_out
    return (jax.nn.sigmoid(gate_a) * a + jax.nn.sigmoid(gate_b) * b) @ w_mix_out


def mixer_prompt(h, g_attn, w_in, w_ret_out, w_swa_out, w_mix_out, sinks, bias, log_gamma):
    B, S, _ = h.shape
    NC = S // CHUNK
    rq, rk, rv, rg, sq, sk, sv, ga, gb = in_proj(h, g_attn, w_in)
    pos = jnp.arange(S, dtype=jnp.int32)
    q, k, v = retention_qkv(rq, rk, rv, pos)
    blk = lambda t: jnp.swapaxes(t.reshape(B, NC, CHUNK, *t.shape[2:]), 0, 1)
    s0 = jnp.zeros((B, RET_HEADS, RET_DK, RET_DV), jnp.float32)
    ret_state, o = lax.scan(lambda st, xs: retention_block(st, xs[0], xs[1], xs[2], log_gamma),
                            s0, (blk(q), blk(k), blk(v)))
    ret_o = jnp.swapaxes(o, 0, 1).reshape(B, S, RET_HEADS, RET_DV)
    q_s = sq.reshape(B, NC, CHUNK, SWA_KV_HEADS, SWA_GROUP, SWA_HD)
    k_s = sk.reshape(B, S, SWA_KV_HEADS, SWA_HD)
    v_s = sv.reshape(B, S, SWA_KV_HEADS, SWA_HD)
    key_pos = (jnp.arange(NC, dtype=jnp.int32)[:, None] * CHUNK
               + jnp.arange(WINDOW + CHUNK, dtype=jnp.int32)[None, :] - WINDOW)
    valid = (key_pos >= 0)[None, :, None, None, None, :]
    swa_o = swa_attend(q_s, band_blocks(k_s), band_blocks(v_s), bias,
                       sinks.reshape(SWA_KV_HEADS, SWA_GROUP), valid)
    y = merge_branches(ret_o, rg, swa_o, ga, gb, w_ret_out, w_swa_out, w_mix_out)
    return y, ret_state, k_s[:, S - WINDOW:], v_s[:, S - WINDOW:]


def mixer_sample(h, ret_state, ck, cv, g_attn, w_in, w_ret_out, w_swa_out, w_mix_out, sinks, bias, log_gamma):
    B, T, _ = h.shape
    rq, rk, rv, rg, sq, sk, sv, ga, gb = in_proj(h, g_attn, w_in)
    pos = PAST_LEN + jnp.arange(T, dtype=jnp.int32)
    q, k, v = retention_qkv(rq, rk, rv, pos)
    new_state, ret_o = retention_block(ret_state.astype(jnp.float32), q, k, v, log_gamma)
    q_s = sq.reshape(B, 1, T, SWA_KV_HEADS, SWA_GROUP, SWA_HD)
    k_all = jnp.concatenate([ck, sk.reshape(B, T, SWA_KV_HEADS, SWA_HD).astype(ck.dtype)], axis=1)
    v_all = jnp.concatenate([cv, sv.reshape(B, T, SWA_KV_HEADS, SWA_HD).astype(cv.dtype)], axis=1)
    swa_o = swa_attend(q_s, k_all[:, None], v_all[:, None], bias,
                       sinks.reshape(SWA_KV_HEADS, SWA_GROUP), None)
    y = merge_branches(ret_o, rg, swa_o, ga, gb, w_ret_out, w_swa_out, w_mix_out)
    return y, new_state, k_all[:, T:], v_all[:, T:]


def mem_kv(mem, g_mem, w_mk, w_mv):
    B = mem.shape[0]
    mn = rms_norm(mem, g_mem)
    return ((mn @ w_mk).reshape(B, N_MEM, MEM_HEADS, MEM_HD),
            (mn @ w_mv).reshape(B, N_MEM, MEM_HEADS, MEM_HD))


def cross_attn(x, mk, mv, g_cross, w_cq, w_co):
    B, S, _ = x.shape
    q = (rms_norm(x, g_cross) @ w_cq).reshape(B, S, MEM_HEADS, MEM_HD)
    s = jnp.einsum('bqhd,bkhd->bhqk', q, mk, preferred_element_type=jnp.float32) * (MEM_HD ** -0.5)
    p = jax.nn.softmax(s, axis=-1).astype(mv.dtype)
    o = jnp.einsum('bhqk,bkhd->bqhd', p, mv).reshape(B, S, MEM_HEADS * MEM_HD)
    return o @ w_co


def swiglu(x, g_ffn, w_gate, w_up, w_down):
    hn = rms_norm(x, g_ffn)
    return (jax.nn.silu(hn @ w_gate) * (hn @ w_up)) @ w_down


def setup_inputs(seed: int = 0) -> dict:
    key = jax.random.key(seed)
    ks = jax.random.split(key, 26)

    def nrm(i, shape, scale=1.0):
        return jax.random.normal(ks[i], shape, jnp.float32) * scale

    def gain(i, shape):
        return 1.0 + 0.01 * jax.random.normal(ks[i], shape, jnp.float32)

    n_swa = min(WINDOW, PAST_LEN)
    L = DEPTH
    return {
        'x_prompt': nrm(0, (BATCH, SEQ, D_MODEL)),
        'x_sample': nrm(1, (DEC_BATCH, DEC_SEQ, D_MODEL)),
        'cache_ret_state': nrm(2, (L, DEC_BATCH, RET_HEADS, RET_DK, RET_DV), 0.1),
        'cache_swa_k': nrm(3, (L, DEC_BATCH, n_swa, SWA_KV_HEADS, SWA_HD)),
        'cache_swa_v': nrm(4, (L, DEC_BATCH, n_swa, SWA_KV_HEADS, SWA_HD)),
        'cache_mem_k': nrm(5, (L, DEC_BATCH, N_MEM, MEM_HEADS, MEM_HD)),
        'cache_mem_v': nrm(6, (L, DEC_BATCH, N_MEM, MEM_HEADS, MEM_HD)),
        'mem_prompt': nrm(7, (BATCH, N_MEM, D_MODEL)),
        'rel_bias': nrm(8, (REL_BUCKETS, SWA_HEADS), 0.1),
        'g_attn': gain(9, (L, D_MODEL)),
        'w_in': nrm(10, (L, D_MODEL, D_IN), D_MODEL ** -0.5),
        'w_ret_out': nrm(11, (L, RET_HEADS * RET_DV, D_MODEL), (RET_HEADS * RET_DV) ** -0.5),
        'w_swa_out': nrm(12, (L, SWA_HEADS * SWA_HD, D_MODEL), (SWA_HEADS * SWA_HD) ** -0.5),
        'w_mix_out': nrm(13, (L, D_MODEL, D_MODEL), D_MODEL ** -0.5),
        'swa_sinks': nrm(14, (L, SWA_HEADS)),
        'g_cross': gain(15, (L, D_MODEL)),
        'g_mem': gain(16, (L, D_MODEL)),
        'w_cq': nrm(17, (L, D_MODEL, MEM_HEADS * MEM_HD), D_MODEL ** -0.5),
        'w_mk': nrm(18, (L, D_MODEL, MEM_HEADS * MEM_HD), D_MODEL ** -0.5),
        'w_mv': nrm(19, (L, D_MODEL, MEM_HEADS * MEM_HD), D_MODEL ** -0.5),
        'w_co': nrm(20, (L, MEM_HEADS * MEM_HD, D_MODEL), (MEM_HEADS * MEM_HD) ** -0.5),
        'g_ffn': gain(21, (L, D_MODEL)),
        'w_gate': nrm(22, (L, D_MODEL, D_FF), D_MODEL ** -0.5),
        'w_up': nrm(23, (L, D_MODEL, D_FF), D_MODEL ** -0.5),
        'w_down': nrm(24, (L, D_FF, D_MODEL), D_FF ** -0.5),
        'g_final': gain(25, (D_MODEL,)),
    }


def reference(x_prompt, x_sample, cache_ret_state, cache_swa_k, cache_swa_v, cache_mem_k, cache_mem_v,
              mem_prompt, rel_bias, g_attn, w_in, w_ret_out, w_swa_out, w_mix_out, swa_sinks,
              g_cross, g_mem, w_cq, w_mk, w_mv, w_co, g_ffn, w_gate, w_up, w_down, g_final):
    log_gamma = retention_log_decay()
    T = x_sample.shape[1]
    bias_p = rel_bias_block(rel_bias, CHUNK, WINDOW)
    bias_s = rel_bias_block(rel_bias, T, cache_swa_k.shape[2])
    hp, hs = x_prompt, x_sample
    ret_p, ret_s, kp, ksm, vp, vsm, mkp, mvp = [], [], [], [], [], [], [], []
    for l in range(DEPTH):
        y, st, kb, vb = mixer_prompt(hp, g_attn[l], w_in[l], w_ret_out[l], w_swa_out[l], w_mix_out[l],
                                     swa_sinks[l], bias_p, log_gamma)
        hp = hp + y
        mk, mv = mem_kv(mem_prompt, g_mem[l], w_mk[l], w_mv[l])
        hp = hp + cross_attn(hp, mk, mv, g_cross[l], w_cq[l], w_co[l])
        hp = hp + swiglu(hp, g_ffn[l], w_gate[l], w_up[l], w_down[l])
        ret_p.append(st); kp.append(kb); vp.append(vb); mkp.append(mk); mvp.append(mv)
        y, st, kb, vb = mixer_sample(hs, cache_ret_state[l], cache_swa_k[l], cache_swa_v[l], g_attn[l], w_in[l],
                                     w_ret_out[l], w_swa_out[l], w_mix_out[l], swa_sinks[l], bias_s, log_gamma)
        hs = hs + y
        hs = hs + cross_attn(hs, cache_mem_k[l], cache_mem_v[l], g_cross[l], w_cq[l], w_co[l])
        hs = hs + swiglu(hs, g_ffn[l], w_gate[l], w_up[l], w_down[l])
        ret_s.append(st); ksm.append(kb); vsm.append(vb)
    y_prompt = rms_norm(hp, g_final)
    y_sample = rms_norm(hs, g_final)
    return (y_prompt, y_sample, jnp.stack(ret_p), jnp.stack(ret_s), jnp.stack(kp), jnp.stack(ksm),
            jnp.stack(vp), jnp.stack(vsm), jnp.stack(mkp), jnp.stack(mvp))
```

```python
import functools
import math

import numpy as np
import jax
import jax.numpy as jnp
from jax import lax
from jax.experimental import pallas as pl
from jax.experimental.pallas import tpu as pltpu

F32 = jnp.float32
BF16 = jnp.bfloat16

D_MODEL = 1024
CHUNK = 64
EPS = 1e-6
NEG_INF = -1e30
PAST_LEN = 4096

RET_HEADS = 4
RET_DK = 256
RET_DV = 256
ROPE_BASE = 10000.0

SWA_HEADS = 16
SWA_KV = 2
SWA_GROUP = 8
SWA_HD = 64
WINDOW = 128
SWA_L = WINDOW + CHUNK

REL_BUCKETS = 32
REL_MAX_DIST = 128

N_MEM = 256
MEM_HEADS = 4
MEM_HD = 256
D_FF = 2816

OFF_RQ, OFF_RK, OFF_RV, OFF_RG, OFF_SQ, OFF_SKV, OFF_GA, OFF_GB = 0, 1024, 2048, 3072, 4096, 5120, 5376, 6400
D_IN = 7424

LANES = 128
HALF = LANES // 2
VMEM_LIMIT = 52 * 1024 * 1024


def _resident(shape):
    nd = len(shape)
    return pl.BlockSpec(shape, lambda *_: (0,) * nd, pipeline_mode=pl.Buffered(1))


def _rms(x):
    return x * lax.rsqrt(jnp.mean(x * x, axis=-1, keepdims=True) + EPS)


def _in_proj_kernel(x_ref, g_ref, w_ref, cos_ref, sin_ref, kdec_ref,
                    rq_ref, rk_ref, rkd_ref, rv_ref, rgs_ref, sq_ref, gas_ref, gbs_ref, kv_ref):
    x = x_ref[...]
    xn = (_rms(x) * g_ref[...]).astype(BF16)
    cos = cos_ref[...]
    sin = sin_ref[...]

    def proj(c0, n):
        return jnp.dot(xn, w_ref[:, c0:c0 + n], preferred_element_type=F32)

    for h in range(RET_HEADS):
        c = h * RET_DK
        acc = proj(OFF_RQ + c, RET_DK)
        x1, x2 = acc[:, :LANES], acc[:, LANES:]
        rq_ref[:, c:c + LANES] = (x1 * cos - x2 * sin).astype(BF16)
        rq_ref[:, c + LANES:c + RET_DK] = (x2 * cos + x1 * sin).astype(BF16)
        acc = proj(OFF_RK + c, RET_DK)
        x1, x2 = acc[:, :LANES], acc[:, LANES:]
        k1 = (x1 * cos - x2 * sin) * (RET_DK ** -0.5)
        k2 = (x2 * cos + x1 * sin) * (RET_DK ** -0.5)
        rk_ref[:, c:c + LANES] = k1.astype(BF16)
        rk_ref[:, c + LANES:c + RET_DK] = k2.astype(BF16)
        kd = kdec_ref[:, c:c + LANES]
        rkd_ref[:, c:c + LANES] = (k1 * kd).astype(BF16)
        rkd_ref[:, c + LANES:c + RET_DK] = (k2 * kd).astype(BF16)
    for c in range(0, D_MODEL, 512):
        rv_ref[:, c:c + 512] = proj(OFF_RV + c, 512).astype(BF16)
        sq_ref[:, c:c + 512] = proj(OFF_SQ + c, 512).astype(BF16)
        rgs_ref[:, c:c + 512] = jax.nn.silu(proj(OFF_RG + c, 512)).astype(BF16)
        gas_ref[:, c:c + 512] = jax.nn.sigmoid(proj(OFF_GA + c, 512)).astype(BF16)
        gbs_ref[:, c:c + 512] = jax.nn.sigmoid(proj(OFF_GB + c, 512)).astype(BF16)
    kv_ref[...] = proj(OFF_SKV, 2 * SWA_KV * SWA_HD)


def _in_proj(x2d, g, w_in, cos_t, sin_t, kdec_t, tm):
    n = x2d.shape[0]
    assert n % tm == 0 and cos_t.shape[0] % tm == 0 and kdec_t.shape[0] % tm == 0
    ncos = cos_t.shape[0] // tm
    nkd = kdec_t.shape[0] // tm
    row = lambda i: (i, 0)
    wide = pl.BlockSpec((tm, D_MODEL), row)
    out_bf = jax.ShapeDtypeStruct((n, D_MODEL), BF16)
    return pl.pallas_call(
        _in_proj_kernel,
        grid=(n // tm,),
        in_specs=[
            wide,
            _resident((1, D_MODEL)),
            _resident((D_MODEL, D_IN)),
            pl.BlockSpec((tm, LANES), lambda i: (i % ncos, 0)),
            pl.BlockSpec((tm, LANES), lambda i: (i % ncos, 0)),
            pl.BlockSpec((tm, D_MODEL), lambda i: (i % nkd, 0)),
        ],
        out_specs=[wide] * 8 + [pl.BlockSpec((tm, 256), row)],
        out_shape=[out_bf] * 8 + [jax.ShapeDtypeStruct((n, 256), F32)],
        compiler_params=pltpu.CompilerParams(dimension_semantics=("parallel",), vmem_limit_bytes=VMEM_LIMIT),
        name="in_proj",
    )(x2d, g, w_in, cos_t, sin_t, kdec_t)


def _swa_col_maps():
    key = np.concatenate([np.arange(LANES), np.arange(LANES), LANES + np.arange(HALF), LANES + np.arange(HALF)])
    sub = np.concatenate([np.zeros(LANES), np.ones(LANES), np.zeros(HALF), np.ones(HALF)]).astype(np.int32)
    return key.astype(np.int32), sub


def _rel_bias_kernel(tab_ref, bucket_ref, sub_ref, out_ref):
    bucket = bucket_ref[...]
    is_b = sub_ref[...] > 0
    for hk in range(SWA_KV):
        for gp in range(SWA_GROUP // 2):
            h0 = hk * SWA_GROUP + gp * 2
            acc = jnp.zeros(bucket.shape, F32)
            for b in range(REL_BUCKETS):
                val = jnp.where(is_b, tab_ref[b, h0 + 1], tab_ref[b, h0])
                acc = jnp.where(bucket == b, val, acc)
            out_ref[hk, gp * CHUNK:(gp + 1) * CHUNK, :] = acc


def _rel_bias(table, bucket2, sub2):
    ncol = bucket2.shape[1]
    return pl.pallas_call(
        _rel_bias_kernel,
        in_specs=[pl.BlockSpec(memory_space=pltpu.SMEM),
                  pl.BlockSpec(memory_space=pltpu.VMEM),
                  pl.BlockSpec(memory_space=pltpu.VMEM)],
        out_specs=pl.BlockSpec(memory_space=pltpu.VMEM),
        out_shape=jax.ShapeDtypeStruct((SWA_KV, 4 * CHUNK, ncol), F32),
        name="rel_bias",
    )(table, bucket2, sub2)


def _mixers_kernel(rq_ref, rk_ref, rkd_ref, rv_ref, rgs_ref, sq_ref, gas_ref, gbs_ref, kv_ref, x_ref,
                   st0_ref, hist_ref, mt_ref, qd_ref, gr_ref, bias_ref, sink_ref,
                   wro_ref, wso_ref, wmo_ref,
                   h_ref, st_ref,
                   state, ka, kb, va, vb, ain, swo,
                   *, G, R, has_history):
    i = pl.program_id(1)
    last = pl.num_programs(1) - 1
    lane = lax.broadcasted_iota(jnp.int32, (1, LANES), 1)
    lo = lane < HALF

    def split_heads(t):
        r = pltpu.roll(t, HALF, axis=1)
        z = jnp.zeros_like(t)
        a0 = jnp.where(lo, t, z).astype(BF16)
        b0 = jnp.where(lo, z, r).astype(BF16)
        a1 = jnp.where(lo, r, z).astype(BF16)
        b1 = jnp.where(lo, z, t).astype(BF16)
        return (a0, a1), (b0, b1)

    @pl.when(i == 0)
    def _():
        state[...] = st0_ref[...]
        for g in range(G):
            hist = hist_ref[g]
            (a0, a1), (b0, b1) = split_heads(hist[:, :LANES])
            ka[g, 0, :WINDOW], ka[g, 1, :WINDOW], kb[g, 0, :WINDOW], kb[g, 1, :WINDOW] = a0, a1, b0, b1
            (a0, a1), (b0, b1) = split_heads(hist[:, LANES:])
            va[g, 0, :WINDOW], va[g, 1, :WINDOW], vb[g, 0, :WINDOW], vb[g, 1, :WINDOW] = a0, a1, b0, b1

    @pl.when(i > 0)
    def _():
        for buf in (ka, kb, va, vb):
            for g in range(G):
                for hk in range(SWA_KV):
                    buf[g, hk, :WINDOW] = buf[g, hk, R:R + WINDOW]

    for g in range(G):
        rows = slice(g * R, (g + 1) * R)
        kvt = kv_ref[rows, :]
        (a0, a1), (b0, b1) = split_heads(kvt[:, :LANES])
        ka[g, 0, WINDOW:], ka[g, 1, WINDOW:], kb[g, 0, WINDOW:], kb[g, 1, WINDOW:] = a0, a1, b0, b1
        (a0, a1), (b0, b1) = split_heads(kvt[:, LANES:])
        va[g, 0, WINDOW:], va[g, 1, WINDOW:], vb[g, 0, WINDOW:], vb[g, 1, WINDOW:] = a0, a1, b0, b1

    for g in range(G):
        rows = slice(g * R, (g + 1) * R)
        for h in range(RET_HEADS):
            cols = slice(h * RET_DK, (h + 1) * RET_DK)
            q = rq_ref[rows, cols]
            k = rk_ref[rows, cols]
            kd = rkd_ref[rows, cols]
            v = rv_ref[rows, cols]
            s_prev = state[g, h]
            sc = lax.dot_general(q, k, (((1,), (1,)), ((), ())), preferred_element_type=F32) * mt_ref[h]
            o = jnp.dot(sc.astype(BF16), v, preferred_element_type=F32)
            o = o + jnp.dot(q, s_prev.astype(BF16), preferred_element_type=F32) * qd_ref[h]
            kv_new = lax.dot_general(kd, v, (((0,), (0,)), ((), ())), preferred_element_type=F32)
            state[g, h] = s_prev * gr_ref[h] + kv_new
            ain[rows, cols] = (_rms(o) * rgs_ref[rows, cols].astype(F32)).astype(BF16)

    ones_a = jnp.where(lo, 1.0, 0.0).astype(BF16)
    ones_b = jnp.where(lo, 0.0, 1.0).astype(BF16)
    ones_a = jnp.broadcast_to(ones_a, (LANES, LANES))
    ones_b = jnp.broadcast_to(ones_b, (LANES, LANES))
    lane3 = lax.broadcasted_iota(jnp.int32, (1, 3 * LANES), 1)
    key_of_col = jnp.where(lane3 < 2 * LANES, lane3 % LANES, LANES + lane3 % HALF)
    n_chunks = R // CHUNK
    for g in range(G):
        for c in range(n_chunks):
            w0 = c * CHUNK
            r0 = g * R + c * CHUNK
            for hk in range(SWA_KV):
                q2 = jnp.concatenate(
                    [sq_ref[r0:r0 + CHUNK, hk * 512 + gp * LANES: hk * 512 + (gp + 1) * LANES] for gp in range(4)],
                    axis=0)
                kbd = jnp.concatenate([ka[g, hk, w0:w0 + LANES], kb[g, hk, w0:w0 + LANES],
                                       ka[g, hk, w0 + LANES:w0 + SWA_L], kb[g, hk, w0 + LANES:w0 + SWA_L]],
                                      axis=0)
                s = lax.dot_general(q2, kbd, (((1,), (1,)), ((), ())), preferred_element_type=F32)
                s = s * (SWA_HD ** -0.5) + bias_ref[hk]
                if not has_history and c < WINDOW // CHUNK:
                    kpos = key_of_col + (i * R + c * CHUNK - WINDOW)
                    s = jnp.where(kpos >= 0, s, NEG_INF)
                t0, t1, t2 = s[:, :LANES], s[:, LANES:2 * LANES], s[:, 2 * LANES:]
                m_a = jnp.max(jnp.maximum(t0, jnp.where(lo, t2, NEG_INF)), axis=-1, keepdims=True)
                m_b = jnp.max(jnp.maximum(t1, jnp.where(lo, NEG_INF, t2)), axis=-1, keepdims=True)
                sk_a, sk_b = [], []
                for gp in range(4):
                    sk_a.append(jnp.full((CHUNK, 1), sink_ref[hk * SWA_GROUP + 2 * gp], F32))
                    sk_b.append(jnp.full((CHUNK, 1), sink_ref[hk * SWA_GROUP + 2 * gp + 1], F32))
                sk_a = jnp.concatenate(sk_a, axis=0)
                sk_b = jnp.concatenate(sk_b, axis=0)
                m_a = jnp.maximum(m_a, sk_a)
                m_b = jnp.maximum(m_b, sk_b)
                e = jnp.concatenate([jnp.exp(t0 - m_a), jnp.exp(t1 - m_b),
                                     jnp.exp(t2 - jnp.where(lo, m_a, m_b))], axis=1).astype(BF16)
                vbd = jnp.concatenate([
                    jnp.concatenate([va[g, hk, w0:w0 + LANES], ones_a], axis=1),
                    jnp.concatenate([vb[g, hk, w0:w0 + LANES], ones_b], axis=1),
                    jnp.concatenate([va[g, hk, w0 + LANES:w0 + SWA_L], ones_a[:HALF]], axis=1),
                    jnp.concatenate([vb[g, hk, w0 + LANES:w0 + SWA_L], ones_b[:HALF]], axis=1)], axis=0)
                oa = jnp.dot(e, vbd, preferred_element_type=F32)
                den = oa[:, LANES:] + jnp.where(lo, jnp.exp(sk_a - m_a), jnp.exp(sk_b - m_b))
                o2 = (oa[:, :LANES] / den).astype(BF16)
                for gp in range(4):
                    swo[r0:r0 + CHUNK, hk * 512 + gp * LANES: hk * 512 + (gp + 1) * LANES] = \
                        o2[gp * CHUNK:(gp + 1) * CHUNK]

    a = jnp.dot(ain[...], wro_ref[...], preferred_element_type=F32)
    b = jnp.dot(swo[...], wso_ref[...], preferred_element_type=F32)
    mix = (gas_ref[...].astype(F32) * a + gbs_ref[...].astype(F32) * b).astype(BF16)
    h_ref[...] = x_ref[...] + jnp.dot(mix, wmo_ref[...], preferred_element_type=F32)

    @pl.when(i == last)
    def _():
        st_ref[...] = state[...]


def _mixers(proj, x2d, st0, hist, mt, qd, gr, bias2, sinks, wro, wso, wmo, *, G, R, n_tiles, has_history):
    n = x2d.shape[0]
    tm = G * R
    n_outer = n // (tm * n_tiles)
    assert n_outer * tm * n_tiles == n and (G == 1 or n_tiles == 1)
    row = lambda o, i: (o * n_tiles + i, 0)
    wide = pl.BlockSpec((tm, D_MODEL), row)
    kern = functools.partial(_mixers_kernel, G=G, R=R, has_history=has_history)
    return pl.pallas_call(
        kern,
        grid=(n_outer, n_tiles),
        in_specs=[wide] * 8 + [pl.BlockSpec((tm, 256), row), wide,
                               pl.BlockSpec((G, RET_HEADS, RET_DK, RET_DV), lambda o, i: (o, 0, 0, 0)),
                               pl.BlockSpec((G, WINDOW, 256), lambda o, i: (o, 0, 0)),
                               _resident(mt.shape), _resident(qd.shape),
                               pl.BlockSpec(memory_space=pltpu.SMEM),
                               _resident(bias2.shape),
                               pl.BlockSpec(memory_space=pltpu.SMEM),
                               _resident(wro.shape), _resident(wso.shape), _resident(wmo.shape)],
        out_specs=[wide, pl.BlockSpec((G, RET_HEADS, RET_DK, RET_DV), lambda o, i: (o, 0, 0, 0))],
        out_shape=[jax.ShapeDtypeStruct((n, D_MODEL), F32),
                   jax.ShapeDtypeStruct((st0.shape[0], RET_HEADS, RET_DK, RET_DV), F32)],
        scratch_shapes=[pltpu.VMEM((G, RET_HEADS, RET_DK, RET_DV), F32)]
                       + [pltpu.VMEM((G, SWA_KV, WINDOW + R, LANES), BF16)] * 4
                       + [pltpu.VMEM((tm, D_MODEL), BF16)] * 2,
        compiler_params=pltpu.CompilerParams(dimension_semantics=("parallel", "arbitrary"),
                                             vmem_limit_bytes=VMEM_LIMIT),
        name="mixers",
    )(*proj, x2d, st0, hist, mt, qd, gr, bias2, sinks, wro, wso, wmo)


def _tail_kernel(h_ref, mk_ref, mv_ref, gc_ref, gf_ref, gl_ref, wcq_ref, wco_ref, wg_ref, wu_ref, wd_ref,
                 y_ref, att, *, G, R):
    h1 = h_ref[...]
    q = jnp.dot((_rms(h1) * gc_ref[...]).astype(BF16), wcq_ref[...], preferred_element_type=F32).astype(BF16)
    for g in range(G):
        rows = slice(g * R, (g + 1) * R)
        for hd in range(MEM_HEADS):
            cols = slice(hd * MEM_HD, (hd + 1) * MEM_HD)
            s = lax.dot_general(q[rows, cols], mk_ref[g, :, cols], (((1,), (1,)), ((), ())),
                                preferred_element_type=F32) * (MEM_HD ** -0.5)
            e = jnp.exp(s - jnp.max(s, axis=-1, keepdims=True))
            den = jnp.sum(e, axis=-1, keepdims=True)
            o = jnp.dot(e.astype(BF16), mv_ref[g, :, cols], preferred_element_type=F32) / den
            att[rows, cols] = o.astype(BF16)
    h2 = h1 + jnp.dot(att[...], wco_ref[...], preferred_element_type=F32)
    hn = (_rms(h2) * gf_ref[...]).astype(BF16)
    gate = jnp.dot(hn, wg_ref[...], preferred_element_type=F32)
    up = jnp.dot(hn, wu_ref[...], preferred_element_type=F32)
    act = (jax.nn.silu(gate) * up).astype(BF16)
    h3 = h2 + jnp.dot(act, wd_ref[...], preferred_element_type=F32)
    y_ref[...] = _rms(h3) * gl_ref[...]


def _tail(h2d, mk, mv, gc, gf, gl, wcq, wco, wg, wu, wd, *, G, R, tiles_per_mem):
    n = h2d.shape[0]
    tm = G * R
    assert n % tm == 0 and (G == 1 or tiles_per_mem == 1)
    row = lambda i: (i, 0)
    mem = pl.BlockSpec((G, N_MEM, D_MODEL), lambda i: (i // tiles_per_mem, 0, 0))
    vec = _resident((1, D_MODEL))
    kern = functools.partial(_tail_kernel, G=G, R=R)
    return pl.pallas_call(
        kern,
        grid=(n // tm,),
        in_specs=[pl.BlockSpec((tm, D_MODEL), row), mem, mem, vec, vec, vec,
                  _resident(wcq.shape), _resident(wco.shape), _resident(wg.shape), _resident(wu.shape),
                  _resident(wd.shape)],
        out_specs=pl.BlockSpec((tm, D_MODEL), row),
        out_shape=jax.ShapeDtypeStruct((n, D_MODEL), F32),
        scratch_shapes=[pltpu.VMEM((tm, D_MODEL), BF16)],
        compiler_params=pltpu.CompilerParams(dimension_semantics=("parallel",), vmem_limit_bytes=VMEM_LIMIT),
        name="tail",
    )(h2d, mk, mv, gc, gf, gl, wcq, wco, wg, wu, wd)


def _mem_kv_kernel(m_ref, g_ref, wk_ref, wv_ref, k_ref, v_ref, kb_ref, vb_ref):
    mn = (_rms(m_ref[...]) * g_ref[...]).astype(BF16)
    k = jnp.dot(mn, wk_ref[...], preferred_element_type=F32)
    v = jnp.dot(mn, wv_ref[...], preferred_element_type=F32)
    k_ref[...] = k
    v_ref[...] = v
    kb_ref[...] = k.astype(BF16)
    vb_ref[...] = v.astype(BF16)


def _mem_kv(mem2d, g, wk, wv, tm):
    n = mem2d.shape[0]
    assert n % tm == 0
    blk = pl.BlockSpec((tm, D_MODEL), lambda i: (i, 0))
    return pl.pallas_call(
        _mem_kv_kernel,
        grid=(n // tm,),
        in_specs=[blk, _resident((1, D_MODEL)), _resident(wk.shape), _resident(wv.shape)],
        out_specs=[blk] * 4,
        out_shape=[jax.ShapeDtypeStruct((n, D_MODEL), F32)] * 2 + [jax.ShapeDtypeStruct((n, D_MODEL), BF16)] * 2,
        compiler_params=pltpu.CompilerParams(dimension_semantics=("parallel",), vmem_limit_bytes=VMEM_LIMIT),
        name="mem_kv",
    )(mem2d, g, wk, wv)


def _log_gamma():
    return jnp.log(1.0 - 2.0 ** (-5.0 - jnp.arange(RET_HEADS, dtype=F32)))


def _rope_tables(pos):
    half = RET_DK // 2
    inv = ROPE_BASE ** (-jnp.arange(half, dtype=F32) / half)
    ang = pos.astype(F32)[:, None] * inv[None, :]
    return jnp.cos(ang), jnp.sin(ang)


def _retention_tables(R):
    lg = _log_gamma()[:, None, None]
    idx = jnp.arange(R, dtype=F32)
    diff = idx[:, None] - idx[None, :]
    cn = (jnp.arange(R) // CHUNK)[:, None]
    cm = (jnp.arange(R) // CHUNK)[None, :]
    mask = jnp.where(cm == cn, jnp.exp(lg * jnp.abs(diff)), jnp.where(cm < cn, jnp.exp(lg * diff), 0.0))
    qd = jnp.exp(_log_gamma()[:, None] * (idx[None, :] + 1.0))
    kd = jnp.exp(_log_gamma()[:, None] * (R - 1.0 - idx[None, :]))
    gr = jnp.exp(_log_gamma() * R)
    qd_full = jnp.broadcast_to(qd[:, :, None], (RET_HEADS, R, RET_DV))
    kd_full = jnp.broadcast_to(kd.T[:, :, None], (R, RET_HEADS, RET_DK)).reshape(R, RET_HEADS * RET_DK)
    return mask.astype(F32), qd_full, kd_full, gr


def _rel_bucket():
    i = jnp.arange(CHUNK, dtype=jnp.int32)[:, None]
    j = jnp.arange(SWA_L, dtype=jnp.int32)[None, :]
    rel = (j - WINDOW) - i
    half = REL_BUCKETS // 2
    max_exact = half // 2
    n = jnp.abs(rel)
    large = max_exact + (jnp.log(jnp.maximum(n, 1).astype(F32) / max_exact)
                         / math.log(REL_MAX_DIST / max_exact) * (half - max_exact)).astype(jnp.int32)
    large = jnp.minimum(large, half - 1)
    return jnp.where(rel > 0, half, 0) + jnp.where(n < max_exact, n, large)


def _tile_rows(t, tm):
    reps = -(-tm // t.shape[0])
    return jnp.tile(t, (reps, 1)) if reps > 1 else t


R_PROMPT = 256
TM_PROJ = 512


def kernel(x_prompt, x_sample, cache_ret_state, cache_swa_k, cache_swa_v, cache_mem_k, cache_mem_v, mem_prompt,
           rel_bias, g_attn, w_in, w_ret_out, w_swa_out, w_mix_out, swa_sinks, g_cross, g_mem, w_cq, w_mk, w_mv,
           w_co, g_ffn, w_gate, w_up, w_down, g_final):
    B, S, D = x_prompt.shape
    Bs, T, _ = x_sample.shape
    assert D == D_MODEL and T == CHUNK and S % R_PROMPT == 0 and cache_swa_k.shape[2] == WINDOW
    assert g_attn.shape[0] == 1, "single layer"
    bf = lambda w: w.astype(BF16)
    vec = lambda g: g.reshape(1, D_MODEL)
    w_in_b, wro, wso, wmo = bf(w_in[0]), bf(w_ret_out[0]), bf(w_swa_out[0]), bf(w_mix_out[0])
    wcq, wco, wmk, wmv = bf(w_cq[0]), bf(w_co[0]), bf(w_mk[0]), bf(w_mv[0])
    wg, wu, wd = bf(w_gate[0]), bf(w_up[0]), bf(w_down[0])

    key_of_col, sub_of_col = _swa_col_maps()
    bucket2 = _rel_bucket()[:, key_of_col]
    bias2 = _rel_bias(rel_bias, bucket2, jnp.asarray(sub_of_col)[None, :])
    sinks = swa_sinks[0]

    def layer(x, pos, R, G, st0, hist, mk, mv, has_history):
        nb, seq, _ = x.shape
        x2d = x.reshape(nb * seq, D_MODEL)
        n = x2d.shape[0]
        tm = min(TM_PROJ, n)
        cos_t, sin_t = _rope_tables(pos)
        mt, qd, kd, gr = _retention_tables(R)
        proj = _in_proj(x2d, vec(g_attn[0]), w_in_b, _tile_rows(cos_t, tm), _tile_rows(sin_t, tm),
                        _tile_rows(kd, tm), tm)
        h1, st = _mixers(proj, x2d, st0, hist, mt, qd, gr, bias2, sinks, wro, wso, wmo,
                         G=G, R=R, n_tiles=seq // R, has_history=has_history)
        y = _tail(h1, mk, mv, vec(g_cross[0]), vec(g_ffn[0]), vec(g_final), wcq, wco, wg, wu, wd,
                  G=G, R=R, tiles_per_mem=seq // R)
        return y.reshape(nb, seq, D_MODEL), st, proj[8].reshape(nb, seq, 256)

    mk_f, mv_f, mk_b, mv_b = _mem_kv(mem_prompt.reshape(B * N_MEM, D_MODEL), vec(g_mem[0]), wmk, wmv, 512)
    y_p, st_p, kv_p = layer(
        x_prompt, jnp.arange(S, dtype=jnp.int32), R_PROMPT, 1,
        jnp.zeros((B, RET_HEADS, RET_DK, RET_DV), F32), jnp.zeros((B, WINDOW, 256), F32),
        mk_b.reshape(B, N_MEM, D_MODEL), mv_b.reshape(B, N_MEM, D_MODEL), False)

    Gs = 4 if Bs % 4 == 0 else 1
    hist_s = jnp.concatenate([cache_swa_k[0].reshape(Bs, WINDOW, LANES), cache_swa_v[0].reshape(Bs, WINDOW, LANES)],
                             axis=-1)
    y_s, st_s, kv_s = layer(
        x_sample, PAST_LEN + jnp.arange(T, dtype=jnp.int32), CHUNK, Gs,
        cache_ret_state[0].astype(F32), hist_s,
        bf(cache_mem_k[0].reshape(Bs, N_MEM, D_MODEL)), bf(cache_mem_v[0].reshape(Bs, N_MEM, D_MODEL)), True)

    kvshape = (1, B, WINDOW, SWA_KV, SWA_HD)
    k_p = kv_p[:, S - WINDOW:, :LANES].reshape(kvshape)
    v_p = kv_p[:, S - WINDOW:, LANES:].reshape(kvshape)
    k_s = jnp.concatenate([cache_swa_k[0][:, T:], kv_s[:, :, :LANES].reshape(Bs, T, SWA_KV, SWA_HD)], axis=1)[None]
    v_s = jnp.concatenate([cache_swa_v[0][:, T:], kv_s[:, :, LANES:].reshape(Bs, T, SWA_KV, SWA_HD)], axis=1)[None]
    mem_shape = (1, B, N_MEM, MEM_HEADS, MEM_HD)
    return (y_p, y_s, st_p[None], st_s[None], k_p, k_s, v_p, v_s, mk_f.reshape(mem_shape), mv_f.reshape(mem_shape))
```

```python
import functools
import math

import numpy as np
import jax
import jax.numpy as jnp
from jax import lax
from jax.experimental import pallas as pl
from jax.experimental.pallas import tpu as pltpu

F32 = jnp.float32
BF16 = jnp.bfloat16

D_MODEL = 1024
CHUNK = 64
EPS = 1e-6
NEG_INF = -1e30
PAST_LEN = 4096

RET_HEADS = 4
RET_DK = 256
RET_DV = 256
ROPE_BASE = 10000.0

SWA_HEADS = 16
SWA_KV = 2
SWA_GROUP = 8
SWA_HD = 64
WINDOW = 128
SWA_L = WINDOW + CHUNK

REL_BUCKETS = 32
REL_MAX_DIST = 128

N_MEM = 256
MEM_HEADS = 4
MEM_HD = 256
D_FF = 2816

OFF_RQ, OFF_RK, OFF_RV, OFF_RG, OFF_SQ, OFF_SKV, OFF_GA, OFF_GB = 0, 1024, 2048, 3072, 4096, 5120, 5376, 6400
D_IN = 7424

LANES = 128
HALF = LANES // 2
VMEM_LIMIT = 52 * 1024 * 1024


def _resident(shape):
    nd = len(shape)
    return pl.BlockSpec(shape, lambda *_: (0,) * nd, pipeline_mode=pl.Buffered(1))


def _rms(x):
    return x * lax.rsqrt(jnp.mean(x * x, axis=-1, keepdims=True) + EPS)


def _in_proj_kernel(x_ref, g_ref, w_ref, cos_ref, sin_ref, kdec_ref,
                    rq_ref, rk_ref, rkd_ref, rv_ref, rgs_ref, sq_ref, gas_ref, gbs_ref, kv_ref):
    x = x_ref[...]
    xn = (_rms(x) * g_ref[...]).astype(BF16)
    cos = cos_ref[...]
    sin = sin_ref[...]

    def proj(c0, n):
        return jnp.dot(xn, w_ref[:, c0:c0 + n], preferred_element_type=F32)

    for h in range(RET_HEADS):
        c = h * RET_DK
        acc = proj(OFF_RQ + c, RET_DK)
        x1, x2 = acc[:, :LANES], acc[:, LANES:]
        rq_ref[:, c:c + LANES] = (x1 * cos - x2 * sin).astype(BF16)
        rq_ref[:, c + LANES:c + RET_DK] = (x2 * cos + x1 * sin).astype(BF16)
        acc = proj(OFF_RK + c, RET_DK)
        x1, x2 = acc[:, :LANES], acc[:, LANES:]
        k1 = (x1 * cos - x2 * sin) * (RET_DK ** -0.5)
        k2 = (x2 * cos + x1 * sin) * (RET_DK ** -0.5)
        rk_ref[:, c:c + LANES] = k1.astype(BF16)
        rk_ref[:, c + LANES:c + RET_DK] = k2.astype(BF16)
        kd = kdec_ref[:, c:c + LANES]
        rkd_ref[:, c:c + LANES] = (k1 * kd).astype(BF16)
        rkd_ref[:, c + LANES:c + RET_DK] = (k2 * kd).astype(BF16)
    for c in range(0, D_MODEL, 512):
        rv_ref[:, c:c + 512] = proj(OFF_RV + c, 512).astype(BF16)
        sq_ref[:, c:c + 512] = proj(OFF_SQ + c, 512).astype(BF16)
        rgs_ref[:, c:c + 512] = jax.nn.silu(proj(OFF_RG + c, 512)).astype(BF16)
        gas_ref[:, c:c + 512] = jax.nn.sigmoid(proj(OFF_GA + c, 512)).astype(BF16)
        gbs_ref[:, c:c + 512] = jax.nn.sigmoid(proj(OFF_GB + c, 512)).astype(BF16)
    kv_ref[...] = proj(OFF_SKV, 2 * SWA_KV * SWA_HD)


def _in_proj(x2d, g, w_in, cos_t, sin_t, kdec_t, tm):
    n = x2d.shape[0]
    assert n % tm == 0 and cos_t.shape[0] % tm == 0 and kdec_t.shape[0] % tm == 0
    ncos = cos_t.shape[0] // tm
    nkd = kdec_t.shape[0] // tm
    row = lambda i: (i, 0)
    wide = pl.BlockSpec((tm, D_MODEL), row)
    out_bf = jax.ShapeDtypeStruct((n, D_MODEL), BF16)
    return pl.pallas_call(
        _in_proj_kernel,
        grid=(n // tm,),
        in_specs=[
            wide,
            _resident((1, D_MODEL)),
            _resident((D_MODEL, D_IN)),
            pl.BlockSpec((tm, LANES), lambda i: (i % ncos, 0)),
            pl.BlockSpec((tm, LANES), lambda i: (i % ncos, 0)),
            pl.BlockSpec((tm, D_MODEL), lambda i: (i % nkd, 0)),
        ],
        out_specs=[wide] * 8 + [pl.BlockSpec((tm, 256), row)],
        out_shape=[out_bf] * 8 + [jax.ShapeDtypeStruct((n, 256), F32)],
        compiler_params=pltpu.CompilerParams(dimension_semantics=("parallel",), vmem_limit_bytes=VMEM_LIMIT),
        name="in_proj",
    )(x2d, g, w_in, cos_t, sin_t, kdec_t)


def _swa_col_maps():
    key = np.concatenate([np.arange(LANES), np.arange(LANES), LANES + np.arange(HALF), LANES + np.arange(HALF)])
    sub = np.concatenate([np.zeros(LANES), np.ones(LANES), np.zeros(HALF), np.ones(HALF)]).astype(np.int32)
    return key.astype(np.int32), sub


def _rel_bias_kernel(tab_ref, bucket_ref, sub_ref, out_ref):
    bucket = bucket_ref[...]
    is_b = sub_ref[...] > 0
    for hk in range(SWA_KV):
        for gp in range(SWA_GROUP // 2):
            h0 = hk * SWA_GROUP + gp * 2
            acc = jnp.zeros(bucket.shape, F32)
            for b in range(REL_BUCKETS):
                val = jnp.where(is_b, tab_ref[b, h0 + 1], tab_ref[b, h0])
                acc = jnp.where(bucket == b, val, acc)
            out_ref[hk, gp * CHUNK:(gp + 1) * CHUNK, :] = acc


def _rel_bias(table, bucket2, sub2):
    ncol = bucket2.shape[1]
    return pl.pallas_call(
        _rel_bias_kernel,
        in_specs=[pl.BlockSpec(memory_space=pltpu.SMEM),
                  pl.BlockSpec(memory_space=pltpu.VMEM),
                  pl.BlockSpec(memory_space=pltpu.VMEM)],
        out_specs=pl.BlockSpec(memory_space=pltpu.VMEM),
        out_shape=jax.ShapeDtypeStruct((SWA_KV, 4 * CHUNK, ncol), F32),
        name="rel_bias",
    )(table, bucket2, sub2)


def _mixers_kernel(rq_ref, rk_ref, rkd_ref, rv_ref, rgs_ref, sq_ref, gas_ref, gbs_ref, kv_ref, x_ref,
                   st0_ref, hist_ref, mt_ref, qd_ref, gr_ref, bias_ref, sink_ref,
                   wro_ref, wso_ref, wmo_ref,
                   h_ref, st_ref,
                   state, ka, kb, va, vb, ain, swo, mixs,
                   *, G, R, n_tiles, has_history):
    t = pl.program_id(0)
    n_real = pl.num_programs(0) - 1
    i = lax.rem(jnp.minimum(t, n_real - 1), n_tiles)
    wslot = lax.rem(t, 2)
    rslot = 1 - wslot
    lane = lax.broadcasted_iota(jnp.int32, (1, LANES), 1)
    lo = lane < HALF

    def split_heads(t):
        r = pltpu.roll(t, HALF, axis=1)
        z = jnp.zeros_like(t)
        a0 = jnp.where(lo, t, z).astype(BF16)
        b0 = jnp.where(lo, z, r).astype(BF16)
        a1 = jnp.where(lo, r, z).astype(BF16)
        b1 = jnp.where(lo, z, t).astype(BF16)
        return (a0, a1), (b0, b1)

    @pl.when(t == 0)
    def _():
        ain[1] = jnp.zeros(ain.shape[1:], BF16)
        swo[1] = jnp.zeros(swo.shape[1:], BF16)

    @pl.when(i == 0)
    def _():
        state[...] = st0_ref[...]
        for g in range(G):
            hist = hist_ref[g]
            tail = slice(R, R + WINDOW)
            (a0, a1), (b0, b1) = split_heads(hist[:, :LANES])
            ka[g, 0, tail], ka[g, 1, tail], kb[g, 0, tail], kb[g, 1, tail] = a0, a1, b0, b1
            (a0, a1), (b0, b1) = split_heads(hist[:, LANES:])
            va[g, 0, tail], va[g, 1, tail], vb[g, 0, tail], vb[g, 1, tail] = a0, a1, b0, b1

    for g in range(G):
        rows = slice(g * R, (g + 1) * R)
        for buf in (ka, kb, va, vb):
            for hk in range(SWA_KV):
                buf[g, hk, :WINDOW] = buf[g, hk, R:R + WINDOW]
        kvt = kv_ref[rows, :]
        (a0, a1), (b0, b1) = split_heads(kvt[:, :LANES])
        ka[g, 0, WINDOW:], ka[g, 1, WINDOW:], kb[g, 0, WINDOW:], kb[g, 1, WINDOW:] = a0, a1, b0, b1
        (a0, a1), (b0, b1) = split_heads(kvt[:, LANES:])
        va[g, 0, WINDOW:], va[g, 1, WINDOW:], vb[g, 0, WINDOW:], vb[g, 1, WINDOW:] = a0, a1, b0, b1

    def ret_front(g, h):
        rows = slice(g * R, (g + 1) * R)
        cols = slice(h * RET_DK, (h + 1) * RET_DK)
        q = rq_ref[rows, cols]
        v = rv_ref[rows, cols]
        s_prev = state[g, h]
        sc = lax.dot_general(q, rk_ref[rows, cols], (((1,), (1,)), ((), ())), preferred_element_type=F32)
        qs = jnp.dot(q, s_prev.astype(BF16), preferred_element_type=F32)
        kv_new = lax.dot_general(rkd_ref[rows, cols], v, (((0,), (0,)), ((), ())), preferred_element_type=F32)
        return sc, qs, kv_new, s_prev, v

    def ret_back(g, h, vals):
        sc, qs, kv_new, s_prev, v = vals
        rows = slice(g * R, (g + 1) * R)
        cols = slice(h * RET_DK, (h + 1) * RET_DK)
        o = jnp.dot((sc * mt_ref[h]).astype(BF16), v, preferred_element_type=F32) + qs * qd_ref[h]
        state[g, h] = s_prev * gr_ref[h] + kv_new
        ain[wslot, rows, cols] = (_rms(o) * rgs_ref[rows, cols].astype(F32)).astype(BF16)

    ones_a = jnp.broadcast_to(jnp.where(lo, 1.0, 0.0).astype(BF16), (LANES, LANES))
    ones_b = jnp.broadcast_to(jnp.where(lo, 0.0, 1.0).astype(BF16), (LANES, LANES))
    lane3 = lax.broadcasted_iota(jnp.int32, (1, 3 * LANES), 1)
    key_of_col = jnp.where(lane3 < 2 * LANES, lane3 % LANES, LANES + lane3 % HALF)

    def swa_front(g, c, hk):
        w0 = c * CHUNK
        r0 = g * R + c * CHUNK
        q2 = jnp.concatenate(
            [sq_ref[r0:r0 + CHUNK, hk * 512 + gp * LANES: hk * 512 + (gp + 1) * LANES] for gp in range(4)],
            axis=0)
        kbd = jnp.concatenate([ka[g, hk, w0:w0 + LANES], kb[g, hk, w0:w0 + LANES],
                               ka[g, hk, w0 + LANES:w0 + SWA_L], kb[g, hk, w0 + LANES:w0 + SWA_L]],
                              axis=0)
        return lax.dot_general(q2, kbd, (((1,), (1,)), ((), ())), preferred_element_type=F32)

    def swa_back(g, c, hk, s):
        w0 = c * CHUNK
        r0 = g * R + c * CHUNK
        s = s * (SWA_HD ** -0.5) + bias_ref[hk]
        if not has_history and c < WINDOW // CHUNK:
            kpos = key_of_col + (i * R + c * CHUNK - WINDOW)
            s = jnp.where(kpos >= 0, s, NEG_INF)
        t0, t1, t2 = s[:, :LANES], s[:, LANES:2 * LANES], s[:, 2 * LANES:]
        m_a = jnp.max(jnp.maximum(t0, jnp.where(lo, t2, NEG_INF)), axis=-1, keepdims=True)
        m_b = jnp.max(jnp.maximum(t1, jnp.where(lo, NEG_INF, t2)), axis=-1, keepdims=True)
        sk_a = jnp.concatenate([jnp.full((CHUNK, 1), sink_ref[hk * SWA_GROUP + 2 * gp], F32) for gp in range(4)],
                               axis=0)
        sk_b = jnp.concatenate([jnp.full((CHUNK, 1), sink_ref[hk * SWA_GROUP + 2 * gp + 1], F32) for gp in range(4)],
                               axis=0)
        m_a = jnp.maximum(m_a, sk_a)
        m_b = jnp.maximum(m_b, sk_b)
        e = jnp.concatenate([jnp.exp(t0 - m_a), jnp.exp(t1 - m_b),
                             jnp.exp(t2 - jnp.where(lo, m_a, m_b))], axis=1).astype(BF16)
        vbd = jnp.concatenate([
            jnp.concatenate([va[g, hk, w0:w0 + LANES], ones_a], axis=1),
            jnp.concatenate([vb[g, hk, w0:w0 + LANES], ones_b], axis=1),
            jnp.concatenate([va[g, hk, w0 + LANES:w0 + SWA_L], ones_a[:HALF]], axis=1),
            jnp.concatenate([vb[g, hk, w0 + LANES:w0 + SWA_L], ones_b[:HALF]], axis=1)], axis=0)
        oa = jnp.dot(e, vbd, preferred_element_type=F32)
        den = oa[:, LANES:] + jnp.where(lo, jnp.exp(sk_a - m_a), jnp.exp(sk_b - m_b))
        o2 = (oa[:, :LANES] / den).astype(BF16)
        for gp in range(4):
            swo[wslot, r0:r0 + CHUNK, hk * 512 + gp * LANES: hk * 512 + (gp + 1) * LANES] = \
                o2[gp * CHUNK:(gp + 1) * CHUNK]

    def merge_ab(j):
        cols = slice(j * MERGE_COLS, (j + 1) * MERGE_COLS)
        a = jnp.dot(ain[rslot], wro_ref[:, cols], preferred_element_type=F32)
        b = jnp.dot(swo[rslot], wso_ref[:, cols], preferred_element_type=F32)
        mixs[:, cols] = (gas_ref[:, cols].astype(F32) * a + gbs_ref[:, cols].astype(F32) * b).astype(BF16)

    def merge_y(j):
        cols = slice(j * MERGE_COLS, (j + 1) * MERGE_COLS)
        h_ref[:, cols] = x_ref[:, cols] + jnp.dot(mixs[...], wmo_ref[:, cols], preferred_element_type=F32)

    items = [(ret_front, ret_back, (g, h)) for g in range(G) for h in range(RET_HEADS)]
    items += [(swa_front, swa_back, (g, c, hk)) for g in range(G) for c in range(R // CHUNK) for hk in range(SWA_KV)]
    tasks = []
    for n in range(len(items) + PIPE_DEPTH):
        if n < len(items):
            tasks.append(("front", n))
        if n >= PIPE_DEPTH:
            tasks.append(("back", n - PIPE_DEPTH))
    n_chunks = D_MODEL // MERGE_COLS
    merges = [functools.partial(merge_ab, j) for j in range(n_chunks)]
    merges += [functools.partial(merge_y, j) for j in range(n_chunks)]
    every = max(1, len(tasks) // len(merges))
    pending = {}
    for pos, (kind, n) in enumerate(tasks):
        front, back, args = items[n]
        if kind == "front":
            pending[n] = front(*args)
        else:
            back(*args, pending.pop(n))
        if (pos + 1) % every == 0 and merges:
            merges.pop(0)()
    for m in merges:
        m()

    @pl.when((i == n_tiles - 1) & (t < n_real))
    def _():
        st_ref[...] = state[...]


def _mixers(proj, x2d, st0, hist, mt, qd, gr, bias2, sinks, wro, wso, wmo, *, G, R, n_tiles, has_history):
    n = x2d.shape[0]
    tm = G * R
    n_steps = n // tm
    assert n_steps * tm == n and n_steps % n_tiles == 0 and (G == 1 or n_tiles == 1)
    att_tile = lambda t: jnp.minimum(t, n_steps - 1)
    att = pl.BlockSpec((tm, D_MODEL), lambda t: (att_tile(t), 0))
    mrg = pl.BlockSpec((tm, D_MODEL), lambda t: (jnp.maximum(t - 1, 0), 0))
    per_row = pl.BlockSpec((G, RET_HEADS, RET_DK, RET_DV), lambda t: (att_tile(t) // n_tiles, 0, 0, 0))
    kern = functools.partial(_mixers_kernel, G=G, R=R, n_tiles=n_tiles, has_history=has_history)
    rq, rk, rkd, rv, rgs, sq, gas, gbs, kv = proj
    return pl.pallas_call(
        kern,
        grid=(n_steps + 1,),
        in_specs=[att] * 6 + [mrg, mrg, pl.BlockSpec((tm, 256), lambda t: (att_tile(t), 0)), mrg,
                              per_row,
                              pl.BlockSpec((G, WINDOW, 256), lambda t: (att_tile(t) // n_tiles, 0, 0)),
                              _resident(mt.shape), _resident(qd.shape),
                              pl.BlockSpec(memory_space=pltpu.SMEM),
                              _resident(bias2.shape),
                              pl.BlockSpec(memory_space=pltpu.SMEM),
                              _resident(wro.shape), _resident(wso.shape), _resident(wmo.shape)],
        out_specs=[mrg, per_row],
        out_shape=[jax.ShapeDtypeStruct((n, D_MODEL), F32),
                   jax.ShapeDtypeStruct((st0.shape[0], RET_HEADS, RET_DK, RET_DV), F32)],
        scratch_shapes=[pltpu.VMEM((G, RET_HEADS, RET_DK, RET_DV), F32)]
                       + [pltpu.VMEM((G, SWA_KV, WINDOW + R, LANES), BF16)] * 4
                       + [pltpu.VMEM((2, tm, D_MODEL), BF16)] * 2
                       + [pltpu.VMEM((tm, D_MODEL), BF16)],
        compiler_params=pltpu.CompilerParams(dimension_semantics=("arbitrary",), vmem_limit_bytes=VMEM_LIMIT),
        name="mixers",
    )(rq, rk, rkd, rv, rgs, sq, gas, gbs, kv, x2d, st0, hist, mt, qd, gr, bias2, sinks, wro, wso, wmo)


def _tail_kernel(h_ref, mk_ref, mv_ref, gc_ref, gf_ref, gl_ref, wcq_ref, wco_ref, wg_ref, wu_ref, wd_ref,
                 y_ref, att, *, G, R):
    h1 = h_ref[...]
    q = jnp.dot((_rms(h1) * gc_ref[...]).astype(BF16), wcq_ref[...], preferred_element_type=F32).astype(BF16)
    for g in range(G):
        rows = slice(g * R, (g + 1) * R)
        for hd in range(MEM_HEADS):
            cols = slice(hd * MEM_HD, (hd + 1) * MEM_HD)
            s = lax.dot_general(q[rows, cols], mk_ref[g, :, cols], (((1,), (1,)), ((), ())),
                                preferred_element_type=F32) * (MEM_HD ** -0.5)
            e = jnp.exp(s - jnp.max(s, axis=-1, keepdims=True))
            den = jnp.sum(e, axis=-1, keepdims=True)
            o = jnp.dot(e.astype(BF16), mv_ref[g, :, cols], preferred_element_type=F32) / den
            att[rows, cols] = o.astype(BF16)
    h2 = h1 + jnp.dot(att[...], wco_ref[...], preferred_element_type=F32)
    hn = (_rms(h2) * gf_ref[...]).astype(BF16)
    gate = jnp.dot(hn, wg_ref[...], preferred_element_type=F32)
    up = jnp.dot(hn, wu_ref[...], preferred_element_type=F32)
    act = (jax.nn.silu(gate) * up).astype(BF16)
    h3 = h2 + jnp.dot(act, wd_ref[...], preferred_element_type=F32)
    y_ref[...] = _rms(h3) * gl_ref[...]


def _tail(h2d, mk, mv, gc, gf, gl, wcq, wco, wg, wu, wd, *, G, R, tiles_per_mem):
    n = h2d.shape[0]
    tm = G * R
    assert n % tm == 0 and (G == 1 or tiles_per_mem == 1)
    row = lambda i: (i, 0)
    mem = pl.BlockSpec((G, N_MEM, D_MODEL), lambda i: (i // tiles_per_mem, 0, 0))
    vec = _resident((1, D_MODEL))
    kern = functools.partial(_tail_kernel, G=G, R=R)
    return pl.pallas_call(
        kern,
        grid=(n // tm,),
        in_specs=[pl.BlockSpec((tm, D_MODEL), row), mem, mem, vec, vec, vec,
                  _resident(wcq.shape), _resident(wco.shape), _resident(wg.shape), _resident(wu.shape),
                  _resident(wd.shape)],
        out_specs=pl.BlockSpec((tm, D_MODEL), row),
        out_shape=jax.ShapeDtypeStruct((n, D_MODEL), F32),
        scratch_shapes=[pltpu.VMEM((tm, D_MODEL), BF16)],
        compiler_params=pltpu.CompilerParams(dimension_semantics=("parallel",), vmem_limit_bytes=VMEM_LIMIT),
        name="tail",
    )(h2d, mk, mv, gc, gf, gl, wcq, wco, wg, wu, wd)


def _mem_kv_kernel(m_ref, g_ref, wk_ref, wv_ref, k_ref, v_ref, kb_ref, vb_ref):
    mn = (_rms(m_ref[...]) * g_ref[...]).astype(BF16)
    k = jnp.dot(mn, wk_ref[...], preferred_element_type=F32)
    v = jnp.dot(mn, wv_ref[...], preferred_element_type=F32)
    k_ref[...] = k
    v_ref[...] = v
    kb_ref[...] = k.astype(BF16)
    vb_ref[...] = v.astype(BF16)


def _mem_kv(mem2d, g, wk, wv, tm):
    n = mem2d.shape[0]
    assert n % tm == 0
    blk = pl.BlockSpec((tm, D_MODEL), lambda i: (i, 0))
    return pl.pallas_call(
        _mem_kv_kernel,
        grid=(n // tm,),
        in_specs=[blk, _resident((1, D_MODEL)), _resident(wk.shape), _resident(wv.shape)],
        out_specs=[blk] * 4,
        out_shape=[jax.ShapeDtypeStruct((n, D_MODEL), F32)] * 2 + [jax.ShapeDtypeStruct((n, D_MODEL), BF16)] * 2,
        compiler_params=pltpu.CompilerParams(dimension_semantics=("parallel",), vmem_limit_bytes=VMEM_LIMIT),
        name="mem_kv",
    )(mem2d, g, wk, wv)


def _log_gamma():
    return jnp.log(1.0 - 2.0 ** (-5.0 - jnp.arange(RET_HEADS, dtype=F32)))


def _rope_tables(pos):
    half = RET_DK // 2
    inv = ROPE_BASE ** (-jnp.arange(half, dtype=F32) / half)
    ang = pos.astype(F32)[:, None] * inv[None, :]
    return jnp.cos(ang), jnp.sin(ang)


def _retention_tables(R):
    lg = _log_gamma()[:, None, None]
    idx = jnp.arange(R, dtype=F32)
    diff = idx[:, None] - idx[None, :]
    cn = (jnp.arange(R) // CHUNK)[:, None]
    cm = (jnp.arange(R) // CHUNK)[None, :]
    mask = jnp.where(cm == cn, jnp.exp(lg * jnp.abs(diff)), jnp.where(cm < cn, jnp.exp(lg * diff), 0.0))
    qd = jnp.exp(_log_gamma()[:, None] * (idx[None, :] + 1.0))
    kd = jnp.exp(_log_gamma()[:, None] * (R - 1.0 - idx[None, :]))
    gr = jnp.exp(_log_gamma() * R)
    qd_full = jnp.broadcast_to(qd[:, :, None], (RET_HEADS, R, RET_DV))
    kd_full = jnp.broadcast_to(kd.T[:, :, None], (R, RET_HEADS, RET_DK)).reshape(R, RET_HEADS * RET_DK)
    return mask.astype(F32), qd_full, kd_full, gr


def _rel_bucket():
    i = jnp.arange(CHUNK, dtype=jnp.int32)[:, None]
    j = jnp.arange(SWA_L, dtype=jnp.int32)[None, :]
    rel = (j - WINDOW) - i
    half = REL_BUCKETS // 2
    max_exact = half // 2
    n = jnp.abs(rel)
    large = max_exact + (jnp.log(jnp.maximum(n, 1).astype(F32) / max_exact)
                         / math.log(REL_MAX_DIST / max_exact) * (half - max_exact)).astype(jnp.int32)
    large = jnp.minimum(large, half - 1)
    return jnp.where(rel > 0, half, 0) + jnp.where(n < max_exact, n, large)


def _tile_rows(t, tm):
    reps = -(-tm // t.shape[0])
    return jnp.tile(t, (reps, 1)) if reps > 1 else t


MERGE_COLS = 256
PIPE_DEPTH = 2
R_PROMPT = 256
TM_PROJ = 512


def kernel(x_prompt, x_sample, cache_ret_state, cache_swa_k, cache_swa_v, cache_mem_k, cache_mem_v, mem_prompt,
           rel_bias, g_attn, w_in, w_ret_out, w_swa_out, w_mix_out, swa_sinks, g_cross, g_mem, w_cq, w_mk, w_mv,
           w_co, g_ffn, w_gate, w_up, w_down, g_final):
    B, S, D = x_prompt.shape
    Bs, T, _ = x_sample.shape
    assert D == D_MODEL and T == CHUNK and S % R_PROMPT == 0 and cache_swa_k.shape[2] == WINDOW
    assert g_attn.shape[0] == 1, "single layer"
    bf = lambda w: w.astype(BF16)
    vec = lambda g: g.reshape(1, D_MODEL)
    w_in_b, wro, wso, wmo = bf(w_in[0]), bf(w_ret_out[0]), bf(w_swa_out[0]), bf(w_mix_out[0])
    wcq, wco, wmk, wmv = bf(w_cq[0]), bf(w_co[0]), bf(w_mk[0]), bf(w_mv[0])
    wg, wu, wd = bf(w_gate[0]), bf(w_up[0]), bf(w_down[0])

    key_of_col, sub_of_col = _swa_col_maps()
    bucket2 = _rel_bucket()[:, key_of_col]
    bias2 = _rel_bias(rel_bias, bucket2, jnp.asarray(sub_of_col)[None, :])
    sinks = swa_sinks[0]

    def layer(x, pos, R, G, st0, hist, mk, mv, has_history):
        nb, seq, _ = x.shape
        x2d = x.reshape(nb * seq, D_MODEL)
        n = x2d.shape[0]
        tm = min(TM_PROJ, n)
        cos_t, sin_t = _rope_tables(pos)
        mt, qd, kd, gr = _retention_tables(R)
        proj = _in_proj(x2d, vec(g_attn[0]), w_in_b, _tile_rows(cos_t, tm), _tile_rows(sin_t, tm),
                        _tile_rows(kd, tm), tm)
        h1, st = _mixers(proj, x2d, st0, hist, mt, qd, gr, bias2, sinks, wro, wso, wmo,
                         G=G, R=R, n_tiles=seq // R, has_history=has_history)
        y = _tail(h1, mk, mv, vec(g_cross[0]), vec(g_ffn[0]), vec(g_final), wcq, wco, wg, wu, wd,
                  G=G, R=R, tiles_per_mem=seq // R)
        return y.reshape(nb, seq, D_MODEL), st, proj[8].reshape(nb, seq, 256)

    mk_f, mv_f, mk_b, mv_b = _mem_kv(mem_prompt.reshape(B * N_MEM, D_MODEL), vec(g_mem[0]), wmk, wmv, 512)
    y_p, st_p, kv_p = layer(
        x_prompt, jnp.arange(S, dtype=jnp.int32), R_PROMPT, 1,
        jnp.zeros((B, RET_HEADS, RET_DK, RET_DV), F32), jnp.zeros((B, WINDOW, 256), F32),
        mk_b.reshape(B, N_MEM, D_MODEL), mv_b.reshape(B, N_MEM, D_MODEL), False)

    Gs = 4 if Bs % 4 == 0 else 1
    hist_s = jnp.concatenate([cache_swa_k[0].reshape(Bs, WINDOW, LANES), cache_swa_v[0].reshape(Bs, WINDOW, LANES)],
                             axis=-1)
    y_s, st_s, kv_s = layer(
        x_sample, PAST_LEN + jnp.arange(T, dtype=jnp.int32), CHUNK, Gs,
        cache_ret_state[0].astype(F32), hist_s,
        bf(cache_mem_k[0].reshape(Bs, N_MEM, D_MODEL)), bf(cache_mem_v[0].reshape(Bs, N_MEM, D_MODEL)), True)

    kvshape = (1, B, WINDOW, SWA_KV, SWA_HD)
    k_p = kv_p[:, S - WINDOW:, :LANES].reshape(kvshape)
    v_p = kv_p[:, S - WINDOW:, LANES:].reshape(kvshape)
    k_s = jnp.concatenate([cache_swa_k[0][:, T:], kv_s[:, :, :LANES].reshape(Bs, T, SWA_KV, SWA_HD)], axis=1)[None]
    v_s = jnp.concatenate([cache_swa_v[0][:, T:], kv_s[:, :, LANES:].reshape(Bs, T, SWA_KV, SWA_HD)], axis=1)[None]
    mem_shape = (1, B, N_MEM, MEM_HEADS, MEM_HD)
    return (y_p, y_s, st_p[None], st_s[None], k_p, k_s, v_p, v_s, mk_f.reshape(mem_shape), mv_f.reshape(mem_shape))
```

```python
import functools
import math

import numpy as np
import jax
import jax.numpy as jnp
from jax import lax
from jax.experimental import pallas as pl
from jax.experimental.pallas import tpu as pltpu

F32 = jnp.float32
BF16 = jnp.bfloat16

D_MODEL = 1024
CHUNK = 64
EPS = 1e-6
NEG_INF = -1e30
PAST_LEN = 4096

RET_HEADS = 4
RET_DK = 256
RET_DV = 256
ROPE_BASE = 10000.0

SWA_HEADS = 16
SWA_KV = 2
SWA_GROUP = 8
SWA_HD = 64
WINDOW = 128
SWA_L = WINDOW + CHUNK

REL_BUCKETS = 32
REL_MAX_DIST = 128

N_MEM = 256
MEM_HEADS = 4
MEM_HD = 256
D_FF = 2816

OFF_RQ, OFF_RK, OFF_RV, OFF_RG, OFF_SQ, OFF_SKV, OFF_GA, OFF_GB = 0, 1024, 2048, 3072, 4096, 5120, 5376, 6400
D_IN = 7424

LANES = 128
HALF = LANES // 2
VMEM_LIMIT = 52 * 1024 * 1024


def _resident(shape):
    nd = len(shape)
    return pl.BlockSpec(shape, lambda *_: (0,) * nd, pipeline_mode=pl.Buffered(1))


def _rms(x):
    return x * lax.rsqrt(jnp.mean(x * x, axis=-1, keepdims=True) + EPS)


def _interleave(primary, secondary):
    out, j = [], 0
    for k, piece in enumerate(primary):
        out.append(piece)
        while j < len(secondary) and (j + 1) * len(primary) <= (k + 1) * len(secondary):
            out.append(secondary[j])
            j += 1
    return out + secondary[j:]


def _in_proj_kernel(x_ref, g_ref, w_ref, cos_ref, sin_ref, kdec_ref,
                    rq_ref, rk_ref, rkd_ref, rv_ref, rgs_ref, sq_ref, gas_ref, gbs_ref, kv_ref):
    x = x_ref[...]
    xn = (_rms(x) * g_ref[...]).astype(BF16)
    cos = cos_ref[...]
    sin = sin_ref[...]

    def proj(c0, n):
        return jnp.dot(xn, w_ref[:, c0:c0 + n], preferred_element_type=F32)

    for h in range(RET_HEADS):
        c = h * RET_DK
        acc = proj(OFF_RQ + c, RET_DK)
        x1, x2 = acc[:, :LANES], acc[:, LANES:]
        rq_ref[:, c:c + LANES] = (x1 * cos - x2 * sin).astype(BF16)
        rq_ref[:, c + LANES:c + RET_DK] = (x2 * cos + x1 * sin).astype(BF16)
        acc = proj(OFF_RK + c, RET_DK)
        x1, x2 = acc[:, :LANES], acc[:, LANES:]
        k1 = (x1 * cos - x2 * sin) * (RET_DK ** -0.5)
        k2 = (x2 * cos + x1 * sin) * (RET_DK ** -0.5)
        rk_ref[:, c:c + LANES] = k1.astype(BF16)
        rk_ref[:, c + LANES:c + RET_DK] = k2.astype(BF16)
        kd = kdec_ref[:, c:c + LANES]
        rkd_ref[:, c:c + LANES] = (k1 * kd).astype(BF16)
        rkd_ref[:, c + LANES:c + RET_DK] = (k2 * kd).astype(BF16)
    for c in range(0, D_MODEL, 512):
        rv_ref[:, c:c + 512] = proj(OFF_RV + c, 512).astype(BF16)
        sq_ref[:, c:c + 512] = proj(OFF_SQ + c, 512).astype(BF16)
        rgs_ref[:, c:c + 512] = jax.nn.silu(proj(OFF_RG + c, 512)).astype(BF16)
        gas_ref[:, c:c + 512] = jax.nn.sigmoid(proj(OFF_GA + c, 512)).astype(BF16)
        gbs_ref[:, c:c + 512] = jax.nn.sigmoid(proj(OFF_GB + c, 512)).astype(BF16)
    kv_ref[...] = proj(OFF_SKV, 2 * SWA_KV * SWA_HD)


def _in_proj(x2d, g, w_in, cos_t, sin_t, kdec_t, tm):
    n = x2d.shape[0]
    assert n % tm == 0 and cos_t.shape[0] % tm == 0 and kdec_t.shape[0] % tm == 0
    ncos = cos_t.shape[0] // tm
    nkd = kdec_t.shape[0] // tm
    row = lambda i: (i, 0)
    wide = pl.BlockSpec((tm, D_MODEL), row)
    out_bf = jax.ShapeDtypeStruct((n, D_MODEL), BF16)
    return pl.pallas_call(
        _in_proj_kernel,
        grid=(n // tm,),
        in_specs=[
            wide,
            _resident((1, D_MODEL)),
            _resident((D_MODEL, D_IN)),
            pl.BlockSpec((tm, LANES), lambda i: (i % ncos, 0)),
            pl.BlockSpec((tm, LANES), lambda i: (i % ncos, 0)),
            pl.BlockSpec((tm, D_MODEL), lambda i: (i % nkd, 0)),
        ],
        out_specs=[wide] * 8 + [pl.BlockSpec((tm, 256), row)],
        out_shape=[out_bf] * 8 + [jax.ShapeDtypeStruct((n, 256), F32)],
        compiler_params=pltpu.CompilerParams(dimension_semantics=("parallel",), vmem_limit_bytes=VMEM_LIMIT),
        name="in_proj",
    )(x2d, g, w_in, cos_t, sin_t, kdec_t)


def _swa_col_maps():
    key = np.concatenate([np.arange(LANES), np.arange(LANES), LANES + np.arange(HALF), LANES + np.arange(HALF)])
    sub = np.concatenate([np.zeros(LANES), np.ones(LANES), np.zeros(HALF), np.ones(HALF)]).astype(np.int32)
    return key.astype(np.int32), sub


def _rel_bias_kernel(tab_ref, bucket_ref, sub_ref, out_ref):
    bucket = bucket_ref[...]
    is_b = sub_ref[...] > 0
    for hk in range(SWA_KV):
        for gp in range(SWA_GROUP // 2):
            h0 = hk * SWA_GROUP + gp * 2
            acc = jnp.zeros(bucket.shape, F32)
            for b in range(REL_BUCKETS):
                val = jnp.where(is_b, tab_ref[b, h0 + 1], tab_ref[b, h0])
                acc = jnp.where(bucket == b, val, acc)
            out_ref[hk, gp * CHUNK:(gp + 1) * CHUNK, :] = acc


def _rel_bias(table, bucket2, sub2):
    ncol = bucket2.shape[1]
    return pl.pallas_call(
        _rel_bias_kernel,
        in_specs=[pl.BlockSpec(memory_space=pltpu.SMEM),
                  pl.BlockSpec(memory_space=pltpu.VMEM),
                  pl.BlockSpec(memory_space=pltpu.VMEM)],
        out_specs=pl.BlockSpec(memory_space=pltpu.VMEM),
        out_shape=jax.ShapeDtypeStruct((SWA_KV, 4 * CHUNK, ncol), F32),
        name="rel_bias",
    )(table, bucket2, sub2)


def _mixers_kernel(rq_ref, rk_ref, rkd_ref, rv_ref, rgs_ref, sq_ref, gas_ref, gbs_ref, kv_ref, x_ref,
                   st0_ref, hist_ref, mt_ref, qd_ref, gr_ref, bias_ref, sink_ref,
                   wro_ref, wso_ref, wmo_ref,
                   h_ref, st_ref,
                   state, ka, kb, va, vb, ain, swo, mixs,
                   *, G, R, n_tiles, has_history):
    t = pl.program_id(0)
    n_real = pl.num_programs(0) - 1
    i = lax.rem(jnp.minimum(t, n_real - 1), n_tiles)
    wslot = lax.rem(t, 2)
    rslot = 1 - wslot
    lane = lax.broadcasted_iota(jnp.int32, (1, LANES), 1)
    lo = lane < HALF

    def split_heads(x):
        r = pltpu.roll(x, HALF, axis=1)
        z = jnp.zeros_like(x)
        a0 = jnp.where(lo, x, z).astype(BF16)
        b0 = jnp.where(lo, z, r).astype(BF16)
        a1 = jnp.where(lo, r, z).astype(BF16)
        b1 = jnp.where(lo, z, x).astype(BF16)
        return (a0, a1), (b0, b1)

    @pl.when(t == 0)
    def _():
        ain[1] = jnp.zeros(ain.shape[1:], BF16)
        swo[1] = jnp.zeros(swo.shape[1:], BF16)

    @pl.when(i == 0)
    def _():
        state[...] = st0_ref[...]
        for g in range(G):
            hist = hist_ref[g]
            tail = slice(R, R + WINDOW)
            (a0, a1), (b0, b1) = split_heads(hist[:, :LANES])
            ka[g, 0, tail], ka[g, 1, tail], kb[g, 0, tail], kb[g, 1, tail] = a0, a1, b0, b1
            (a0, a1), (b0, b1) = split_heads(hist[:, LANES:])
            va[g, 0, tail], va[g, 1, tail], vb[g, 0, tail], vb[g, 1, tail] = a0, a1, b0, b1

    for g in range(G):
        rows = slice(g * R, (g + 1) * R)
        for buf in (ka, kb, va, vb):
            for hk in range(SWA_KV):
                buf[g, hk, :WINDOW] = buf[g, hk, R:R + WINDOW]
        kvt = kv_ref[rows, :]
        (a0, a1), (b0, b1) = split_heads(kvt[:, :LANES])
        ka[g, 0, WINDOW:], ka[g, 1, WINDOW:], kb[g, 0, WINDOW:], kb[g, 1, WINDOW:] = a0, a1, b0, b1
        (a0, a1), (b0, b1) = split_heads(kvt[:, LANES:])
        va[g, 0, WINDOW:], va[g, 1, WINDOW:], vb[g, 0, WINDOW:], vb[g, 1, WINDOW:] = a0, a1, b0, b1

    def ret_front(g, h):
        rows = slice(g * R, (g + 1) * R)
        cols = slice(h * RET_DK, (h + 1) * RET_DK)
        q = rq_ref[rows, cols]
        v = rv_ref[rows, cols]
        s_prev = state[g, h]
        sc = lax.dot_general(q, rk_ref[rows, cols], (((1,), (1,)), ((), ())), preferred_element_type=F32)
        qs = jnp.dot(q, s_prev.astype(BF16), preferred_element_type=F32)
        kv_new = lax.dot_general(rkd_ref[rows, cols], v, (((0,), (0,)), ((), ())), preferred_element_type=F32)
        return sc, qs, kv_new, s_prev, v

    def ret_back(g, h, vals):
        sc, qs, kv_new, s_prev, v = vals
        rows = slice(g * R, (g + 1) * R)
        cols = slice(h * RET_DK, (h + 1) * RET_DK)
        o = jnp.dot((sc * mt_ref[h]).astype(BF16), v, preferred_element_type=F32) + qs * qd_ref[h]
        state[g, h] = s_prev * gr_ref[h] + kv_new
        ain[wslot, rows, cols] = (_rms(o) * rgs_ref[rows, cols].astype(F32)).astype(BF16)

    ones_a = jnp.broadcast_to(jnp.where(lo, 1.0, 0.0).astype(BF16), (LANES, LANES))
    ones_b = jnp.broadcast_to(jnp.where(lo, 0.0, 1.0).astype(BF16), (LANES, LANES))
    lane3 = lax.broadcasted_iota(jnp.int32, (1, 3 * LANES), 1)
    key_of_col = jnp.where(lane3 < 2 * LANES, lane3 % LANES, LANES + lane3 % HALF)

    def swa_front(g, c, hk):
        w0 = c * CHUNK
        r0 = g * R + c * CHUNK
        q2 = jnp.concatenate(
            [sq_ref[r0:r0 + CHUNK, hk * 512 + gp * LANES: hk * 512 + (gp + 1) * LANES] for gp in range(4)],
            axis=0)
        kbd = jnp.concatenate([ka[g, hk, w0:w0 + LANES], kb[g, hk, w0:w0 + LANES],
                               ka[g, hk, w0 + LANES:w0 + SWA_L], kb[g, hk, w0 + LANES:w0 + SWA_L]],
                              axis=0)
        return lax.dot_general(q2, kbd, (((1,), (1,)), ((), ())), preferred_element_type=F32)

    def swa_back(g, c, hk, s):
        w0 = c * CHUNK
        r0 = g * R + c * CHUNK
        s = s * (SWA_HD ** -0.5) + bias_ref[hk]
        if not has_history and c < WINDOW // CHUNK:
            kpos = key_of_col + (i * R + c * CHUNK - WINDOW)
            s = jnp.where(kpos >= 0, s, NEG_INF)
        t0, t1, t2 = s[:, :LANES], s[:, LANES:2 * LANES], s[:, 2 * LANES:]
        m_a = jnp.max(jnp.maximum(t0, jnp.where(lo, t2, NEG_INF)), axis=-1, keepdims=True)
        m_b = jnp.max(jnp.maximum(t1, jnp.where(lo, NEG_INF, t2)), axis=-1, keepdims=True)
        sk_a = jnp.concatenate([jnp.full((CHUNK, 1), sink_ref[hk * SWA_GROUP + 2 * gp], F32) for gp in range(4)],
                               axis=0)
        sk_b = jnp.concatenate([jnp.full((CHUNK, 1), sink_ref[hk * SWA_GROUP + 2 * gp + 1], F32) for gp in range(4)],
                               axis=0)
        m_a = jnp.maximum(m_a, sk_a)
        m_b = jnp.maximum(m_b, sk_b)
        e = jnp.concatenate([jnp.exp(t0 - m_a), jnp.exp(t1 - m_b),
                             jnp.exp(t2 - jnp.where(lo, m_a, m_b))], axis=1).astype(BF16)
        vbd = jnp.concatenate([
            jnp.concatenate([va[g, hk, w0:w0 + LANES], ones_a], axis=1),
            jnp.concatenate([vb[g, hk, w0:w0 + LANES], ones_b], axis=1),
            jnp.concatenate([va[g, hk, w0 + LANES:w0 + SWA_L], ones_a[:HALF]], axis=1),
            jnp.concatenate([vb[g, hk, w0 + LANES:w0 + SWA_L], ones_b[:HALF]], axis=1)], axis=0)
        oa = jnp.dot(e, vbd, preferred_element_type=F32)
        den = oa[:, LANES:] + jnp.where(lo, jnp.exp(sk_a - m_a), jnp.exp(sk_b - m_b))
        o2 = (oa[:, :LANES] / den).astype(BF16)
        for gp in range(4):
            swo[wslot, r0:r0 + CHUNK, hk * 512 + gp * LANES: hk * 512 + (gp + 1) * LANES] = \
                o2[gp * CHUNK:(gp + 1) * CHUNK]

    def merge_ab(j):
        cols = slice(j * MERGE_COLS, (j + 1) * MERGE_COLS)
        a = jnp.dot(ain[rslot], wro_ref[:, cols], preferred_element_type=F32)
        b = jnp.dot(swo[rslot], wso_ref[:, cols], preferred_element_type=F32)
        mixs[:, cols] = (gas_ref[:, cols].astype(F32) * a + gbs_ref[:, cols].astype(F32) * b).astype(BF16)

    def merge_y(j):
        cols = slice(j * MERGE_COLS, (j + 1) * MERGE_COLS)
        h_ref[:, cols] = x_ref[:, cols] + jnp.dot(mixs[...], wmo_ref[:, cols], preferred_element_type=F32)

    items = [(ret_front, ret_back, (g, h)) for g in range(G) for h in range(RET_HEADS)]
    items += [(swa_front, swa_back, (g, c, hk)) for g in range(G) for c in range(R // CHUNK) for hk in range(SWA_KV)]
    tasks = []
    for n in range(len(items) + PIPE_DEPTH):
        if n < len(items):
            tasks.append(("front", n))
        if n >= PIPE_DEPTH:
            tasks.append(("back", n - PIPE_DEPTH))
    n_chunks = D_MODEL // MERGE_COLS
    merges = [functools.partial(merge_ab, j) for j in range(n_chunks)]
    merges += [functools.partial(merge_y, j) for j in range(n_chunks)]
    every = max(1, len(tasks) // len(merges))
    pending = {}
    for pos, (kind, n) in enumerate(tasks):
        front, back, args = items[n]
        if kind == "front":
            pending[n] = front(*args)
        else:
            back(*args, pending.pop(n))
        if (pos + 1) % every == 0 and merges:
            merges.pop(0)()
    for m in merges:
        m()

    @pl.when((i == n_tiles - 1) & (t < n_real))
    def _():
        st_ref[...] = state[...]


def _mixers(proj, x2d, st0, hist, mt, qd, gr, bias2, sinks, wro, wso, wmo, *, G, R, n_tiles, has_history):
    n = x2d.shape[0]
    tm = G * R
    n_steps = n // tm
    assert n_steps * tm == n and n_steps % n_tiles == 0 and (G == 1 or n_tiles == 1)
    att_tile = lambda t: jnp.minimum(t, n_steps - 1)
    att = pl.BlockSpec((tm, D_MODEL), lambda t: (att_tile(t), 0))
    mrg = pl.BlockSpec((tm, D_MODEL), lambda t: (jnp.maximum(t - 1, 0), 0))
    per_row = pl.BlockSpec((G, RET_HEADS, RET_DK, RET_DV), lambda t: (att_tile(t) // n_tiles, 0, 0, 0))
    kern = functools.partial(_mixers_kernel, G=G, R=R, n_tiles=n_tiles, has_history=has_history)
    rq, rk, rkd, rv, rgs, sq, gas, gbs, kv = proj
    return pl.pallas_call(
        kern,
        grid=(n_steps + 1,),
        in_specs=[att] * 6 + [mrg, mrg, pl.BlockSpec((tm, 256), lambda t: (att_tile(t), 0)), mrg,
                              per_row,
                              pl.BlockSpec((G, WINDOW, 256), lambda t: (att_tile(t) // n_tiles, 0, 0)),
                              _resident(mt.shape), _resident(qd.shape),
                              pl.BlockSpec(memory_space=pltpu.SMEM),
                              _resident(bias2.shape),
                              pl.BlockSpec(memory_space=pltpu.SMEM),
                              _resident(wro.shape), _resident(wso.shape), _resident(wmo.shape)],
        out_specs=[mrg, per_row],
        out_shape=[jax.ShapeDtypeStruct((n, D_MODEL), F32),
                   jax.ShapeDtypeStruct((st0.shape[0], RET_HEADS, RET_DK, RET_DV), F32)],
        scratch_shapes=[pltpu.VMEM((G, RET_HEADS, RET_DK, RET_DV), F32)]
                       + [pltpu.VMEM((G, SWA_KV, WINDOW + R, LANES), BF16)] * 4
                       + [pltpu.VMEM((2, tm, D_MODEL), BF16)] * 2
                       + [pltpu.VMEM((tm, D_MODEL), BF16)],
        compiler_params=pltpu.CompilerParams(dimension_semantics=("arbitrary",), vmem_limit_bytes=VMEM_LIMIT),
        name="mixers",
    )(rq, rk, rkd, rv, rgs, sq, gas, gbs, kv, x2d, st0, hist, mt, qd, gr, bias2, sinks, wro, wso, wmo)


def _tail_kernel(h_ref, mk_ref, mv_ref, gc_ref, gf_ref, gl_ref, wcq_ref, wco_ref, wg_ref, wu_ref, wd_ref,
                 y_ref, qs, att, h2s, hn2s, h2_prev, hn2_prev, acts, h3s, *, G, R):
    @pl.when(pl.program_id(0) == 0)
    def _():
        h2s[...] = jnp.zeros(h2s.shape, F32)
        hn2s[...] = jnp.zeros(hn2s.shape, BF16)

    h2_prev[...] = h2s[...]
    hn2_prev[...] = hn2s[...]

    def q_proj():
        hn = (_rms(h_ref[...]) * gc_ref[...]).astype(BF16)
        qs[...] = jnp.dot(hn, wcq_ref[...], preferred_element_type=F32).astype(BF16)

    def head_front(g, hd):
        rows = slice(g * R, (g + 1) * R)
        cols = slice(hd * MEM_HD, (hd + 1) * MEM_HD)
        return lax.dot_general(qs[rows, cols], mk_ref[g, :, cols], (((1,), (1,)), ((), ())),
                               preferred_element_type=F32)

    def head_back(g, hd, s):
        rows = slice(g * R, (g + 1) * R)
        cols = slice(hd * MEM_HD, (hd + 1) * MEM_HD)
        s = s * (MEM_HD ** -0.5)
        e = jnp.exp(s - jnp.max(s, axis=-1, keepdims=True))
        den = jnp.sum(e, axis=-1, keepdims=True)
        o = jnp.dot(e.astype(BF16), mv_ref[g, :, cols], preferred_element_type=F32) / den
        att[rows, cols] = o.astype(BF16)

    def attn_out():
        h2 = h_ref[...] + jnp.dot(att[...], wco_ref[...], preferred_element_type=F32)
        h2s[...] = h2
        hn2s[...] = (_rms(h2) * gf_ref[...]).astype(BF16)

    def ff_chunk(j):
        cols = slice(j * FF_COLS, (j + 1) * FF_COLS)
        hn = hn2_prev[...]
        gate = jnp.dot(hn, wg_ref[:, cols], preferred_element_type=F32)
        up = jnp.dot(hn, wu_ref[:, cols], preferred_element_type=F32)
        acts[:, cols] = (jax.nn.silu(gate) * up).astype(BF16)

    def down_chunk(j):
        cols = slice(j * FF_COLS, (j + 1) * FF_COLS)
        h3s[:, cols] = h2_prev[:, cols] + jnp.dot(acts[...], wd_ref[:, cols], preferred_element_type=F32)

    def final_norm():
        y_ref[...] = _rms(h3s[...]) * gl_ref[...]

    heads = [(g, hd) for g in range(G) for hd in range(MEM_HEADS)]
    pending = {}

    def front(n):
        pending[n] = head_front(*heads[n])

    def back(n):
        head_back(*heads[n], pending.pop(n))

    attn = [q_proj]
    for n in range(len(heads) + 1):
        if n < len(heads):
            attn.append(functools.partial(front, n))
        if n >= 1:
            attn.append(functools.partial(back, n - 1))
    attn.append(attn_out)
    ffn = [functools.partial(ff_chunk, j) for j in range(D_FF // FF_COLS)]
    ffn += [functools.partial(down_chunk, j) for j in range(D_MODEL // FF_COLS)]
    ffn.append(final_norm)
    for piece in _interleave(ffn, attn):
        piece()


def _tail(h2d, mk, mv, gc, gf, gl, wcq, wco, wg, wu, wd, *, G, R, tiles_per_mem):
    n = h2d.shape[0]
    tm = G * R
    n_steps = n // tm
    assert n_steps * tm == n and (G == 1 or tiles_per_mem == 1)
    a_tile = lambda t: jnp.minimum(t, n_steps - 1)
    mem = pl.BlockSpec((G, N_MEM, D_MODEL), lambda t: (a_tile(t) // tiles_per_mem, 0, 0))
    vec = _resident((1, D_MODEL))
    kern = functools.partial(_tail_kernel, G=G, R=R)
    return pl.pallas_call(
        kern,
        grid=(n_steps + 1,),
        in_specs=[pl.BlockSpec((tm, D_MODEL), lambda t: (a_tile(t), 0)), mem, mem, vec, vec, vec,
                  _resident(wcq.shape), _resident(wco.shape), _resident(wg.shape), _resident(wu.shape),
                  _resident(wd.shape)],
        out_specs=pl.BlockSpec((tm, D_MODEL), lambda t: (jnp.maximum(t - 1, 0), 0)),
        out_shape=jax.ShapeDtypeStruct((n, D_MODEL), F32),
        scratch_shapes=[pltpu.VMEM((tm, D_MODEL), BF16), pltpu.VMEM((tm, D_MODEL), BF16),
                        pltpu.VMEM((tm, D_MODEL), F32), pltpu.VMEM((tm, D_MODEL), BF16),
                        pltpu.VMEM((tm, D_MODEL), F32), pltpu.VMEM((tm, D_MODEL), BF16),
                        pltpu.VMEM((tm, D_FF), BF16), pltpu.VMEM((tm, D_MODEL), F32)],
        compiler_params=pltpu.CompilerParams(dimension_semantics=("arbitrary",), vmem_limit_bytes=VMEM_LIMIT),
        name="tail",
    )(h2d, mk, mv, gc, gf, gl, wcq, wco, wg, wu, wd)


def _mem_kv_kernel(m_ref, g_ref, wk_ref, wv_ref, k_ref, v_ref, kb_ref, vb_ref):
    mn = (_rms(m_ref[...]) * g_ref[...]).astype(BF16)
    k = jnp.dot(mn, wk_ref[...], preferred_element_type=F32)
    v = jnp.dot(mn, wv_ref[...], preferred_element_type=F32)
    k_ref[...] = k
    v_ref[...] = v
    kb_ref[...] = k.astype(BF16)
    vb_ref[...] = v.astype(BF16)


def _mem_kv(mem2d, g, wk, wv, tm):
    n = mem2d.shape[0]
    assert n % tm == 0
    blk = pl.BlockSpec((tm, D_MODEL), lambda i: (i, 0))
    return pl.pallas_call(
        _mem_kv_kernel,
        grid=(n // tm,),
        in_specs=[blk, _resident((1, D_MODEL)), _resident(wk.shape), _resident(wv.shape)],
        out_specs=[blk] * 4,
        out_shape=[jax.ShapeDtypeStruct((n, D_MODEL), F32)] * 2 + [jax.ShapeDtypeStruct((n, D_MODEL), BF16)] * 2,
        compiler_params=pltpu.CompilerParams(dimension_semantics=("parallel",), vmem_limit_bytes=VMEM_LIMIT),
        name="mem_kv",
    )(mem2d, g, wk, wv)


def _log_gamma():
    return jnp.log(1.0 - 2.0 ** (-5.0 - jnp.arange(RET_HEADS, dtype=F32)))


def _rope_tables(pos):
    half = RET_DK // 2
    inv = ROPE_BASE ** (-jnp.arange(half, dtype=F32) / half)
    ang = pos.astype(F32)[:, None] * inv[None, :]
    return jnp.cos(ang), jnp.sin(ang)


def _retention_tables(R):
    lg = _log_gamma()[:, None, None]
    idx = jnp.arange(R, dtype=F32)
    diff = idx[:, None] - idx[None, :]
    cn = (jnp.arange(R) // CHUNK)[:, None]
    cm = (jnp.arange(R) // CHUNK)[None, :]
    mask = jnp.where(cm == cn, jnp.exp(lg * jnp.abs(diff)), jnp.where(cm < cn, jnp.exp(lg * diff), 0.0))
    qd = jnp.exp(_log_gamma()[:, None] * (idx[None, :] + 1.0))
    kd = jnp.exp(_log_gamma()[:, None] * (R - 1.0 - idx[None, :]))
    gr = jnp.exp(_log_gamma() * R)
    qd_full = jnp.broadcast_to(qd[:, :, None], (RET_HEADS, R, RET_DV))
    kd_full = jnp.broadcast_to(kd.T[:, :, None], (R, RET_HEADS, RET_DK)).reshape(R, RET_HEADS * RET_DK)
    return mask.astype(F32), qd_full, kd_full, gr


def _rel_bucket():
    i = jnp.arange(CHUNK, dtype=jnp.int32)[:, None]
    j = jnp.arange(SWA_L, dtype=jnp.int32)[None, :]
    rel = (j - WINDOW) - i
    half = REL_BUCKETS // 2
    max_exact = half // 2
    n = jnp.abs(rel)
    large = max_exact + (jnp.log(jnp.maximum(n, 1).astype(F32) / max_exact)
                         / math.log(REL_MAX_DIST / max_exact) * (half - max_exact)).astype(jnp.int32)
    large = jnp.minimum(large, half - 1)
    return jnp.where(rel > 0, half, 0) + jnp.where(n < max_exact, n, large)


def _tile_rows(t, tm):
    reps = -(-tm // t.shape[0])
    return jnp.tile(t, (reps, 1)) if reps > 1 else t


FF_COLS = 256
MERGE_COLS = 256
PIPE_DEPTH = 2
R_PROMPT = 256
TM_PROJ = 512


def kernel(x_prompt, x_sample, cache_ret_state, cache_swa_k, cache_swa_v, cache_mem_k, cache_mem_v, mem_prompt,
           rel_bias, g_attn, w_in, w_ret_out, w_swa_out, w_mix_out, swa_sinks, g_cross, g_mem, w_cq, w_mk, w_mv,
           w_co, g_ffn, w_gate, w_up, w_down, g_final):
    B, S, D = x_prompt.shape
    Bs, T, _ = x_sample.shape
    assert D == D_MODEL and T == CHUNK and S % R_PROMPT == 0 and cache_swa_k.shape[2] == WINDOW
    assert g_attn.shape[0] == 1, "single layer"
    bf = lambda w: w.astype(BF16)
    vec = lambda g: g.reshape(1, D_MODEL)
    w_in_b, wro, wso, wmo = bf(w_in[0]), bf(w_ret_out[0]), bf(w_swa_out[0]), bf(w_mix_out[0])
    wcq, wco, wmk, wmv = bf(w_cq[0]), bf(w_co[0]), bf(w_mk[0]), bf(w_mv[0])
    wg, wu, wd = bf(w_gate[0]), bf(w_up[0]), bf(w_down[0])

    key_of_col, sub_of_col = _swa_col_maps()
    bucket2 = _rel_bucket()[:, key_of_col]
    bias2 = _rel_bias(rel_bias, bucket2, jnp.asarray(sub_of_col)[None, :])
    sinks = swa_sinks[0]

    def layer(x, pos, R, G, st0, hist, mk, mv, has_history):
        nb, seq, _ = x.shape
        x2d = x.reshape(nb * seq, D_MODEL)
        n = x2d.shape[0]
        tm = min(TM_PROJ, n)
        cos_t, sin_t = _rope_tables(pos)
        mt, qd, kd, gr = _retention_tables(R)
        proj = _in_proj(x2d, vec(g_attn[0]), w_in_b, _tile_rows(cos_t, tm), _tile_rows(sin_t, tm),
                        _tile_rows(kd, tm), tm)
        h1, st = _mixers(proj, x2d, st0, hist, mt, qd, gr, bias2, sinks, wro, wso, wmo,
                         G=G, R=R, n_tiles=seq // R, has_history=has_history)
        y = _tail(h1, mk, mv, vec(g_cross[0]), vec(g_ffn[0]), vec(g_final), wcq, wco, wg, wu, wd,
                  G=G, R=R, tiles_per_mem=seq // R)
        return y.reshape(nb, seq, D_MODEL), st, proj[8].reshape(nb, seq, 256)

    mk_f, mv_f, mk_b, mv_b = _mem_kv(mem_prompt.reshape(B * N_MEM, D_MODEL), vec(g_mem[0]), wmk, wmv, 512)
    y_p, st_p, kv_p = layer(
        x_prompt, jnp.arange(S, dtype=jnp.int32), R_PROMPT, 1,
        jnp.zeros((B, RET_HEADS, RET_DK, RET_DV), F32), jnp.zeros((B, WINDOW, 256), F32),
        mk_b.reshape(B, N_MEM, D_MODEL), mv_b.reshape(B, N_MEM, D_MODEL), False)

    Gs = 4 if Bs % 4 == 0 else 1
    hist_s = jnp.concatenate([cache_swa_k[0].reshape(Bs, WINDOW, LANES), cache_swa_v[0].reshape(Bs, WINDOW, LANES)],
                             axis=-1)
    y_s, st_s, kv_s = layer(
        x_sample, PAST_LEN + jnp.arange(T, dtype=jnp.int32), CHUNK, Gs,
        cache_ret_state[0].astype(F32), hist_s,
        bf(cache_mem_k[0].reshape(Bs, N_MEM, D_MODEL)), bf(cache_mem_v[0].reshape(Bs, N_MEM, D_MODEL)), True)

    kvshape = (1, B, WINDOW, SWA_KV, SWA_HD)
    k_p = kv_p[:, S - WINDOW:, :LANES].reshape(kvshape)
    v_p = kv_p[:, S - WINDOW:, LANES:].reshape(kvshape)
    k_s = jnp.concatenate([cache_swa_k[0][:, T:], kv_s[:, :, :LANES].reshape(Bs, T, SWA_KV, SWA_HD)], axis=1)[None]
    v_s = jnp.concatenate([cache_swa_v[0][:, T:], kv_s[:, :, LANES:].reshape(Bs, T, SWA_KV, SWA_HD)], axis=1)[None]
    mem_shape = (1, B, N_MEM, MEM_HEADS, MEM_HD)
    return (y_p, y_s, st_p[None], st_s[None], k_p, k_s, v_p, v_s, mk_f.reshape(mem_shape), mv_f.reshape(mem_shape))
```

```python
import functools
import math

import numpy as np
import jax
import jax.numpy as jnp
from jax import lax
from jax.experimental import pallas as pl
from jax.experimental.pallas import tpu as pltpu

F32 = jnp.float32
BF16 = jnp.bfloat16

D_MODEL = 1024
CHUNK = 64
EPS = 1e-6
NEG_INF = -1e30
PAST_LEN = 4096

RET_HEADS = 4
RET_DK = 256
RET_DV = 256
ROPE_BASE = 10000.0

SWA_HEADS = 16
SWA_KV = 2
SWA_GROUP = 8
SWA_HD = 64
WINDOW = 128
SWA_L = WINDOW + CHUNK

REL_BUCKETS = 32
REL_MAX_DIST = 128

N_MEM = 256
MEM_HEADS = 4
MEM_HD = 256
D_FF = 2816

OFF_RQ, OFF_RK, OFF_RV, OFF_RG, OFF_SQ, OFF_SKV, OFF_GA, OFF_GB = 0, 1024, 2048, 3072, 4096, 5120, 5376, 6400
D_IN = 7424

LANES = 128
HALF = LANES // 2
VMEM_LIMIT = 56 * 1024 * 1024

FF_COLS = 256
MERGE_COLS = 256
GATE_COLS = 512
PIPE_DEPTH = 2
R_PROMPT = 256


def _resident(shape):
    nd = len(shape)
    return pl.BlockSpec(shape, lambda *_: (0,) * nd, pipeline_mode=pl.Buffered(1))


def _rms(x):
    return x * lax.rsqrt(jnp.mean(x * x, axis=-1, keepdims=True) + EPS)


def _interleave(primary, secondary):
    out, j = [], 0
    for k, piece in enumerate(primary):
        out.append(piece)
        while j < len(secondary) and (j + 1) * len(primary) <= (k + 1) * len(secondary):
            out.append(secondary[j])
            j += 1
    return out + secondary[j:]


def _swa_col_maps():
    key = np.concatenate([np.arange(LANES), np.arange(LANES), LANES + np.arange(HALF), LANES + np.arange(HALF)])
    sub = np.concatenate([np.zeros(LANES), np.ones(LANES), np.zeros(HALF), np.ones(HALF)]).astype(np.int32)
    return key.astype(np.int32), sub


def _rel_bias_kernel(tab_ref, bucket_ref, sub_ref, out_ref):
    bucket = bucket_ref[...]
    is_b = sub_ref[...] > 0
    for hk in range(SWA_KV):
        for gp in range(SWA_GROUP // 2):
            h0 = hk * SWA_GROUP + gp * 2
            acc = jnp.zeros(bucket.shape, F32)
            for b in range(REL_BUCKETS):
                val = jnp.where(is_b, tab_ref[b, h0 + 1], tab_ref[b, h0])
                acc = jnp.where(bucket == b, val, acc)
            out_ref[hk, gp * CHUNK:(gp + 1) * CHUNK, :] = acc


def _rel_bias(table, bucket2, sub2):
    ncol = bucket2.shape[1]
    return pl.pallas_call(
        _rel_bias_kernel,
        in_specs=[pl.BlockSpec(memory_space=pltpu.SMEM),
                  pl.BlockSpec(memory_space=pltpu.VMEM),
                  pl.BlockSpec(memory_space=pltpu.VMEM)],
        out_specs=pl.BlockSpec(memory_space=pltpu.VMEM),
        out_shape=jax.ShapeDtypeStruct((SWA_KV, 4 * CHUNK, ncol), F32),
        name="rel_bias",
    )(table, bucket2, sub2)


def _front_kernel(x_ref, xp_ref, g_ref, w_ref, cos_ref, sin_ref, kdec_ref,
                  st0_ref, hist_ref, mt_ref, qd_ref, gr_ref, bias_ref, sink_ref,
                  wro_ref, wso_ref, wmo_ref,
                  h_ref, st_ref, kvo_ref,
                  xn, rq, rk, rkd, rv, rgs, sq, gas, gbs, state, ka, kb, va, vb, ain, swo, mixs,
                  *, G, R, n_tiles, has_history):
    t = pl.program_id(0)
    n_real = pl.num_programs(0) - 1
    i = lax.rem(jnp.minimum(t, n_real - 1), n_tiles)
    wslot = lax.rem(t, 2)
    rslot = 1 - wslot
    lane = lax.broadcasted_iota(jnp.int32, (1, LANES), 1)
    lo = lane < HALF

    def split_heads(x):
        r = pltpu.roll(x, HALF, axis=1)
        z = jnp.zeros_like(x)
        a0 = jnp.where(lo, x, z).astype(BF16)
        b0 = jnp.where(lo, z, r).astype(BF16)
        a1 = jnp.where(lo, r, z).astype(BF16)
        b1 = jnp.where(lo, z, x).astype(BF16)
        return (a0, a1), (b0, b1)

    @pl.when(t == 0)
    def _():
        ain[1] = jnp.zeros(ain.shape[1:], BF16)
        swo[1] = jnp.zeros(swo.shape[1:], BF16)
        gas[...] = jnp.zeros(gas.shape, BF16)
        gbs[...] = jnp.zeros(gbs.shape, BF16)

    @pl.when(i == 0)
    def _():
        state[...] = st0_ref[...]
        for g in range(G):
            hist = hist_ref[g]
            tail = slice(R, R + WINDOW)
            (a0, a1), (b0, b1) = split_heads(hist[:, :LANES])
            ka[g, 0, tail], ka[g, 1, tail], kb[g, 0, tail], kb[g, 1, tail] = a0, a1, b0, b1
            (a0, a1), (b0, b1) = split_heads(hist[:, LANES:])
            va[g, 0, tail], va[g, 1, tail], vb[g, 0, tail], vb[g, 1, tail] = a0, a1, b0, b1

    def norm():
        xn[...] = (_rms(x_ref[...]) * g_ref[...]).astype(BF16)

    def proj(c0, n):
        return jnp.dot(xn[...], w_ref[:, c0:c0 + n], preferred_element_type=F32)

    def proj_head(h):
        c = h * RET_DK
        cos = cos_ref[...]
        sin = sin_ref[...]
        acc = proj(OFF_RQ + c, RET_DK)
        x1, x2 = acc[:, :LANES], acc[:, LANES:]
        rq[:, c:c + LANES] = (x1 * cos - x2 * sin).astype(BF16)
        rq[:, c + LANES:c + RET_DK] = (x2 * cos + x1 * sin).astype(BF16)
        acc = proj(OFF_RK + c, RET_DK)
        x1, x2 = acc[:, :LANES], acc[:, LANES:]
        k1 = (x1 * cos - x2 * sin) * (RET_DK ** -0.5)
        k2 = (x2 * cos + x1 * sin) * (RET_DK ** -0.5)
        rk[:, c:c + LANES] = k1.astype(BF16)
        rk[:, c + LANES:c + RET_DK] = k2.astype(BF16)
        kd = kdec_ref[:, c:c + LANES]
        rkd[:, c:c + LANES] = (k1 * kd).astype(BF16)
        rkd[:, c + LANES:c + RET_DK] = (k2 * kd).astype(BF16)
        rv[:, c:c + RET_DV] = proj(OFF_RV + c, RET_DV).astype(BF16)
        rgs[:, c:c + RET_DV] = jax.nn.silu(proj(OFF_RG + c, RET_DV)).astype(BF16)

    def proj_sq(j):
        c = j * GATE_COLS
        sq[:, c:c + GATE_COLS] = proj(OFF_SQ + c, GATE_COLS).astype(BF16)

    def proj_kv():
        kvt_all = proj(OFF_SKV, 2 * SWA_KV * SWA_HD)
        kvo_ref[...] = kvt_all
        for g in range(G):
            kvt = kvt_all[g * R:(g + 1) * R]
            for buf in (ka, kb, va, vb):
                for hk in range(SWA_KV):
                    buf[g, hk, :WINDOW] = buf[g, hk, R:R + WINDOW]
            (a0, a1), (b0, b1) = split_heads(kvt[:, :LANES])
            ka[g, 0, WINDOW:], ka[g, 1, WINDOW:], kb[g, 0, WINDOW:], kb[g, 1, WINDOW:] = a0, a1, b0, b1
            (a0, a1), (b0, b1) = split_heads(kvt[:, LANES:])
            va[g, 0, WINDOW:], va[g, 1, WINDOW:], vb[g, 0, WINDOW:], vb[g, 1, WINDOW:] = a0, a1, b0, b1

    def proj_gate(dst, off, j):
        c = j * GATE_COLS
        dst[:, c:c + GATE_COLS] = jax.nn.sigmoid(proj(off + c, GATE_COLS)).astype(BF16)

    def ret_front(g, h):
        rows = slice(g * R, (g + 1) * R)
        cols = slice(h * RET_DK, (h + 1) * RET_DK)
        q = rq[rows, cols]
        v = rv[rows, cols]
        s_prev = state[g, h]
        sc = lax.dot_general(q, rk[rows, cols], (((1,), (1,)), ((), ())), preferred_element_type=F32)
        qs = jnp.dot(q, s_prev.astype(BF16), preferred_element_type=F32)
        kv_new = lax.dot_general(rkd[rows, cols], v, (((0,), (0,)), ((), ())), preferred_element_type=F32)
        return sc, qs, kv_new, s_prev, v

    def ret_back(g, h, vals):
        sc, qs, kv_new, s_prev, v = vals
        rows = slice(g * R, (g + 1) * R)
        cols = slice(h * RET_DK, (h + 1) * RET_DK)
        o = jnp.dot((sc * mt_ref[h]).astype(BF16), v, preferred_element_type=F32) + qs * qd_ref[h]
        state[g, h] = s_prev * gr_ref[h] + kv_new
        ain[wslot, rows, cols] = (_rms(o) * rgs[rows, cols].astype(F32)).astype(BF16)

    ones_a = jnp.broadcast_to(jnp.where(lo, 1.0, 0.0).astype(BF16), (LANES, LANES))
    ones_b = jnp.broadcast_to(jnp.where(lo, 0.0, 1.0).astype(BF16), (LANES, LANES))
    lane3 = lax.broadcasted_iota(jnp.int32, (1, 3 * LANES), 1)
    key_of_col = jnp.where(lane3 < 2 * LANES, lane3 % LANES, LANES + lane3 % HALF)

    def swa_front(g, c, hk):
        w0 = c * CHUNK
        r0 = g * R + c * CHUNK
        q2 = jnp.concatenate(
            [sq[r0:r0 + CHUNK, hk * 512 + gp * LANES: hk * 512 + (gp + 1) * LANES] for gp in range(4)],
            axis=0)
        kbd = jnp.concatenate([ka[g, hk, w0:w0 + LANES], kb[g, hk, w0:w0 + LANES],
                               ka[g, hk, w0 + LANES:w0 + SWA_L], kb[g, hk, w0 + LANES:w0 + SWA_L]],
                              axis=0)
        return lax.dot_general(q2, kbd, (((1,), (1,)), ((), ())), preferred_element_type=F32)

    def swa_back(g, c, hk, s):
        w0 = c * CHUNK
        r0 = g * R + c * CHUNK
        s = s * (SWA_HD ** -0.5) + bias_ref[hk]
        if not has_history and c < WINDOW // CHUNK:
            kpos = key_of_col + (i * R + c * CHUNK - WINDOW)
            s = jnp.where(kpos >= 0, s, NEG_INF)
        t0, t1, t2 = s[:, :LANES], s[:, LANES:2 * LANES], s[:, 2 * LANES:]
        m_a = jnp.max(jnp.maximum(t0, jnp.where(lo, t2, NEG_INF)), axis=-1, keepdims=True)
        m_b = jnp.max(jnp.maximum(t1, jnp.where(lo, NEG_INF, t2)), axis=-1, keepdims=True)
        sk_a = jnp.concatenate([jnp.full((CHUNK, 1), sink_ref[hk * SWA_GROUP + 2 * gp], F32) for gp in range(4)],
                               axis=0)
        sk_b = jnp.concatenate([jnp.full((CHUNK, 1), sink_ref[hk * SWA_GROUP + 2 * gp + 1], F32) for gp in range(4)],
                               axis=0)
        m_a = jnp.maximum(m_a, sk_a)
        m_b = jnp.maximum(m_b, sk_b)
        e = jnp.concatenate([jnp.exp(t0 - m_a), jnp.exp(t1 - m_b),
                             jnp.exp(t2 - jnp.where(lo, m_a, m_b))], axis=1).astype(BF16)
        vbd = jnp.concatenate([
            jnp.concatenate([va[g, hk, w0:w0 + LANES], ones_a], axis=1),
            jnp.concatenate([vb[g, hk, w0:w0 + LANES], ones_b], axis=1),
            jnp.concatenate([va[g, hk, w0 + LANES:w0 + SWA_L], ones_a[:HALF]], axis=1),
            jnp.concatenate([vb[g, hk, w0 + LANES:w0 + SWA_L], ones_b[:HALF]], axis=1)], axis=0)
        oa = jnp.dot(e, vbd, preferred_element_type=F32)
        den = oa[:, LANES:] + jnp.where(lo, jnp.exp(sk_a - m_a), jnp.exp(sk_b - m_b))
        o2 = (oa[:, :LANES] / den).astype(BF16)
        for gp in range(4):
            swo[wslot, r0:r0 + CHUNK, hk * 512 + gp * LANES: hk * 512 + (gp + 1) * LANES] = \
                o2[gp * CHUNK:(gp + 1) * CHUNK]

    def merge_ab(j):
        cols = slice(j * MERGE_COLS, (j + 1) * MERGE_COLS)
        a = jnp.dot(ain[rslot], wro_ref[:, cols], preferred_element_type=F32)
        b = jnp.dot(swo[rslot], wso_ref[:, cols], preferred_element_type=F32)
        mixs[:, cols] = (gas[:, cols].astype(F32) * a + gbs[:, cols].astype(F32) * b).astype(BF16)

    def merge_y(j):
        cols = slice(j * MERGE_COLS, (j + 1) * MERGE_COLS)
        h_ref[:, cols] = xp_ref[:, cols] + jnp.dot(mixs[...], wmo_ref[:, cols], preferred_element_type=F32)

    n_mc = D_MODEL // MERGE_COLS
    n_gc = D_MODEL // GATE_COLS
    dense = [(functools.partial(merge_ab, 0), None), (norm, None)]
    for h in range(RET_HEADS):
        dense.append((functools.partial(proj_head, h), ("head", h)))
        if h + 1 < n_mc:
            dense.append((functools.partial(merge_ab, h + 1), None))
    dense += [(functools.partial(merge_ab, j), None) for j in range(RET_HEADS + 1, n_mc)]
    dense += [(functools.partial(proj_sq, j), None) for j in range(n_gc)]
    dense.append((proj_kv, ("swa", 0)))
    dense += [(functools.partial(proj_gate, gas, OFF_GA, j), None) for j in range(n_gc)]
    dense += [(functools.partial(proj_gate, gbs, OFF_GB, j), None) for j in range(n_gc)]
    dense += [(functools.partial(merge_y, j), None) for j in range(n_mc)]

    items = [(ret_front, ret_back, (g, h), ("head", h)) for h in range(RET_HEADS) for g in range(G)]
    items += [(swa_front, swa_back, (g, c, hk), ("swa", 0))
              for g in range(G) for c in range(R // CHUNK) for hk in range(SWA_KV)]
    tasks = []
    for n in range(len(items) + PIPE_DEPTH):
        if n < len(items):
            tasks.append(("front", n))
        if n >= PIPE_DEPTH:
            tasks.append(("back", n - PIPE_DEPTH))

    pending, done, nxt = {}, set(), 0

    def run_task(kind, n):
        front, back, args, _ = items[n]
        if kind == "front":
            pending[n] = front(*args)
        else:
            back(*args, pending.pop(n))

    for k, (piece, tag) in enumerate(dense):
        piece()
        if tag is not None:
            done.add(tag)
        quota = -(-(len(tasks) - nxt) // (len(dense) - k))
        while nxt < len(tasks) and quota > 0:
            kind, n = tasks[nxt]
            if kind == "front" and items[n][3] not in done:
                break
            run_task(kind, n)
            nxt += 1
            quota -= 1
    assert nxt == len(tasks)

    @pl.when((i == n_tiles - 1) & (t < n_real))
    def _():
        st_ref[...] = state[...]


def _front(x2d, g, w_in, cos_t, sin_t, kdec_t, st0, hist, mt, qd, gr, bias2, sinks, wro, wso, wmo,
           *, G, R, n_tiles, has_history):
    n = x2d.shape[0]
    tm = G * R
    n_steps = n // tm
    assert n_steps * tm == n and n_steps % n_tiles == 0 and (G == 1 or n_tiles == 1)
    assert cos_t.shape == (n_tiles * tm, LANES) and kdec_t.shape == (tm, D_MODEL)
    cur = lambda t: jnp.minimum(t, n_steps - 1)
    prev = lambda t: jnp.maximum(t - 1, 0)
    rope = pl.BlockSpec((tm, LANES), lambda t: (cur(t) % n_tiles, 0))
    per_row = pl.BlockSpec((G, RET_HEADS, RET_DK, RET_DV), lambda t: (cur(t) // n_tiles, 0, 0, 0))
    kern = functools.partial(_front_kernel, G=G, R=R, n_tiles=n_tiles, has_history=has_history)
    tile_bf = pltpu.VMEM((tm, D_MODEL), BF16)
    return pl.pallas_call(
        kern,
        grid=(n_steps + 1,),
        in_specs=[pl.BlockSpec((tm, D_MODEL), lambda t: (cur(t), 0)),
                  pl.BlockSpec((tm, D_MODEL), lambda t: (prev(t), 0)),
                  _resident((1, D_MODEL)), _resident(w_in.shape), rope, rope, _resident(kdec_t.shape),
                  per_row,
                  pl.BlockSpec((G, WINDOW, 256), lambda t: (cur(t) // n_tiles, 0, 0)),
                  _resident(mt.shape), _resident(qd.shape),
                  pl.BlockSpec(memory_space=pltpu.SMEM),
                  _resident(bias2.shape),
                  pl.BlockSpec(memory_space=pltpu.SMEM),
                  _resident(wro.shape), _resident(wso.shape), _resident(wmo.shape)],
        out_specs=[pl.BlockSpec((tm, D_MODEL), lambda t: (prev(t), 0)), per_row,
                   pl.BlockSpec((tm, 256), lambda t: (cur(t), 0))],
        out_shape=[jax.ShapeDtypeStruct((n, D_MODEL), F32),
                   jax.ShapeDtypeStruct((st0.shape[0], RET_HEADS, RET_DK, RET_DV), F32),
                   jax.ShapeDtypeStruct((n, 256), F32)],
        scratch_shapes=[tile_bf] * 9
                       + [pltpu.VMEM((G, RET_HEADS, RET_DK, RET_DV), F32)]
                       + [pltpu.VMEM((G, SWA_KV, WINDOW + R, LANES), BF16)] * 4
                       + [pltpu.VMEM((2, tm, D_MODEL), BF16)] * 2
                       + [tile_bf],
        compiler_params=pltpu.CompilerParams(dimension_semantics=("arbitrary",), vmem_limit_bytes=VMEM_LIMIT),
        name="front",
    )(x2d, x2d, g, w_in, cos_t, sin_t, kdec_t, st0, hist, mt, qd, gr, bias2, sinks, wro, wso, wmo)


def _tail_kernel(h_ref, mk_ref, mv_ref, gc_ref, gf_ref, gl_ref, wcq_ref, wco_ref, wg_ref, wu_ref, wd_ref,
                 y_ref, qs, att, h2s, hn2s, h2_prev, hn2_prev, acts, h3s, *, G, R):
    @pl.when(pl.program_id(0) == 0)
    def _():
        h2s[...] = jnp.zeros(h2s.shape, F32)
        hn2s[...] = jnp.zeros(hn2s.shape, BF16)

    h2_prev[...] = h2s[...]
    hn2_prev[...] = hn2s[...]

    def q_proj():
        hn = (_rms(h_ref[...]) * gc_ref[...]).astype(BF16)
        qs[...] = jnp.dot(hn, wcq_ref[...], preferred_element_type=F32).astype(BF16)

    def head_front(g, hd):
        rows = slice(g * R, (g + 1) * R)
        cols = slice(hd * MEM_HD, (hd + 1) * MEM_HD)
        return lax.dot_general(qs[rows, cols], mk_ref[g, :, cols], (((1,), (1,)), ((), ())),
                               preferred_element_type=F32)

    def head_back(g, hd, s):
        rows = slice(g * R, (g + 1) * R)
        cols = slice(hd * MEM_HD, (hd + 1) * MEM_HD)
        s = s * (MEM_HD ** -0.5)
        e = jnp.exp(s - jnp.max(s, axis=-1, keepdims=True))
        den = jnp.sum(e, axis=-1, keepdims=True)
        o = jnp.dot(e.astype(BF16), mv_ref[g, :, cols], preferred_element_type=F32) / den
        att[rows, cols] = o.astype(BF16)

    def attn_out():
        h2 = h_ref[...] + jnp.dot(att[...], wco_ref[...], preferred_element_type=F32)
        h2s[...] = h2
        hn2s[...] = (_rms(h2) * gf_ref[...]).astype(BF16)

    def ff_chunk(j):
        cols = slice(j * FF_COLS, (j + 1) * FF_COLS)
        hn = hn2_prev[...]
        gate = jnp.dot(hn, wg_ref[:, cols], preferred_element_type=F32)
        up = jnp.dot(hn, wu_ref[:, cols], preferred_element_type=F32)
        acts[:, cols] = (jax.nn.silu(gate) * up).astype(BF16)

    def down_chunk(j):
        cols = slice(j * FF_COLS, (j + 1) * FF_COLS)
        h3s[:, cols] = h2_prev[:, cols] + jnp.dot(acts[...], wd_ref[:, cols], preferred_element_type=F32)

    def final_norm():
        y_ref[...] = _rms(h3s[...]) * gl_ref[...]

    heads = [(g, hd) for g in range(G) for hd in range(MEM_HEADS)]
    pending = {}

    def front(n):
        pending[n] = head_front(*heads[n])

    def back(n):
        head_back(*heads[n], pending.pop(n))

    attn = [q_proj]
    for n in range(len(heads) + 1):
        if n < len(heads):
            attn.append(functools.partial(front, n))
        if n >= 1:
            attn.append(functools.partial(back, n - 1))
    attn.append(attn_out)
    ffn = [functools.partial(ff_chunk, j) for j in range(D_FF // FF_COLS)]
    ffn += [functools.partial(down_chunk, j) for j in range(D_MODEL // FF_COLS)]
    ffn.append(final_norm)
    for piece in _interleave(ffn, attn):
        piece()


def _tail(h2d, mk, mv, gc, gf, gl, wcq, wco, wg, wu, wd, *, G, R, tiles_per_mem):
    n = h2d.shape[0]
    tm = G * R
    n_steps = n // tm
    assert n_steps * tm == n and (G == 1 or tiles_per_mem == 1)
    a_tile = lambda t: jnp.minimum(t, n_steps - 1)
    mem = pl.BlockSpec((G, N_MEM, D_MODEL), lambda t: (a_tile(t) // tiles_per_mem, 0, 0))
    vec = _resident((1, D_MODEL))
    kern = functools.partial(_tail_kernel, G=G, R=R)
    return pl.pallas_call(
        kern,
        grid=(n_steps + 1,),
        in_specs=[pl.BlockSpec((tm, D_MODEL), lambda t: (a_tile(t), 0)), mem, mem, vec, vec, vec,
                  _resident(wcq.shape), _resident(wco.shape), _resident(wg.shape), _resident(wu.shape),
                  _resident(wd.shape)],
        out_specs=pl.BlockSpec((tm, D_MODEL), lambda t: (jnp.maximum(t - 1, 0), 0)),
        out_shape=jax.ShapeDtypeStruct((n, D_MODEL), F32),
        scratch_shapes=[pltpu.VMEM((tm, D_MODEL), BF16), pltpu.VMEM((tm, D_MODEL), BF16),
                        pltpu.VMEM((tm, D_MODEL), F32), pltpu.VMEM((tm, D_MODEL), BF16),
                        pltpu.VMEM((tm, D_MODEL), F32), pltpu.VMEM((tm, D_MODEL), BF16),
                        pltpu.VMEM((tm, D_FF), BF16), pltpu.VMEM((tm, D_MODEL), F32)],
        compiler_params=pltpu.CompilerParams(dimension_semantics=("arbitrary",), vmem_limit_bytes=VMEM_LIMIT),
        name="tail",
    )(h2d, mk, mv, gc, gf, gl, wcq, wco, wg, wu, wd)


def _mem_kv_kernel(m_ref, g_ref, wk_ref, wv_ref, k_ref, v_ref, kb_ref, vb_ref):
    mn = (_rms(m_ref[...]) * g_ref[...]).astype(BF16)
    k = jnp.dot(mn, wk_ref[...], preferred_element_type=F32)
    v = jnp.dot(mn, wv_ref[...], preferred_element_type=F32)
    k_ref[...] = k
    v_ref[...] = v
    kb_ref[...] = k.astype(BF16)
    vb_ref[...] = v.astype(BF16)


def _mem_kv(mem2d, g, wk, wv, tm):
    n = mem2d.shape[0]
    assert n % tm == 0
    blk = pl.BlockSpec((tm, D_MODEL), lambda i: (i, 0))
    return pl.pallas_call(
        _mem_kv_kernel,
        grid=(n // tm,),
        in_specs=[blk, _resident((1, D_MODEL)), _resident(wk.shape), _resident(wv.shape)],
        out_specs=[blk] * 4,
        out_shape=[jax.ShapeDtypeStruct((n, D_MODEL), F32)] * 2 + [jax.ShapeDtypeStruct((n, D_MODEL), BF16)] * 2,
        compiler_params=pltpu.CompilerParams(dimension_semantics=("parallel",), vmem_limit_bytes=VMEM_LIMIT),
        name="mem_kv",
    )(mem2d, g, wk, wv)


def _log_gamma():
    return jnp.log(1.0 - 2.0 ** (-5.0 - jnp.arange(RET_HEADS, dtype=F32)))


def _rope_tables(pos):
    half = RET_DK // 2
    inv = ROPE_BASE ** (-jnp.arange(half, dtype=F32) / half)
    ang = pos.astype(F32)[:, None] * inv[None, :]
    return jnp.cos(ang), jnp.sin(ang)


def _retention_tables(R):
    lg = _log_gamma()[:, None, None]
    idx = jnp.arange(R, dtype=F32)
    diff = idx[:, None] - idx[None, :]
    cn = (jnp.arange(R) // CHUNK)[:, None]
    cm = (jnp.arange(R) // CHUNK)[None, :]
    mask = jnp.where(cm == cn, jnp.exp(lg * jnp.abs(diff)), jnp.where(cm < cn, jnp.exp(lg * diff), 0.0))
    qd = jnp.exp(_log_gamma()[:, None] * (idx[None, :] + 1.0))
    kd = jnp.exp(_log_gamma()[:, None] * (R - 1.0 - idx[None, :]))
    gr = jnp.exp(_log_gamma() * R)
    qd_full = jnp.broadcast_to(qd[:, :, None], (RET_HEADS, R, RET_DV))
    kd_full = jnp.broadcast_to(kd.T[:, :, None], (R, RET_HEADS, RET_DK)).reshape(R, RET_HEADS * RET_DK)
    return mask.astype(F32), qd_full, kd_full, gr


def _rel_bucket():
    i = jnp.arange(CHUNK, dtype=jnp.int32)[:, None]
    j = jnp.arange(SWA_L, dtype=jnp.int32)[None, :]
    rel = (j - WINDOW) - i
    half = REL_BUCKETS // 2
    max_exact = half // 2
    n = jnp.abs(rel)
    large = max_exact + (jnp.log(jnp.maximum(n, 1).astype(F32) / max_exact)
                         / math.log(REL_MAX_DIST / max_exact) * (half - max_exact)).astype(jnp.int32)
    large = jnp.minimum(large, half - 1)
    return jnp.where(rel > 0, half, 0) + jnp.where(n < max_exact, n, large)


def _tile_rows(t, tm):
    reps = -(-tm // t.shape[0])
    return jnp.tile(t, (reps, 1)) if reps > 1 else t


def kernel(x_prompt, x_sample, cache_ret_state, cache_swa_k, cache_swa_v, cache_mem_k, cache_mem_v, mem_prompt,
           rel_bias, g_attn, w_in, w_ret_out, w_swa_out, w_mix_out, swa_sinks, g_cross, g_mem, w_cq, w_mk, w_mv,
           w_co, g_ffn, w_gate, w_up, w_down, g_final):
    B, S, D = x_prompt.shape
    Bs, T, _ = x_sample.shape
    assert D == D_MODEL and T == CHUNK and S % R_PROMPT == 0 and cache_swa_k.shape[2] == WINDOW
    assert g_attn.shape[0] == 1, "single layer"
    bf = lambda w: w.astype(BF16)
    vec = lambda g: g.reshape(1, D_MODEL)
    w_in_b, wro, wso, wmo = bf(w_in[0]), bf(w_ret_out[0]), bf(w_swa_out[0]), bf(w_mix_out[0])
    wcq, wco, wmk, wmv = bf(w_cq[0]), bf(w_co[0]), bf(w_mk[0]), bf(w_mv[0])
    wg, wu, wd = bf(w_gate[0]), bf(w_up[0]), bf(w_down[0])

    key_of_col, sub_of_col = _swa_col_maps()
    bucket2 = _rel_bucket()[:, key_of_col]
    bias2 = _rel_bias(rel_bias, bucket2, jnp.asarray(sub_of_col)[None, :])
    sinks = swa_sinks[0]

    def layer(x, pos, R, G, st0, hist, mk, mv, has_history):
        nb, seq, _ = x.shape
        x2d = x.reshape(nb * seq, D_MODEL)
        tm = G * R
        cos_t, sin_t = _rope_tables(pos)
        mt, qd, kd, gr = _retention_tables(R)
        h1, st, kv = _front(x2d, vec(g_attn[0]), w_in_b, _tile_rows(cos_t, tm), _tile_rows(sin_t, tm),
                            _tile_rows(kd, tm), st0, hist, mt, qd, gr, bias2, sinks, wro, wso, wmo,
                            G=G, R=R, n_tiles=seq // R, has_history=has_history)
        y = _tail(h1, mk, mv, vec(g_cross[0]), vec(g_ffn[0]), vec(g_final), wcq, wco, wg, wu, wd,
                  G=G, R=R, tiles_per_mem=seq // R)
        return y.reshape(nb, seq, D_MODEL), st, kv.reshape(nb, seq, 256)

    mk_f, mv_f, mk_b, mv_b = _mem_kv(mem_prompt.reshape(B * N_MEM, D_MODEL), vec(g_mem[0]), wmk, wmv, 512)
    y_p, st_p, kv_p = layer(
        x_prompt, jnp.arange(S, dtype=jnp.int32), R_PROMPT, 1,
        jnp.zeros((B, RET_HEADS, RET_DK, RET_DV), F32), jnp.zeros((B, WINDOW, 256), F32),
        mk_b.reshape(B, N_MEM, D_MODEL), mv_b.reshape(B, N_MEM, D_MODEL), False)

    Gs = 2 if Bs % 2 == 0 else 1
    hist_s = jnp.concatenate([cache_swa_k[0].reshape(Bs, WINDOW, LANES), cache_swa_v[0].reshape(Bs, WINDOW, LANES)],
                             axis=-1)
    y_s, st_s, kv_s = layer(
        x_sample, PAST_LEN + jnp.arange(T, dtype=jnp.int32), CHUNK, Gs,
        cache_ret_state[0].astype(F32), hist_s,
        bf(cache_mem_k[0].reshape(Bs, N_MEM, D_MODEL)), bf(cache_mem_v[0].reshape(Bs, N_MEM, D_MODEL)), True)

    kvshape = (1, B, WINDOW, SWA_KV, SWA_HD)
    k_p = kv_p[:, S - WINDOW:, :LANES].reshape(kvshape)
    v_p = kv_p[:, S - WINDOW:, LANES:].reshape(kvshape)
    k_s = jnp.concatenate([cache_swa_k[0][:, T:], kv_s[:, :, :LANES].reshape(Bs, T, SWA_KV, SWA_HD)], axis=1)[None]
    v_s = jnp.concatenate([cache_swa_v[0][:, T:], kv_s[:, :, LANES:].reshape(Bs, T, SWA_KV, SWA_HD)], axis=1)[None]
    mem_shape = (1, B, N_MEM, MEM_HEADS, MEM_HD)
    return (y_p, y_s, st_p[None], st_s[None], k_p, k_s, v_p, v_s, mk_f.reshape(mem_shape), mv_f.reshape(mem_shape))
```

```python
import functools
import math

import numpy as np
import jax
import jax.numpy as jnp
from jax import lax
from jax.experimental import pallas as pl
from jax.experimental.pallas import tpu as pltpu

F32 = jnp.float32
BF16 = jnp.bfloat16

D_MODEL = 1024
CHUNK = 64
EPS = 1e-6
NEG_INF = -1e30
PAST_LEN = 4096

RET_HEADS = 4
RET_DK = 256
RET_DV = 256
ROPE_BASE = 10000.0

SWA_HEADS = 16
SWA_KV = 2
SWA_GROUP = 8
SWA_HD = 64
WINDOW = 128
SWA_L = WINDOW + CHUNK

REL_BUCKETS = 32
REL_MAX_DIST = 128

N_MEM = 256
MEM_HEADS = 4
MEM_HD = 256
D_FF = 2816

OFF_RQ, OFF_RK, OFF_RV, OFF_RG, OFF_SQ, OFF_SKV, OFF_GA, OFF_GB = 0, 1024, 2048, 3072, 4096, 5120, 5376, 6400
D_IN = 7424

LANES = 128
HALF = LANES // 2
VMEM_LIMIT = 56 * 1024 * 1024

FF_COLS = 256
MERGE_COLS = 256
GATE_COLS = 512
KV_COLS = 2 * SWA_KV * SWA_HD
assert OFF_GA == OFF_SKV + KV_COLS and OFF_GB == OFF_GA + D_MODEL
PIPE_DEPTH = 2
TAIL_ROWS = 512
R_PROMPT = 256


def _resident(shape):
    nd = len(shape)
    return pl.BlockSpec(shape, lambda *_: (0,) * nd, pipeline_mode=pl.Buffered(1))


def _rms(x):
    return x * lax.rsqrt(jnp.mean(x * x, axis=-1, keepdims=True) + EPS)


def _interleave(primary, secondary):
    out, j = [], 0
    for k, piece in enumerate(primary):
        out.append(piece)
        while j < len(secondary) and (j + 1) * len(primary) <= (k + 1) * len(secondary):
            out.append(secondary[j])
            j += 1
    return out + secondary[j:]


def _swa_col_maps():
    key = np.concatenate([np.arange(LANES), np.arange(LANES), LANES + np.arange(HALF), LANES + np.arange(HALF)])
    sub = np.concatenate([np.zeros(LANES), np.ones(LANES), np.zeros(HALF), np.ones(HALF)]).astype(np.int32)
    return key.astype(np.int32), sub


def _rel_bias_kernel(tab_ref, bucket_ref, sub_ref, out_ref):
    bucket = bucket_ref[...]
    is_b = sub_ref[...] > 0
    for hk in range(SWA_KV):
        for gp in range(SWA_GROUP // 2):
            h0 = hk * SWA_GROUP + gp * 2
            acc = jnp.zeros(bucket.shape, F32)
            for b in range(REL_BUCKETS):
                val = jnp.where(is_b, tab_ref[b, h0 + 1], tab_ref[b, h0])
                acc = jnp.where(bucket == b, val, acc)
            out_ref[hk, gp * CHUNK:(gp + 1) * CHUNK, :] = acc


def _rel_bias(table, bucket2, sub2):
    ncol = bucket2.shape[1]
    return pl.pallas_call(
        _rel_bias_kernel,
        in_specs=[pl.BlockSpec(memory_space=pltpu.SMEM),
                  pl.BlockSpec(memory_space=pltpu.VMEM),
                  pl.BlockSpec(memory_space=pltpu.VMEM)],
        out_specs=pl.BlockSpec(memory_space=pltpu.VMEM),
        out_shape=jax.ShapeDtypeStruct((SWA_KV, 4 * CHUNK, ncol), F32),
        name="rel_bias",
    )(table, bucket2, sub2)


def _front_kernel(x_ref, xp_ref, g_ref, w_ref, cos_ref, sin_ref, kdec_ref,
                  st0_ref, hist_ref, mt_ref, qd_ref, gr_ref, bias_ref, sink_ref,
                  wro_ref, wso_ref, wmo_ref,
                  h_ref, st_ref, kvo_ref,
                  xn, rq, rk, rkd, rv, rgs, sq, gas, gbs, state, ka, kb, va, vb, ain, swo, mixs,
                  *, G, R, n_tiles, has_history):
    t = pl.program_id(0)
    n_real = pl.num_programs(0) - 1
    i = lax.rem(jnp.minimum(t, n_real - 1), n_tiles)
    wslot = lax.rem(t, 2)
    rslot = 1 - wslot
    lane = lax.broadcasted_iota(jnp.int32, (1, LANES), 1)
    lo = lane < HALF

    def split_heads(x):
        r = pltpu.roll(x, HALF, axis=1)
        z = jnp.zeros_like(x)
        a0 = jnp.where(lo, x, z).astype(BF16)
        b0 = jnp.where(lo, z, r).astype(BF16)
        a1 = jnp.where(lo, r, z).astype(BF16)
        b1 = jnp.where(lo, z, x).astype(BF16)
        return (a0, a1), (b0, b1)

    @pl.when(t == 0)
    def _():
        ain[1] = jnp.zeros(ain.shape[1:], BF16)
        swo[1] = jnp.zeros(swo.shape[1:], BF16)
        gas[...] = jnp.zeros(gas.shape, BF16)
        gbs[...] = jnp.zeros(gbs.shape, BF16)

    @pl.when(i == 0)
    def _():
        state[...] = st0_ref[...]
        for g in range(G):
            hist = hist_ref[g]
            tail = slice(R, R + WINDOW)
            (a0, a1), (b0, b1) = split_heads(hist[:, :LANES])
            ka[g, 0, tail], ka[g, 1, tail], kb[g, 0, tail], kb[g, 1, tail] = a0, a1, b0, b1
            (a0, a1), (b0, b1) = split_heads(hist[:, LANES:])
            va[g, 0, tail], va[g, 1, tail], vb[g, 0, tail], vb[g, 1, tail] = a0, a1, b0, b1

    def norm():
        xn[...] = (_rms(x_ref[...]) * g_ref[...]).astype(BF16)

    def proj(c0, n):
        return jnp.dot(xn[...], w_ref[:, c0:c0 + n], preferred_element_type=F32)

    def proj_head(h):
        c = h * RET_DK
        cos = cos_ref[...]
        sin = sin_ref[...]
        acc = proj(OFF_RQ + c, RET_DK)
        x1, x2 = acc[:, :LANES], acc[:, LANES:]
        rq[:, c:c + LANES] = (x1 * cos - x2 * sin).astype(BF16)
        rq[:, c + LANES:c + RET_DK] = (x2 * cos + x1 * sin).astype(BF16)
        acc = proj(OFF_RK + c, RET_DK)
        x1, x2 = acc[:, :LANES], acc[:, LANES:]
        k1 = (x1 * cos - x2 * sin) * (RET_DK ** -0.5)
        k2 = (x2 * cos + x1 * sin) * (RET_DK ** -0.5)
        rk[:, c:c + LANES] = k1.astype(BF16)
        rk[:, c + LANES:c + RET_DK] = k2.astype(BF16)
        kd = kdec_ref[:, c:c + LANES]
        rkd[:, c:c + LANES] = (k1 * kd).astype(BF16)
        rkd[:, c + LANES:c + RET_DK] = (k2 * kd).astype(BF16)
        rv[:, c:c + RET_DV] = proj(OFF_RV + c, RET_DV).astype(BF16)
        rgs[:, c:c + RET_DV] = jax.nn.silu(proj(OFF_RG + c, RET_DV)).astype(BF16)

    def proj_sq(j):
        c = j * GATE_COLS
        sq[:, c:c + GATE_COLS] = proj(OFF_SQ + c, GATE_COLS).astype(BF16)

    def proj_kv():
        acc = proj(OFF_SKV, 2 * KV_COLS)
        kvt_all = acc[:, :KV_COLS]
        gas[:, :KV_COLS] = jax.nn.sigmoid(acc[:, KV_COLS:]).astype(BF16)
        kvo_ref[...] = kvt_all
        for g in range(G):
            kvt = kvt_all[g * R:(g + 1) * R]
            for buf in (ka, kb, va, vb):
                for hk in range(SWA_KV):
                    buf[g, hk, :WINDOW] = buf[g, hk, R:R + WINDOW]
            (a0, a1), (b0, b1) = split_heads(kvt[:, :LANES])
            ka[g, 0, WINDOW:], ka[g, 1, WINDOW:], kb[g, 0, WINDOW:], kb[g, 1, WINDOW:] = a0, a1, b0, b1
            (a0, a1), (b0, b1) = split_heads(kvt[:, LANES:])
            va[g, 0, WINDOW:], va[g, 1, WINDOW:], vb[g, 0, WINDOW:], vb[g, 1, WINDOW:] = a0, a1, b0, b1

    def proj_gates(a, b):
        acc = jax.nn.sigmoid(proj(OFF_GA + a, b - a)).astype(BF16)
        if b <= D_MODEL:
            gas[:, a:b] = acc
        elif a >= D_MODEL:
            gbs[:, a - D_MODEL:b - D_MODEL] = acc
        else:
            gas[:, a:] = acc[:, :D_MODEL - a]
            gbs[:, :b - D_MODEL] = acc[:, D_MODEL - a:]

    def ret_front(g, h):
        rows = slice(g * R, (g + 1) * R)
        cols = slice(h * RET_DK, (h + 1) * RET_DK)
        q = rq[rows, cols]
        v = rv[rows, cols]
        s_prev = state[g, h]
        sc = lax.dot_general(q, rk[rows, cols], (((1,), (1,)), ((), ())), preferred_element_type=F32)
        qs = jnp.dot(q, s_prev.astype(BF16), preferred_element_type=F32)
        kv_new = lax.dot_general(rkd[rows, cols], v, (((0,), (0,)), ((), ())), preferred_element_type=F32)
        return sc, qs, kv_new, s_prev, v

    def ret_back(g, h, vals):
        sc, qs, kv_new, s_prev, v = vals
        rows = slice(g * R, (g + 1) * R)
        cols = slice(h * RET_DK, (h + 1) * RET_DK)
        o = jnp.dot((sc * mt_ref[h]).astype(BF16), v, preferred_element_type=F32) + qs * qd_ref[h]
        state[g, h] = s_prev * gr_ref[h] + kv_new
        ain[wslot, rows, cols] = (_rms(o) * rgs[rows, cols].astype(F32)).astype(BF16)

    ones_a = jnp.broadcast_to(jnp.where(lo, 1.0, 0.0).astype(BF16), (LANES, LANES))
    ones_b = jnp.broadcast_to(jnp.where(lo, 0.0, 1.0).astype(BF16), (LANES, LANES))
    lane3 = lax.broadcasted_iota(jnp.int32, (1, 3 * LANES), 1)
    key_of_col = jnp.where(lane3 < 2 * LANES, lane3 % LANES, LANES + lane3 % HALF)

    def swa_front(g, c, hk):
        w0 = c * CHUNK
        r0 = g * R + c * CHUNK
        q2 = jnp.concatenate(
            [sq[r0:r0 + CHUNK, hk * 512 + gp * LANES: hk * 512 + (gp + 1) * LANES] for gp in range(4)],
            axis=0)
        kbd = jnp.concatenate([ka[g, hk, w0:w0 + LANES], kb[g, hk, w0:w0 + LANES],
                               ka[g, hk, w0 + LANES:w0 + SWA_L], kb[g, hk, w0 + LANES:w0 + SWA_L]],
                              axis=0)
        return lax.dot_general(q2, kbd, (((1,), (1,)), ((), ())), preferred_element_type=F32)

    def swa_back(g, c, hk, s):
        w0 = c * CHUNK
        r0 = g * R + c * CHUNK
        s = s * (SWA_HD ** -0.5) + bias_ref[hk]
        if not has_history and c < WINDOW // CHUNK:
            kpos = key_of_col + (i * R + c * CHUNK - WINDOW)
            s = jnp.where(kpos >= 0, s, NEG_INF)
        t0, t1, t2 = s[:, :LANES], s[:, LANES:2 * LANES], s[:, 2 * LANES:]
        m_a = jnp.max(jnp.maximum(t0, jnp.where(lo, t2, NEG_INF)), axis=-1, keepdims=True)
        m_b = jnp.max(jnp.maximum(t1, jnp.where(lo, NEG_INF, t2)), axis=-1, keepdims=True)
        sk_a = jnp.concatenate([jnp.full((CHUNK, 1), sink_ref[hk * SWA_GROUP + 2 * gp], F32) for gp in range(4)],
                               axis=0)
        sk_b = jnp.concatenate([jnp.full((CHUNK, 1), sink_ref[hk * SWA_GROUP + 2 * gp + 1], F32) for gp in range(4)],
                               axis=0)
        m_a = jnp.maximum(m_a, sk_a)
        m_b = jnp.maximum(m_b, sk_b)
        e = jnp.concatenate([jnp.exp(t0 - m_a), jnp.exp(t1 - m_b),
                             jnp.exp(t2 - jnp.where(lo, m_a, m_b))], axis=1).astype(BF16)
        vbd = jnp.concatenate([
            jnp.concatenate([va[g, hk, w0:w0 + LANES], ones_a], axis=1),
            jnp.concatenate([vb[g, hk, w0:w0 + LANES], ones_b], axis=1),
            jnp.concatenate([va[g, hk, w0 + LANES:w0 + SWA_L], ones_a[:HALF]], axis=1),
            jnp.concatenate([vb[g, hk, w0 + LANES:w0 + SWA_L], ones_b[:HALF]], axis=1)], axis=0)
        oa = jnp.dot(e, vbd, preferred_element_type=F32)
        den = oa[:, LANES:] + jnp.where(lo, jnp.exp(sk_a - m_a), jnp.exp(sk_b - m_b))
        o2 = (oa[:, :LANES] / den).astype(BF16)
        for gp in range(4):
            swo[wslot, r0:r0 + CHUNK, hk * 512 + gp * LANES: hk * 512 + (gp + 1) * LANES] = \
                o2[gp * CHUNK:(gp + 1) * CHUNK]

    def merge_ab(j):
        cols = slice(j * MERGE_COLS, (j + 1) * MERGE_COLS)
        a = jnp.dot(ain[rslot], wro_ref[:, cols], preferred_element_type=F32)
        b = jnp.dot(swo[rslot], wso_ref[:, cols], preferred_element_type=F32)
        mixs[:, cols] = (gas[:, cols].astype(F32) * a + gbs[:, cols].astype(F32) * b).astype(BF16)

    def merge_y(j):
        cols = slice(j * GATE_COLS, (j + 1) * GATE_COLS)
        h_ref[:, cols] = xp_ref[:, cols] + jnp.dot(mixs[...], wmo_ref[:, cols], preferred_element_type=F32)

    n_mc = D_MODEL // MERGE_COLS
    n_gc = D_MODEL // GATE_COLS
    dense = [(functools.partial(merge_ab, 0), None), (norm, None)]
    for h in range(RET_HEADS):
        dense.append((functools.partial(proj_head, h), ("head", h)))
        if h + 1 < n_mc:
            dense.append((functools.partial(merge_ab, h + 1), None))
    dense += [(functools.partial(merge_ab, j), None) for j in range(RET_HEADS + 1, n_mc)]
    dense += [(functools.partial(proj_sq, j), None) for j in range(n_gc)]
    dense.append((proj_kv, ("swa", 0)))
    gate_cuts = list(range(KV_COLS, 2 * D_MODEL, GATE_COLS)) + [2 * D_MODEL]
    dense += [(functools.partial(proj_gates, a, b), None) for a, b in zip(gate_cuts[:-1], gate_cuts[1:])]
    dense += [(functools.partial(merge_y, j), None) for j in range(n_gc)]

    items = [(ret_front, ret_back, (g, h), ("head", h)) for h in range(RET_HEADS) for g in range(G)]
    items += [(swa_front, swa_back, (g, c, hk), ("swa", 0))
              for g in range(G) for c in range(R // CHUNK) for hk in range(SWA_KV)]
    tasks = []
    for n in range(len(items) + PIPE_DEPTH):
        if n < len(items):
            tasks.append(("front", n))
        if n >= PIPE_DEPTH:
            tasks.append(("back", n - PIPE_DEPTH))

    pending, done, nxt = {}, set(), 0

    def run_task(kind, n):
        front, back, args, _ = items[n]
        if kind == "front":
            pending[n] = front(*args)
        else:
            back(*args, pending.pop(n))

    for k, (piece, tag) in enumerate(dense):
        piece()
        if tag is not None:
            done.add(tag)
        quota = -(-(len(tasks) - nxt) // (len(dense) - k))
        while nxt < len(tasks) and quota > 0:
            kind, n = tasks[nxt]
            if kind == "front" and items[n][3] not in done:
                break
            run_task(kind, n)
            nxt += 1
            quota -= 1
    assert nxt == len(tasks)

    @pl.when((i == n_tiles - 1) & (t < n_real))
    def _():
        st_ref[...] = state[...]


def _front(x2d, g, w_in, cos_t, sin_t, kdec_t, st0, hist, mt, qd, gr, bias2, sinks, wro, wso, wmo,
           *, G, R, n_tiles, has_history):
    n = x2d.shape[0]
    tm = G * R
    n_steps = n // tm
    assert n_steps * tm == n and n_steps % n_tiles == 0 and (G == 1 or n_tiles == 1)
    assert cos_t.shape == (n_tiles * tm, LANES) and kdec_t.shape == (tm, D_MODEL)
    cur = lambda t: jnp.minimum(t, n_steps - 1)
    prev = lambda t: jnp.maximum(t - 1, 0)
    rope = pl.BlockSpec((tm, LANES), lambda t: (cur(t) % n_tiles, 0))
    per_row = pl.BlockSpec((G, RET_HEADS, RET_DK, RET_DV), lambda t: (cur(t) // n_tiles, 0, 0, 0))
    kern = functools.partial(_front_kernel, G=G, R=R, n_tiles=n_tiles, has_history=has_history)
    tile_bf = pltpu.VMEM((tm, D_MODEL), BF16)
    return pl.pallas_call(
        kern,
        grid=(n_steps + 1,),
        in_specs=[pl.BlockSpec((tm, D_MODEL), lambda t: (cur(t), 0)),
                  pl.BlockSpec((tm, D_MODEL), lambda t: (prev(t), 0)),
                  _resident((1, D_MODEL)), _resident(w_in.shape), rope, rope, _resident(kdec_t.shape),
                  per_row,
                  pl.BlockSpec((G, WINDOW, 256), lambda t: (cur(t) // n_tiles, 0, 0)),
                  _resident(mt.shape), _resident(qd.shape),
                  pl.BlockSpec(memory_space=pltpu.SMEM),
                  _resident(bias2.shape),
                  pl.BlockSpec(memory_space=pltpu.SMEM),
                  _resident(wro.shape), _resident(wso.shape), _resident(wmo.shape)],
        out_specs=[pl.BlockSpec((tm, D_MODEL), lambda t: (prev(t), 0)), per_row,
                   pl.BlockSpec((tm, 256), lambda t: (cur(t), 0))],
        out_shape=[jax.ShapeDtypeStruct((n, D_MODEL), F32),
                   jax.ShapeDtypeStruct((st0.shape[0], RET_HEADS, RET_DK, RET_DV), F32),
                   jax.ShapeDtypeStruct((n, 256), F32)],
        scratch_shapes=[tile_bf] * 9
                       + [pltpu.VMEM((G, RET_HEADS, RET_DK, RET_DV), F32)]
                       + [pltpu.VMEM((G, SWA_KV, WINDOW + R, LANES), BF16)] * 4
                       + [pltpu.VMEM((2, tm, D_MODEL), BF16)] * 2
                       + [tile_bf],
        compiler_params=pltpu.CompilerParams(dimension_semantics=("arbitrary",), vmem_limit_bytes=VMEM_LIMIT),
        name="front",
    )(x2d, x2d, g, w_in, cos_t, sin_t, kdec_t, st0, hist, mt, qd, gr, bias2, sinks, wro, wso, wmo)


def _tail_kernel(h_ref, mk_ref, mv_ref, gc_ref, gf_ref, gl_ref, wcq_ref, wco_ref, wg_ref, wu_ref, wd_ref,
                 y_ref, qs, att, h2s, hn2s, h2_prev, hn2_prev, acts, h3s, *, G, R):
    @pl.when(pl.program_id(0) == 0)
    def _():
        h2s[...] = jnp.zeros(h2s.shape, F32)
        hn2s[...] = jnp.zeros(hn2s.shape, BF16)

    h2_prev[...] = h2s[...]
    hn2_prev[...] = hn2s[...]

    def q_proj():
        hn = (_rms(h_ref[...]) * gc_ref[...]).astype(BF16)
        qs[...] = jnp.dot(hn, wcq_ref[...], preferred_element_type=F32).astype(BF16)

    def head_front(g, hd):
        rows = slice(g * R, (g + 1) * R)
        cols = slice(hd * MEM_HD, (hd + 1) * MEM_HD)
        return lax.dot_general(qs[rows, cols], mk_ref[g, :, cols], (((1,), (1,)), ((), ())),
                               preferred_element_type=F32)

    def head_back(g, hd, s):
        rows = slice(g * R, (g + 1) * R)
        cols = slice(hd * MEM_HD, (hd + 1) * MEM_HD)
        s = s * (MEM_HD ** -0.5)
        e = jnp.exp(s - jnp.max(s, axis=-1, keepdims=True))
        den = jnp.sum(e, axis=-1, keepdims=True)
        o = jnp.dot(e.astype(BF16), mv_ref[g, :, cols], preferred_element_type=F32) / den
        att[rows, cols] = o.astype(BF16)

    def attn_out():
        h2 = h_ref[...] + jnp.dot(att[...], wco_ref[...], preferred_element_type=F32)
        h2s[...] = h2
        hn2s[...] = (_rms(h2) * gf_ref[...]).astype(BF16)

    def ff_chunk(j):
        cols = slice(j * FF_COLS, (j + 1) * FF_COLS)
        hn = hn2_prev[...]
        gate = jnp.dot(hn, wg_ref[:, cols], preferred_element_type=F32)
        up = jnp.dot(hn, wu_ref[:, cols], preferred_element_type=F32)
        acts[:, cols] = (jax.nn.silu(gate) * up).astype(BF16)

    def down_chunk(j):
        cols = slice(j * FF_COLS, (j + 1) * FF_COLS)
        h3s[:, cols] = h2_prev[:, cols] + jnp.dot(acts[...], wd_ref[:, cols], preferred_element_type=F32)

    def final_norm():
        y_ref[...] = _rms(h3s[...]) * gl_ref[...]

    heads = [(g, hd) for g in range(G) for hd in range(MEM_HEADS)]
    pending = {}

    def front(n):
        pending[n] = head_front(*heads[n])

    def back(n):
        head_back(*heads[n], pending.pop(n))

    attn = [q_proj]
    for n in range(len(heads) + 1):
        if n < len(heads):
            attn.append(functools.partial(front, n))
        if n >= 1:
            attn.append(functools.partial(back, n - 1))
    attn.append(attn_out)
    ffn = [functools.partial(ff_chunk, j) for j in range(D_FF // FF_COLS)]
    ffn += [functools.partial(down_chunk, j) for j in range(D_MODEL // FF_COLS)]
    ffn.append(final_norm)
    for piece in _interleave(ffn, attn):
        piece()


def _tail(h2d, mk, mv, gc, gf, gl, wcq, wco, wg, wu, wd, *, G, R, tiles_per_mem):
    n = h2d.shape[0]
    tm = G * R
    n_steps = n // tm
    assert n_steps * tm == n and (G == 1 or tiles_per_mem == 1)
    a_tile = lambda t: jnp.minimum(t, n_steps - 1)
    mem = pl.BlockSpec((G, N_MEM, D_MODEL), lambda t: (a_tile(t) // tiles_per_mem, 0, 0))
    vec = _resident((1, D_MODEL))
    kern = functools.partial(_tail_kernel, G=G, R=R)
    return pl.pallas_call(
        kern,
        grid=(n_steps + 1,),
        in_specs=[pl.BlockSpec((tm, D_MODEL), lambda t: (a_tile(t), 0)), mem, mem, vec, vec, vec,
                  _resident(wcq.shape), _resident(wco.shape), _resident(wg.shape), _resident(wu.shape),
                  _resident(wd.shape)],
        out_specs=pl.BlockSpec((tm, D_MODEL), lambda t: (jnp.maximum(t - 1, 0), 0)),
        out_shape=jax.ShapeDtypeStruct((n, D_MODEL), F32),
        scratch_shapes=[pltpu.VMEM((tm, D_MODEL), BF16), pltpu.VMEM((tm, D_MODEL), BF16),
                        pltpu.VMEM((tm, D_MODEL), F32), pltpu.VMEM((tm, D_MODEL), BF16),
                        pltpu.VMEM((tm, D_MODEL), F32), pltpu.VMEM((tm, D_MODEL), BF16),
                        pltpu.VMEM((tm, D_FF), BF16), pltpu.VMEM((tm, D_MODEL), F32)],
        compiler_params=pltpu.CompilerParams(dimension_semantics=("arbitrary",), vmem_limit_bytes=VMEM_LIMIT),
        name="tail",
    )(h2d, mk, mv, gc, gf, gl, wcq, wco, wg, wu, wd)


def _mem_kv_kernel(m_ref, g_ref, wk_ref, wv_ref, k_ref, v_ref, kb_ref, vb_ref):
    mn = (_rms(m_ref[...]) * g_ref[...]).astype(BF16)
    k = jnp.dot(mn, wk_ref[...], preferred_element_type=F32)
    v = jnp.dot(mn, wv_ref[...], preferred_element_type=F32)
    k_ref[...] = k
    v_ref[...] = v
    kb_ref[...] = k.astype(BF16)
    vb_ref[...] = v.astype(BF16)


def _mem_kv(mem2d, g, wk, wv, tm):
    n = mem2d.shape[0]
    assert n % tm == 0
    blk = pl.BlockSpec((tm, D_MODEL), lambda i: (i, 0))
    return pl.pallas_call(
        _mem_kv_kernel,
        grid=(n // tm,),
        in_specs=[blk, _resident((1, D_MODEL)), _resident(wk.shape), _resident(wv.shape)],
        out_specs=[blk] * 4,
        out_shape=[jax.ShapeDtypeStruct((n, D_MODEL), F32)] * 2 + [jax.ShapeDtypeStruct((n, D_MODEL), BF16)] * 2,
        compiler_params=pltpu.CompilerParams(dimension_semantics=("parallel",), vmem_limit_bytes=VMEM_LIMIT),
        name="mem_kv",
    )(mem2d, g, wk, wv)


def _log_gamma():
    return jnp.log(1.0 - 2.0 ** (-5.0 - jnp.arange(RET_HEADS, dtype=F32)))


def _rope_tables(pos):
    half = RET_DK // 2
    inv = ROPE_BASE ** (-jnp.arange(half, dtype=F32) / half)
    ang = pos.astype(F32)[:, None] * inv[None, :]
    return jnp.cos(ang), jnp.sin(ang)


def _retention_tables(R):
    lg = _log_gamma()[:, None, None]
    idx = jnp.arange(R, dtype=F32)
    diff = idx[:, None] - idx[None, :]
    cn = (jnp.arange(R) // CHUNK)[:, None]
    cm = (jnp.arange(R) // CHUNK)[None, :]
    mask = jnp.where(cm == cn, jnp.exp(lg * jnp.abs(diff)), jnp.where(cm < cn, jnp.exp(lg * diff), 0.0))
    qd = jnp.exp(_log_gamma()[:, None] * (idx[None, :] + 1.0))
    kd = jnp.exp(_log_gamma()[:, None] * (R - 1.0 - idx[None, :]))
    gr = jnp.exp(_log_gamma() * R)
    qd_full = jnp.broadcast_to(qd[:, :, None], (RET_HEADS, R, RET_DV))
    kd_full = jnp.broadcast_to(kd.T[:, :, None], (R, RET_HEADS, RET_DK)).reshape(R, RET_HEADS * RET_DK)
    return mask.astype(F32), qd_full, kd_full, gr


def _rel_bucket():
    i = jnp.arange(CHUNK, dtype=jnp.int32)[:, None]
    j = jnp.arange(SWA_L, dtype=jnp.int32)[None, :]
    rel = (j - WINDOW) - i
    half = REL_BUCKETS // 2
    max_exact = half // 2
    n = jnp.abs(rel)
    large = max_exact + (jnp.log(jnp.maximum(n, 1).astype(F32) / max_exact)
                         / math.log(REL_MAX_DIST / max_exact) * (half - max_exact)).astype(jnp.int32)
    large = jnp.minimum(large, half - 1)
    return jnp.where(rel > 0, half, 0) + jnp.where(n < max_exact, n, large)


def _tile_rows(t, tm):
    reps = -(-tm // t.shape[0])
    return jnp.tile(t, (reps, 1)) if reps > 1 else t


def kernel(x_prompt, x_sample, cache_ret_state, cache_swa_k, cache_swa_v, cache_mem_k, cache_mem_v, mem_prompt,
           rel_bias, g_attn, w_in, w_ret_out, w_swa_out, w_mix_out, swa_sinks, g_cross, g_mem, w_cq, w_mk, w_mv,
           w_co, g_ffn, w_gate, w_up, w_down, g_final):
    B, S, D = x_prompt.shape
    Bs, T, _ = x_sample.shape
    assert D == D_MODEL and T == CHUNK and S % R_PROMPT == 0 and cache_swa_k.shape[2] == WINDOW
    assert g_attn.shape[0] == 1, "single layer"
    bf = lambda w: w.astype(BF16)
    vec = lambda g: g.reshape(1, D_MODEL)
    w_in_b, wro, wso, wmo = bf(w_in[0]), bf(w_ret_out[0]), bf(w_swa_out[0]), bf(w_mix_out[0])
    wcq, wco, wmk, wmv = bf(w_cq[0]), bf(w_co[0]), bf(w_mk[0]), bf(w_mv[0])
    wg, wu, wd = bf(w_gate[0]), bf(w_up[0]), bf(w_down[0])

    key_of_col, sub_of_col = _swa_col_maps()
    bucket2 = _rel_bucket()[:, key_of_col]
    bias2 = _rel_bias(rel_bias, bucket2, jnp.asarray(sub_of_col)[None, :])
    sinks = swa_sinks[0]

    def layer(x, pos, R, G, st0, hist, mk, mv, has_history):
        nb, seq, _ = x.shape
        x2d = x.reshape(nb * seq, D_MODEL)
        tm = G * R
        cos_t, sin_t = _rope_tables(pos)
        mt, qd, kd, gr = _retention_tables(R)
        h1, st, kv = _front(x2d, vec(g_attn[0]), w_in_b, _tile_rows(cos_t, tm), _tile_rows(sin_t, tm),
                            _tile_rows(kd, tm), st0, hist, mt, qd, gr, bias2, sinks, wro, wso, wmo,
                            G=G, R=R, n_tiles=seq // R, has_history=has_history)
        r_tail = TAIL_ROWS if (G == 1 and seq % TAIL_ROWS == 0) else R
        y = _tail(h1, mk, mv, vec(g_cross[0]), vec(g_ffn[0]), vec(g_final), wcq, wco, wg, wu, wd,
                  G=G, R=r_tail, tiles_per_mem=seq // r_tail)
        return y.reshape(nb, seq, D_MODEL), st, kv.reshape(nb, seq, 256)

    mk_f, mv_f, mk_b, mv_b = _mem_kv(mem_prompt.reshape(B * N_MEM, D_MODEL), vec(g_mem[0]), wmk, wmv, 512)
    y_p, st_p, kv_p = layer(
        x_prompt, jnp.arange(S, dtype=jnp.int32), R_PROMPT, 1,
        jnp.zeros((B, RET_HEADS, RET_DK, RET_DV), F32), jnp.zeros((B, WINDOW, 256), F32),
        mk_b.reshape(B, N_MEM, D_MODEL), mv_b.reshape(B, N_MEM, D_MODEL), False)

    Gs = 2 if Bs % 2 == 0 else 1
    hist_s = jnp.concatenate([cache_swa_k[0].reshape(Bs, WINDOW, LANES), cache_swa_v[0].reshape(Bs, WINDOW, LANES)],
                             axis=-1)
    y_s, st_s, kv_s = layer(
        x_sample, PAST_LEN + jnp.arange(T, dtype=jnp.int32), CHUNK, Gs,
        cache_ret_state[0].astype(F32), hist_s,
        bf(cache_mem_k[0].reshape(Bs, N_MEM, D_MODEL)), bf(cache_mem_v[0].reshape(Bs, N_MEM, D_MODEL)), True)

    kvshape = (1, B, WINDOW, SWA_KV, SWA_HD)
    k_p = kv_p[:, S - WINDOW:, :LANES].reshape(kvshape)
    v_p = kv_p[:, S - WINDOW:, LANES:].reshape(kvshape)
    k_s = jnp.concatenate([cache_swa_k[0][:, T:], kv_s[:, :, :LANES].reshape(Bs, T, SWA_KV, SWA_HD)], axis=1)[None]
    v_s = jnp.concatenate([cache_swa_v[0][:, T:], kv_s[:, :, LANES:].reshape(Bs, T, SWA_KV, SWA_HD)], axis=1)[None]
    mem_shape = (1, B, N_MEM, MEM_HEADS, MEM_HD)
    return (y_p, y_s, st_p[None], st_s[None], k_p, k_s, v_p, v_s, mk_f.reshape(mem_shape), mv_f.reshape(mem_shape))
```

```python
import functools
import math

import numpy as np
import jax
import jax.numpy as jnp
from jax import lax
from jax.experimental import pallas as pl
from jax.experimental.pallas import tpu as pltpu

F32 = jnp.float32
BF16 = jnp.bfloat16

D_MODEL = 1024
CHUNK = 64
EPS = 1e-6
NEG_INF = -1e30
PAST_LEN = 4096

RET_HEADS = 4
RET_DK = 256
RET_DV = 256
ROPE_BASE = 10000.0

SWA_HEADS = 16
SWA_KV = 2
SWA_GROUP = 8
SWA_HD = 64
WINDOW = 128
SWA_L = WINDOW + CHUNK

REL_BUCKETS = 32
REL_MAX_DIST = 128

N_MEM = 256
MEM_HEADS = 4
MEM_HD = 256
D_FF = 2816

OFF_RQ, OFF_RK, OFF_RV, OFF_RG, OFF_SQ, OFF_SKV, OFF_GA, OFF_GB = 0, 1024, 2048, 3072, 4096, 5120, 5376, 6400
D_IN = 7424

LANES = 128
HALF = LANES // 2
VMEM_LIMIT = 56 * 1024 * 1024

FF_COLS = 256
MERGE_COLS = 256
GATE_COLS = 512
KV_COLS = 2 * SWA_KV * SWA_HD
PIPE_DEPTH = 2
TAIL_ROWS = 512
R_PROMPT = 256


def _resident(shape):
    nd = len(shape)
    return pl.BlockSpec(shape, lambda *_: (0,) * nd, pipeline_mode=pl.Buffered(1))


def _rms(x):
    return x * lax.rsqrt(jnp.mean(x * x, axis=-1, keepdims=True) + EPS)


def _interleave(primary, secondary):
    out, j = [], 0
    for k, piece in enumerate(primary):
        out.append(piece)
        while j < len(secondary) and (j + 1) * len(primary) <= (k + 1) * len(secondary):
            out.append(secondary[j])
            j += 1
    return out + secondary[j:]


def _swa_col_maps():
    key = np.concatenate([np.arange(LANES), np.arange(LANES), LANES + np.arange(HALF), LANES + np.arange(HALF)])
    sub = np.concatenate([np.zeros(LANES), np.ones(LANES), np.zeros(HALF), np.ones(HALF)]).astype(np.int32)
    return key.astype(np.int32), sub


def _rel_bias_kernel(tab_ref, bucket_ref, sub_ref, out_ref):
    bucket = bucket_ref[...]
    is_b = sub_ref[...] > 0
    for hk in range(SWA_KV):
        for gp in range(SWA_GROUP // 2):
            h0 = hk * SWA_GROUP + gp * 2
            acc = jnp.zeros(bucket.shape, F32)
            for b in range(REL_BUCKETS):
                val = jnp.where(is_b, tab_ref[b, h0 + 1], tab_ref[b, h0])
                acc = jnp.where(bucket == b, val, acc)
            out_ref[hk, gp * CHUNK:(gp + 1) * CHUNK, :] = acc


def _rel_bias(table, bucket2, sub2):
    ncol = bucket2.shape[1]
    return pl.pallas_call(
        _rel_bias_kernel,
        in_specs=[pl.BlockSpec(memory_space=pltpu.SMEM),
                  pl.BlockSpec(memory_space=pltpu.VMEM),
                  pl.BlockSpec(memory_space=pltpu.VMEM)],
        out_specs=pl.BlockSpec(memory_space=pltpu.VMEM),
        out_shape=jax.ShapeDtypeStruct((SWA_KV, 4 * CHUNK, ncol), F32),
        name="rel_bias",
    )(table, bucket2, sub2)


def _front_kernel(x_ref, xp_ref, g_ref, w_ref, cos_ref, sin_ref, kdec_ref,
                  st0_ref, hist_ref, mt_ref, qd_ref, gr_ref, bias_ref, sink_ref,
                  wro_ref, wso_ref, wmo_ref,
                  h_ref, st_ref, kvo_ref,
                  xn, rq, rk, rkd, rv, rgs, sq, gas, gbs, state, ka, kb, va, vb, ain, swo, mixs,
                  *, G, R, n_tiles, has_history):
    t = pl.program_id(0)
    n_real = pl.num_programs(0) - 1
    i = lax.rem(jnp.minimum(t, n_real - 1), n_tiles)
    lane =lax.broadcasted_iota(jnp.int32, (1, LANES), 1)
    lo = lane < HALF

    def split_heads(x):
        r = pltpu.roll(x, HALF, axis=1)
        z = jnp.zeros_like(x)
        a0 = jnp.where(lo, x, z).astype(BF16)
        b0 = jnp.where(lo, z, r).astype(BF16)
        a1 = jnp.where(lo, r, z).astype(BF16)
        b1 = jnp.where(lo, z, x).astype(BF16)
        return (a0, a1), (b0, b1)

    @pl.when(t == 0)
    def _():
        ain[...] = jnp.zeros(ain.shape, BF16)
        swo[...] = jnp.zeros(swo.shape, BF16)
        gas[...] = jnp.zeros(gas.shape, BF16)
        gbs[...] = jnp.zeros(gbs.shape, BF16)

    @pl.when(i == 0)
    def _():
        state[...] = st0_ref[...]
        for g in range(G):
            hist = hist_ref[g]
            tail = slice(R, R + WINDOW)
            (a0, a1), (b0, b1) = split_heads(hist[:, :LANES])
            ka[g, 0, tail], ka[g, 1, tail], kb[g, 0, tail], kb[g, 1, tail] = a0, a1, b0, b1
            (a0, a1), (b0, b1) = split_heads(hist[:, LANES:])
            va[g, 0, tail], va[g, 1, tail], vb[g, 0, tail], vb[g, 1, tail] = a0, a1, b0, b1

    def norm():
        xn[...] = (_rms(x_ref[...]) * g_ref[...]).astype(BF16)

    def proj(c0, n):
        return jnp.dot(xn[...], w_ref[:, c0:c0 + n], preferred_element_type=F32)

    def proj_head(h):
        c = h * RET_DK
        cos = cos_ref[...]
        sin = sin_ref[...]
        acc = proj(OFF_RQ + c, RET_DK)
        x1, x2 = acc[:, :LANES], acc[:, LANES:]
        rq[:, c:c + LANES] = (x1 * cos - x2 * sin).astype(BF16)
        rq[:, c + LANES:c + RET_DK] = (x2 * cos + x1 * sin).astype(BF16)
        acc = proj(OFF_RK + c, RET_DK)
        x1, x2 = acc[:, :LANES], acc[:, LANES:]
        k1 = (x1 * cos - x2 * sin) * (RET_DK ** -0.5)
        k2 = (x2 * cos + x1 * sin) * (RET_DK ** -0.5)
        rk[:, c:c + LANES] = k1.astype(BF16)
        rk[:, c + LANES:c + RET_DK] = k2.astype(BF16)
        kd = kdec_ref[:, c:c + LANES]
        rkd[:, c:c + LANES] = (k1 * kd).astype(BF16)
        rkd[:, c + LANES:c + RET_DK] = (k2 * kd).astype(BF16)
        rv[:, c:c + RET_DV] = proj(OFF_RV + c, RET_DV).astype(BF16)
        rgs[:, c:c + RET_DV] = jax.nn.silu(proj(OFF_RG + c, RET_DV)).astype(BF16)

    def proj_sq(j):
        c = j * GATE_COLS
        sq[:, c:c + GATE_COLS] = proj(OFF_SQ + c, GATE_COLS).astype(BF16)

    def proj_kv():
        kvt_all = proj(OFF_SKV, KV_COLS)
        kvo_ref[...] = kvt_all
        for g in range(G):
            kvt = kvt_all[g * R:(g + 1) * R]
            for buf in (ka, kb, va, vb):
                for hk in range(SWA_KV):
                    buf[g, hk, :WINDOW] = buf[g, hk, R:R + WINDOW]
            (a0, a1), (b0, b1) = split_heads(kvt[:, :LANES])
            ka[g, 0, WINDOW:], ka[g, 1, WINDOW:], kb[g, 0, WINDOW:], kb[g, 1, WINDOW:] = a0, a1, b0, b1
            (a0, a1), (b0, b1) = split_heads(kvt[:, LANES:])
            va[g, 0, WINDOW:], va[g, 1, WINDOW:], vb[g, 0, WINDOW:], vb[g, 1, WINDOW:] = a0, a1, b0, b1

    def proj_gate(dst, off, j):
        c = j * GATE_COLS
        dst[:, c:c + GATE_COLS] = jax.nn.sigmoid(proj(off + c, GATE_COLS)).astype(BF16)

    def ret_front(g, h):
        rows = slice(g * R, (g + 1) * R)
        cols = slice(h * RET_DK, (h + 1) * RET_DK)
        q = rq[rows, cols]
        v = rv[rows, cols]
        s_prev = state[g, h]
        sc = lax.dot_general(q, rk[rows, cols], (((1,), (1,)), ((), ())), preferred_element_type=F32)
        qs = jnp.dot(q, s_prev.astype(BF16), preferred_element_type=F32)
        kv_new = lax.dot_general(rkd[rows, cols], v, (((0,), (0,)), ((), ())), preferred_element_type=F32)
        return sc, qs, kv_new, s_prev, v

    def ret_back(g, h, vals):
        sc, qs, kv_new, s_prev, v = vals
        rows = slice(g * R, (g + 1) * R)
        cols = slice(h * RET_DK, (h + 1) * RET_DK)
        o = jnp.dot((sc * mt_ref[h]).astype(BF16), v, preferred_element_type=F32) + qs * qd_ref[h]
        state[g, h] = s_prev * gr_ref[h] + kv_new
        ain[rows, cols] =(_rms(o) * rgs[rows, cols].astype(F32)).astype(BF16)

    ones_a = jnp.broadcast_to(jnp.where(lo, 1.0, 0.0).astype(BF16), (LANES, LANES))
    ones_b = jnp.broadcast_to(jnp.where(lo, 0.0, 1.0).astype(BF16), (LANES, LANES))
    lane3 = lax.broadcasted_iota(jnp.int32, (1, 3 * LANES), 1)
    key_of_col = jnp.where(lane3 < 2 * LANES, lane3 % LANES, LANES + lane3 % HALF)

    def swa_front(g, c, hk):
        w0 = c * CHUNK
        r0 = g * R + c * CHUNK
        q2 = jnp.concatenate(
            [sq[r0:r0 + CHUNK, hk * 512 + gp * LANES: hk * 512 + (gp + 1) * LANES] for gp in range(4)],
            axis=0)
        kbd = jnp.concatenate([ka[g, hk, w0:w0 + LANES], kb[g, hk, w0:w0 + LANES],
                               ka[g, hk, w0 + LANES:w0 + SWA_L], kb[g, hk, w0 + LANES:w0 + SWA_L]],
                              axis=0)
        return lax.dot_general(q2, kbd, (((1,), (1,)), ((), ())), preferred_element_type=F32)

    def swa_back(g, c, hk, s):
        w0 = c * CHUNK
        r0 = g * R + c * CHUNK
        s = s * (SWA_HD ** -0.5) + bias_ref[hk]
        if not has_history and c < WINDOW // CHUNK:
            kpos = key_of_col + (i * R + c * CHUNK - WINDOW)
            s = jnp.where(kpos >= 0, s, NEG_INF)
        t0, t1, t2 = s[:, :LANES], s[:, LANES:2 * LANES], s[:, 2 * LANES:]
        m_a = jnp.max(jnp.maximum(t0, jnp.where(lo, t2, NEG_INF)), axis=-1, keepdims=True)
        m_b = jnp.max(jnp.maximum(t1, jnp.where(lo, NEG_INF, t2)), axis=-1, keepdims=True)
        sk_a = jnp.concatenate([jnp.full((CHUNK, 1), sink_ref[hk * SWA_GROUP + 2 * gp], F32) for gp in range(4)],
                               axis=0)
        sk_b = jnp.concatenate([jnp.full((CHUNK, 1), sink_ref[hk * SWA_GROUP + 2 * gp + 1], F32) for gp in range(4)],
                               axis=0)
        m_a = jnp.maximum(m_a, sk_a)
        m_b = jnp.maximum(m_b, sk_b)
        e = jnp.concatenate([jnp.exp(t0 - m_a), jnp.exp(t1 - m_b),
                             jnp.exp(t2 - jnp.where(lo, m_a, m_b))], axis=1).astype(BF16)
        vbd = jnp.concatenate([
            jnp.concatenate([va[g, hk, w0:w0 + LANES], ones_a], axis=1),
            jnp.concatenate([vb[g, hk, w0:w0 + LANES], ones_b], axis=1),
            jnp.concatenate([va[g, hk, w0 + LANES:w0 + SWA_L], ones_a[:HALF]], axis=1),
            jnp.concatenate([vb[g, hk, w0 + LANES:w0 + SWA_L], ones_b[:HALF]], axis=1)], axis=0)
        oa = jnp.dot(e, vbd, preferred_element_type=F32)
        den = oa[:, LANES:] + jnp.where(lo, jnp.exp(sk_a - m_a), jnp.exp(sk_b - m_b))
        o2 = (oa[:, :LANES] / den).astype(BF16)
        for gp in range(4):
            swo[r0:r0 + CHUNK, hk * 512 + gp * LANES: hk * 512 + (gp + 1) * LANES] = o2[gp * CHUNK:(gp + 1) * CHUNK]

    def merge_ab(j):
        cols = slice(j * MERGE_COLS, (j + 1) * MERGE_COLS)
        a = jnp.dot(ain[...], wro_ref[:, cols], preferred_element_type=F32)
        b = jnp.dot(swo[...], wso_ref[:, cols], preferred_element_type=F32)
        mixs[:, cols] = (gas[:, cols].astype(F32) * a + gbs[:, cols].astype(F32) * b).astype(BF16)

    def merge_y(j):
        cols = slice(j * MERGE_COLS, (j + 1) * MERGE_COLS)
        h_ref[:, cols] = xp_ref[:, cols] + jnp.dot(mixs[...], wmo_ref[:, cols], preferred_element_type=F32)

    n_mc = D_MODEL // MERGE_COLS
    n_gc = D_MODEL // GATE_COLS
    dense = [(functools.partial(merge_ab, 0), None), (norm, None), (functools.partial(merge_ab, 1), None),
             (functools.partial(proj_head, 0), ("head", 0))]
    dense += [(functools.partial(merge_ab, j), ("merged", 0) if j == n_mc - 1 else None) for j in range(2, n_mc)]
    dense += [(functools.partial(proj_head, h), ("head", h)) for h in range(1, RET_HEADS)]
    dense += [(functools.partial(proj_sq, j), None) for j in range(n_gc)]
    dense.append((proj_kv, ("swa", 0)))
    dense += [(functools.partial(proj_gate, gas, OFF_GA, j), None) for j in range(n_gc)]
    dense += [(functools.partial(proj_gate, gbs, OFF_GB, j), None) for j in range(n_gc)]
    dense += [(functools.partial(merge_y, j), None) for j in range(n_mc)]

    items = [(ret_front, ret_back, (g, h), ("head", h)) for h in range(RET_HEADS) for g in range(G)]
    items += [(swa_front, swa_back, (g, c, hk), ("swa", 0))
              for g in range(G) for c in range(R // CHUNK) for hk in range(SWA_KV)]
    tasks = []
    for n in range(len(items) + PIPE_DEPTH):
        if n < len(items):
            tasks.append(("front", n))
        if n >= PIPE_DEPTH:
            tasks.append(("back", n - PIPE_DEPTH))

    pending, done, nxt = {}, set(), 0

    def run_task(kind, n):
        front, back, args, _ = items[n]
        if kind == "front":
            pending[n] = front(*args)
        else:
            back(*args, pending.pop(n))

    for k, (piece, tag) in enumerate(dense):
        piece()
        if tag is not None:
            done.add(tag)
        quota = -(-(len(tasks) - nxt) // (len(dense) - k))
        while nxt < len(tasks) and quota > 0:
            kind, n = tasks[nxt]
            if kind == "front" and items[n][3] not in done:
                break
            if kind == "back" and ("merged", 0) not in done:
                break
            run_task(kind, n)
            nxt += 1
            quota -= 1
    assert nxt == len(tasks)

    @pl.when((i == n_tiles - 1) & (t < n_real))
    def _():
        st_ref[...] = state[...]


def _front(x2d, g, w_in, cos_t, sin_t, kdec_t, st0, hist, mt, qd, gr, bias2, sinks, wro, wso, wmo,
           *, G, R, n_tiles, has_history):
    n = x2d.shape[0]
    tm = G * R
    n_steps = n // tm
    assert n_steps * tm == n and n_steps % n_tiles == 0 and (G == 1 or n_tiles == 1)
    assert cos_t.shape == (n_tiles * tm, LANES) and kdec_t.shape == (tm, D_MODEL)
    cur = lambda t: jnp.minimum(t, n_steps - 1)
    prev = lambda t: jnp.maximum(t - 1, 0)
    rope = pl.BlockSpec((tm, LANES), lambda t: (cur(t) % n_tiles, 0))
    per_row = pl.BlockSpec((G, RET_HEADS, RET_DK, RET_DV), lambda t: (cur(t) // n_tiles, 0, 0, 0))
    kern = functools.partial(_front_kernel, G=G, R=R, n_tiles=n_tiles, has_history=has_history)
    tile_bf = pltpu.VMEM((tm, D_MODEL), BF16)
    return pl.pallas_call(
        kern,
        grid=(n_steps + 1,),
        in_specs=[pl.BlockSpec((tm, D_MODEL), lambda t: (cur(t), 0)),
                  pl.BlockSpec((tm, D_MODEL), lambda t: (prev(t), 0)),
                  _resident((1, D_MODEL)), _resident(w_in.shape), rope, rope, _resident(kdec_t.shape),
                  per_row,
                  pl.BlockSpec((G, WINDOW, 256), lambda t: (cur(t) // n_tiles, 0, 0)),
                  _resident(mt.shape), _resident(qd.shape),
                  pl.BlockSpec(memory_space=pltpu.SMEM),
                  _resident(bias2.shape),
                  pl.BlockSpec(memory_space=pltpu.SMEM),
                  _resident(wro.shape), _resident(wso.shape), _resident(wmo.shape)],
        out_specs=[pl.BlockSpec((tm, D_MODEL), lambda t: (prev(t), 0)), per_row,
                   pl.BlockSpec((tm, 256), lambda t: (cur(t), 0))],
        out_shape=[jax.ShapeDtypeStruct((n, D_MODEL), F32),
                   jax.ShapeDtypeStruct((st0.shape[0], RET_HEADS, RET_DK, RET_DV), F32),
                   jax.ShapeDtypeStruct((n, 256), F32)],
        scratch_shapes=[tile_bf] * 9
                       + [pltpu.VMEM((G, RET_HEADS, RET_DK, RET_DV), F32)]
                       + [pltpu.VMEM((G, SWA_KV, WINDOW + R, LANES), BF16)] * 4
                       + [tile_bf] * 3,
        compiler_params=pltpu.CompilerParams(dimension_semantics=("arbitrary",), vmem_limit_bytes=VMEM_LIMIT),
        name="front",
    )(x2d, x2d, g, w_in, cos_t, sin_t, kdec_t, st0, hist, mt, qd, gr, bias2, sinks, wro, wso, wmo)


def _tail_kernel(h_ref, mk_ref, mv_ref, gc_ref, gf_ref, gl_ref, wcq_ref, wco_ref, wg_ref, wu_ref, wd_ref,
                 y_ref, qs, att, h2s, hn2s, h2_prev, hn2_prev, acts, h3s, *, G, R):
    @pl.when(pl.program_id(0) == 0)
    def _():
        h2s[...] = jnp.zeros(h2s.shape, F32)
        hn2s[...] = jnp.zeros(hn2s.shape, BF16)

    h2_prev[...] = h2s[...]
    hn2_prev[...] = hn2s[...]

    def q_proj():
        hn = (_rms(h_ref[...]) * gc_ref[...]).astype(BF16)
        qs[...] = jnp.dot(hn, wcq_ref[...], preferred_element_type=F32).astype(BF16)

    def head_front(g, hd):
        rows = slice(g * R, (g + 1) * R)
        cols = slice(hd * MEM_HD, (hd + 1) * MEM_HD)
        return lax.dot_general(qs[rows, cols], mk_ref[g, :, cols], (((1,), (1,)), ((), ())),
                               preferred_element_type=F32)

    def head_back(g, hd, s):
        rows = slice(g * R, (g + 1) * R)
        cols = slice(hd * MEM_HD, (hd + 1) * MEM_HD)
        s = s * (MEM_HD ** -0.5)
        e = jnp.exp(s - jnp.max(s, axis=-1, keepdims=True))
        den = jnp.sum(e, axis=-1, keepdims=True)
        o = jnp.dot(e.astype(BF16), mv_ref[g, :, cols], preferred_element_type=F32) / den
        att[rows, cols] = o.astype(BF16)

    def attn_out():
        h2 = h_ref[...] + jnp.dot(att[...], wco_ref[...], preferred_element_type=F32)
        h2s[...] = h2
        hn2s[...] = (_rms(h2) * gf_ref[...]).astype(BF16)

    def ff_chunk(j):
        cols = slice(j * FF_COLS, (j + 1) * FF_COLS)
        hn = hn2_prev[...]
        gate = jnp.dot(hn, wg_ref[:, cols], preferred_element_type=F32)
        up = jnp.dot(hn, wu_ref[:, cols], preferred_element_type=F32)
        acts[:, cols] = (jax.nn.silu(gate) * up).astype(BF16)

    def down_chunk(j):
        cols = slice(j * FF_COLS, (j + 1) * FF_COLS)
        h3s[:, cols] = h2_prev[:, cols] + jnp.dot(acts[...], wd_ref[:, cols], preferred_element_type=F32)

    def final_norm():
        y_ref[...] = _rms(h3s[...]) * gl_ref[...]

    heads = [(g, hd) for g in range(G) for hd in range(MEM_HEADS)]
    pending = {}

    def front(n):
        pending[n] = head_front(*heads[n])

    def back(n):
        head_back(*heads[n], pending.pop(n))

    attn = [q_proj]
    for n in range(len(heads) + 1):
        if n < len(heads):
            attn.append(functools.partial(front, n))
        if n >= 1:
            attn.append(functools.partial(back, n - 1))
    attn.append(attn_out)
    ffn = [functools.partial(ff_chunk, j) for j in range(D_FF // FF_COLS)]
    ffn += [functools.partial(down_chunk, j) for j in range(D_MODEL // FF_COLS)]
    ffn.append(final_norm)
    for piece in _interleave(ffn, attn):
        piece()


def _tail(h2d, mk, mv, gc, gf, gl, wcq, wco, wg, wu, wd, *, G, R, tiles_per_mem):
    n = h2d.shape[0]
    tm = G * R
    n_steps = n // tm
    assert n_steps * tm == n and (G == 1 or tiles_per_mem == 1)
    a_tile = lambda t: jnp.minimum(t, n_steps - 1)
    mem = pl.BlockSpec((G, N_MEM, D_MODEL), lambda t: (a_tile(t) // tiles_per_mem, 0, 0))
    vec = _resident((1, D_MODEL))
    kern = functools.partial(_tail_kernel, G=G, R=R)
    return pl.pallas_call(
        kern,
        grid=(n_steps + 1,),
        in_specs=[pl.BlockSpec((tm, D_MODEL), lambda t: (a_tile(t), 0)), mem, mem, vec, vec, vec,
                  _resident(wcq.shape), _resident(wco.shape), _resident(wg.shape), _resident(wu.shape),
                  _resident(wd.shape)],
        out_specs=pl.BlockSpec((tm, D_MODEL), lambda t: (jnp.maximum(t - 1, 0), 0)),
        out_shape=jax.ShapeDtypeStruct((n, D_MODEL), F32),
        scratch_shapes=[pltpu.VMEM((tm, D_MODEL), BF16), pltpu.VMEM((tm, D_MODEL), BF16),
                        pltpu.VMEM((tm, D_MODEL), F32), pltpu.VMEM((tm, D_MODEL), BF16),
                        pltpu.VMEM((tm, D_MODEL), F32), pltpu.VMEM((tm, D_MODEL), BF16),
                        pltpu.VMEM((tm, D_FF), BF16), pltpu.VMEM((tm, D_MODEL), F32)],
        compiler_params=pltpu.CompilerParams(dimension_semantics=("arbitrary",), vmem_limit_bytes=VMEM_LIMIT),
        name="tail",
    )(h2d, mk, mv, gc, gf, gl, wcq, wco, wg, wu, wd)


def _mem_kv_kernel(m_ref, g_ref, wk_ref, wv_ref, k_ref, v_ref, kb_ref, vb_ref):
    mn = (_rms(m_ref[...]) * g_ref[...]).astype(BF16)
    k = jnp.dot(mn, wk_ref[...], preferred_element_type=F32)
    v = jnp.dot(mn, wv_ref[...], preferred_element_type=F32)
    k_ref[...] = k
    v_ref[...] = v
    kb_ref[...] = k.astype(BF16)
    vb_ref[...] = v.astype(BF16)


def _mem_kv(mem2d, g, wk, wv, tm):
    n = mem2d.shape[0]
    assert n % tm == 0
    blk = pl.BlockSpec((tm, D_MODEL), lambda i: (i, 0))
    return pl.pallas_call(
        _mem_kv_kernel,
        grid=(n // tm,),
        in_specs=[blk, _resident((1, D_MODEL)), _resident(wk.shape), _resident(wv.shape)],
        out_specs=[blk] * 4,
        out_shape=[jax.ShapeDtypeStruct((n, D_MODEL), F32)] * 2 + [jax.ShapeDtypeStruct((n, D_MODEL), BF16)] * 2,
        compiler_params=pltpu.CompilerParams(dimension_semantics=("parallel",), vmem_limit_bytes=VMEM_LIMIT),
        name="mem_kv",
    )(mem2d, g, wk, wv)


def _log_gamma():
    return jnp.log(1.0 - 2.0 ** (-5.0 - jnp.arange(RET_HEADS, dtype=F32)))


def _rope_tables(pos):
    half = RET_DK // 2
    inv = ROPE_BASE ** (-jnp.arange(half, dtype=F32) / half)
    ang = pos.astype(F32)[:, None] * inv[None, :]
    return jnp.cos(ang), jnp.sin(ang)


def _retention_tables(R):
    lg = _log_gamma()[:, None, None]
    idx = jnp.arange(R, dtype=F32)
    diff = idx[:, None] - idx[None, :]
    cn = (jnp.arange(R) // CHUNK)[:, None]
    cm = (jnp.arange(R) // CHUNK)[None, :]
    mask = jnp.where(cm == cn, jnp.exp(lg * jnp.abs(diff)), jnp.where(cm < cn, jnp.exp(lg * diff), 0.0))
    qd = jnp.exp(_log_gamma()[:, None] * (idx[None, :] + 1.0))
    kd = jnp.exp(_log_gamma()[:, None] * (R - 1.0 - idx[None, :]))
    gr = jnp.exp(_log_gamma() * R)
    qd_full = jnp.broadcast_to(qd[:, :, None], (RET_HEADS, R, RET_DV))
    kd_full = jnp.broadcast_to(kd.T[:, :, None], (R, RET_HEADS, RET_DK)).reshape(R, RET_HEADS * RET_DK)
    return mask.astype(F32), qd_full, kd_full, gr


def _rel_bucket():
    i = jnp.arange(CHUNK, dtype=jnp.int32)[:, None]
    j = jnp.arange(SWA_L, dtype=jnp.int32)[None, :]
    rel = (j - WINDOW) - i
    half = REL_BUCKETS // 2
    max_exact = half // 2
    n = jnp.abs(rel)
    large = max_exact + (jnp.log(jnp.maximum(n, 1).astype(F32) / max_exact)
                         / math.log(REL_MAX_DIST / max_exact) * (half - max_exact)).astype(jnp.int32)
    large = jnp.minimum(large, half - 1)
    return jnp.where(rel > 0, half, 0) + jnp.where(n < max_exact, n, large)


def _tile_rows(t, tm):
    reps = -(-tm // t.shape[0])
    return jnp.tile(t, (reps, 1)) if reps > 1 else t


def kernel(x_prompt, x_sample, cache_ret_state, cache_swa_k, cache_swa_v, cache_mem_k, cache_mem_v, mem_prompt,
           rel_bias, g_attn, w_in, w_ret_out, w_swa_out, w_mix_out, swa_sinks, g_cross, g_mem, w_cq, w_mk, w_mv,
           w_co, g_ffn, w_gate, w_up, w_down, g_final):
    B, S, D = x_prompt.shape
    Bs, T, _ = x_sample.shape
    assert D == D_MODEL and T == CHUNK and S % R_PROMPT == 0 and cache_swa_k.shape[2] == WINDOW
    assert g_attn.shape[0] == 1, "single layer"
    bf = lambda w: w.astype(BF16)
    vec = lambda g: g.reshape(1, D_MODEL)
    w_in_b, wro, wso, wmo = bf(w_in[0]), bf(w_ret_out[0]), bf(w_swa_out[0]), bf(w_mix_out[0])
    wcq, wco, wmk, wmv = bf(w_cq[0]), bf(w_co[0]), bf(w_mk[0]), bf(w_mv[0])
    wg, wu, wd = bf(w_gate[0]), bf(w_up[0]), bf(w_down[0])

    key_of_col, sub_of_col = _swa_col_maps()
    bucket2 = _rel_bucket()[:, key_of_col]
    bias2 = _rel_bias(rel_bias, bucket2, jnp.asarray(sub_of_col)[None, :])
    sinks = swa_sinks[0]

    def layer(x, pos, R, G, st0, hist, mk, mv, has_history):
        nb, seq, _ = x.shape
        x2d = x.reshape(nb * seq, D_MODEL)
        tm = G * R
        cos_t, sin_t = _rope_tables(pos)
        mt, qd, kd, gr = _retention_tables(R)
        h1, st, kv = _front(x2d, vec(g_attn[0]), w_in_b, _tile_rows(cos_t, tm), _tile_rows(sin_t, tm),
                            _tile_rows(kd, tm), st0, hist, mt, qd, gr, bias2, sinks, wro, wso, wmo,
                            G=G, R=R, n_tiles=seq // R, has_history=has_history)
        r_tail = TAIL_ROWS if (G == 1 and seq % TAIL_ROWS == 0) else R
        y = _tail(h1, mk, mv, vec(g_cross[0]), vec(g_ffn[0]), vec(g_final), wcq, wco, wg, wu, wd,
                  G=G, R=r_tail, tiles_per_mem=seq // r_tail)
        return y.reshape(nb, seq, D_MODEL), st, kv.reshape(nb, seq, 256)

    mk_f, mv_f, mk_b, mv_b = _mem_kv(mem_prompt.reshape(B * N_MEM, D_MODEL), vec(g_mem[0]), wmk, wmv, 512)
    y_p, st_p, kv_p = layer(
        x_prompt, jnp.arange(S, dtype=jnp.int32), R_PROMPT, 1,
        jnp.zeros((B, RET_HEADS, RET_DK, RET_DV), F32), jnp.zeros((B, WINDOW, 256), F32),
        mk_b.reshape(B, N_MEM, D_MODEL), mv_b.reshape(B, N_MEM, D_MODEL), False)

    Gs = 2 if Bs % 2 == 0 else 1
    hist_s = jnp.concatenate([cache_swa_k[0].reshape(Bs, WINDOW, LANES), cache_swa_v[0].reshape(Bs, WINDOW, LANES)],
                             axis=-1)
    y_s, st_s, kv_s = layer(
        x_sample, PAST_LEN + jnp.arange(T, dtype=jnp.int32), CHUNK, Gs,
        cache_ret_state[0].astype(F32), hist_s,
        bf(cache_mem_k[0].reshape(Bs, N_MEM, D_MODEL)), bf(cache_mem_v[0].reshape(Bs, N_MEM, D_MODEL)), True)

    kvshape = (1, B, WINDOW, SWA_KV, SWA_HD)
    k_p = kv_p[:, S - WINDOW:, :LANES].reshape(kvshape)
    v_p = kv_p[:, S - WINDOW:, LANES:].reshape(kvshape)
    k_s = jnp.concatenate([cache_swa_k[0][:, T:], kv_s[:, :, :LANES].reshape(Bs, T, SWA_KV, SWA_HD)], axis=1)[None]
    v_s = jnp.concatenate([cache_swa_v[0][:, T:], kv_s[:, :, LANES:].reshape(Bs, T, SWA_KV, SWA_HD)], axis=1)[None]
    mem_shape = (1, B, N_MEM, MEM_HEADS, MEM_HD)
    return (y_p, y_s, st_p[None], st_s[None], k_p, k_s, v_p, v_s, mk_f.reshape(mem_shape), mv_f.reshape(mem_shape))
```

```python
import functools
import math

import numpy as np
import jax
import jax.numpy as jnp
from jax import lax
from jax.experimental import pallas as pl
from jax.experimental.pallas import tpu as pltpu

F32 = jnp.float32
BF16 = jnp.bfloat16

D_MODEL = 1024
CHUNK = 64
EPS = 1e-6
NEG_INF = -1e30
PAST_LEN = 4096

RET_HEADS = 4
RET_DK = 256
RET_DV = 256
ROPE_BASE = 10000.0

SWA_HEADS = 16
SWA_KV = 2
SWA_GROUP = 8
SWA_HD = 64
WINDOW = 128
SWA_L = WINDOW + CHUNK

REL_BUCKETS = 32
REL_MAX_DIST = 128

N_MEM = 256
MEM_HEADS = 4
MEM_HD = 256
D_FF = 2816

OFF_RQ, OFF_RK, OFF_RV, OFF_RG, OFF_SQ, OFF_SKV, OFF_GA, OFF_GB = 0, 1024, 2048, 3072, 4096, 5120, 5376, 6400
D_IN = 7424

LANES = 128
HALF = LANES // 2
VMEM_LIMIT = 56 * 1024 * 1024

FF_COLS = 256
MERGE_COLS = 256
GATE_COLS = 512
KV_COLS = 2 * SWA_KV * SWA_HD
PIPE_DEPTH = 2
TAIL_ROWS = 512
R_PROMPT = 256


def _resident(shape):
    nd = len(shape)
    return pl.BlockSpec(shape, lambda *_: (0,) * nd, pipeline_mode=pl.Buffered(1))


def _rms(x):
    return x * lax.rsqrt(jnp.mean(x * x, axis=-1, keepdims=True) + EPS)


def _interleave(primary, secondary):
    out, j = [], 0
    for k, piece in enumerate(primary):
        out.append(piece)
        while j < len(secondary) and (j + 1) * len(primary) <= (k + 1) * len(secondary):
            out.append(secondary[j])
            j += 1
    return out + secondary[j:]


def _swa_col_maps():
    key = np.concatenate([np.arange(LANES), np.arange(LANES), LANES + np.arange(HALF), LANES + np.arange(HALF)])
    sub = np.concatenate([np.zeros(LANES), np.ones(LANES), np.zeros(HALF), np.ones(HALF)]).astype(np.int32)
    return key.astype(np.int32), sub


def _rel_bias_kernel(tab_ref, bucket_ref, sub_ref, out_ref):
    bucket = bucket_ref[...]
    is_b = sub_ref[...] > 0
    for hk in range(SWA_KV):
        for gp in range(SWA_GROUP // 2):
            h0 = hk * SWA_GROUP + gp * 2
            acc = jnp.zeros(bucket.shape, F32)
            for b in range(REL_BUCKETS):
                val = jnp.where(is_b, tab_ref[b, h0 + 1], tab_ref[b, h0])
                acc = jnp.where(bucket == b, val, acc)
            out_ref[hk, gp * CHUNK:(gp + 1) * CHUNK, :] = acc


def _rel_bias(table, bucket2, sub2):
    ncol = bucket2.shape[1]
    return pl.pallas_call(
        _rel_bias_kernel,
        in_specs=[pl.BlockSpec(memory_space=pltpu.SMEM),
                  pl.BlockSpec(memory_space=pltpu.VMEM),
                  pl.BlockSpec(memory_space=pltpu.VMEM)],
        out_specs=pl.BlockSpec(memory_space=pltpu.VMEM),
        out_shape=jax.ShapeDtypeStruct((SWA_KV, 4 * CHUNK, ncol), F32),
        name="rel_bias",
    )(table, bucket2, sub2)


def _front_kernel(x_ref, g_ref, w_ref, cos_ref, sin_ref, kdec_ref,
                  st0_ref, hist_ref, mt_ref, qd_ref, gr_ref, bias_ref, sink_ref,
                  wro_ref, wso_ref, wmo_ref,
                  h_ref, st_ref, kvl_ref,
                  xn, rq, rk, rkd, rv, rgs, sq, gas, gbs, state, ka, kb, va, vb, ain, swo, mixs, x_prev,
                  *, G, R, n_tiles, has_history):
    t = pl.program_id(0)
    n_real = pl.num_programs(0) - 1
    i = lax.rem(jnp.minimum(t, n_real - 1), n_tiles)
    lane =lax.broadcasted_iota(jnp.int32, (1, LANES), 1)
    lo = lane < HALF

    def split_heads(x):
        r = pltpu.roll(x, HALF, axis=1)
        z = jnp.zeros_like(x)
        a0 = jnp.where(lo, x, z).astype(BF16)
        b0 = jnp.where(lo, z, r).astype(BF16)
        a1 = jnp.where(lo, r, z).astype(BF16)
        b1 = jnp.where(lo, z, x).astype(BF16)
        return (a0, a1), (b0, b1)

    @pl.when(t == 0)
    def _():
        ain[...] = jnp.zeros(ain.shape, BF16)
        swo[...] = jnp.zeros(swo.shape, BF16)
        x_prev[...] = jnp.zeros(x_prev.shape, F32)
        gas[...] = jnp.zeros(gas.shape, BF16)
        gbs[...] = jnp.zeros(gbs.shape, BF16)

    @pl.when(i == 0)
    def _():
        state[...] = st0_ref[...] if has_history else jnp.zeros(state.shape, F32)
        for g in range(G):
            hist = hist_ref[g] if has_history else jnp.zeros((WINDOW, KV_COLS), F32)
            tail = slice(R, R + WINDOW)
            (a0, a1), (b0, b1) = split_heads(hist[:, :LANES])
            ka[g, 0, tail], ka[g, 1, tail], kb[g, 0, tail], kb[g, 1, tail] = a0, a1, b0, b1
            (a0, a1), (b0, b1) = split_heads(hist[:, LANES:])
            va[g, 0, tail], va[g, 1, tail], vb[g, 0, tail], vb[g, 1, tail] = a0, a1, b0, b1

    def norm():
        xn[...] = (_rms(x_ref[...]) * g_ref[...]).astype(BF16)

    def proj(c0, n):
        return jnp.dot(xn[...], w_ref[:, c0:c0 + n], preferred_element_type=F32)

    def proj_head(h):
        c = h * RET_DK
        tile_rows = pl.ds(pl.multiple_of(i * (G * R), G * R), G * R)
        cos = cos_ref[tile_rows, :]
        sin = sin_ref[tile_rows, :]
        acc = proj(OFF_RQ + c, RET_DK)
        x1, x2 = acc[:, :LANES], acc[:, LANES:]
        rq[:, c:c + LANES] = (x1 * cos - x2 * sin).astype(BF16)
        rq[:, c + LANES:c + RET_DK] = (x2 * cos + x1 * sin).astype(BF16)
        acc = proj(OFF_RK + c, RET_DK)
        x1, x2 = acc[:, :LANES], acc[:, LANES:]
        k1 = (x1 * cos - x2 * sin) * (RET_DK ** -0.5)
        k2 = (x2 * cos + x1 * sin) * (RET_DK ** -0.5)
        rk[:, c:c + LANES] = k1.astype(BF16)
        rk[:, c + LANES:c + RET_DK] = k2.astype(BF16)
        kd = kdec_ref[:, c:c + LANES]
        rkd[:, c:c + LANES] = (k1 * kd).astype(BF16)
        rkd[:, c + LANES:c + RET_DK] = (k2 * kd).astype(BF16)
        rv[:, c:c + RET_DV] = proj(OFF_RV + c, RET_DV).astype(BF16)
        rgs[:, c:c + RET_DV] = jax.nn.silu(proj(OFF_RG + c, RET_DV)).astype(BF16)

    def proj_sq(j):
        c = j * GATE_COLS
        sq[:, c:c + GATE_COLS] = proj(OFF_SQ + c, GATE_COLS).astype(BF16)

    def proj_kv():
        kvt_all = proj(OFF_SKV, KV_COLS)
        for g in range(G):
            kvl_ref[g] = kvt_all[(g + 1) * R - min(R, WINDOW):(g + 1) * R]
        for g in range(G):
            kvt = kvt_all[g * R:(g + 1) * R]
            for buf in (ka, kb, va, vb):
                for hk in range(SWA_KV):
                    buf[g, hk, :WINDOW] = buf[g, hk, R:R + WINDOW]
            (a0, a1), (b0, b1) = split_heads(kvt[:, :LANES])
            ka[g, 0, WINDOW:], ka[g, 1, WINDOW:], kb[g, 0, WINDOW:], kb[g, 1, WINDOW:] = a0, a1, b0, b1
            (a0, a1), (b0, b1) = split_heads(kvt[:, LANES:])
            va[g, 0, WINDOW:], va[g, 1, WINDOW:], vb[g, 0, WINDOW:], vb[g, 1, WINDOW:] = a0, a1, b0, b1

    def proj_gate(dst, off, j):
        c = j * GATE_COLS
        dst[:, c:c + GATE_COLS] = jax.nn.sigmoid(proj(off + c, GATE_COLS)).astype(BF16)

    def ret_front(g, h):
        rows = slice(g * R, (g + 1) * R)
        cols = slice(h * RET_DK, (h + 1) * RET_DK)
        q = rq[rows, cols]
        v = rv[rows, cols]
        s_prev = state[g, h]
        sc = lax.dot_general(q, rk[rows, cols], (((1,), (1,)), ((), ())), preferred_element_type=F32)
        qs = jnp.dot(q, s_prev.astype(BF16), preferred_element_type=F32)
        kv_new = lax.dot_general(rkd[rows, cols], v, (((0,), (0,)), ((), ())), preferred_element_type=F32)
        return sc, qs, kv_new, s_prev, v

    def ret_back(g, h, vals):
        sc, qs, kv_new, s_prev, v = vals
        rows = slice(g * R, (g + 1) * R)
        cols = slice(h * RET_DK, (h + 1) * RET_DK)
        o = jnp.dot((sc * mt_ref[h]).astype(BF16), v, preferred_element_type=F32) + qs * qd_ref[h]
        state[g, h] = s_prev * gr_ref[h] + kv_new
        ain[rows, cols] =(_rms(o) * rgs[rows, cols].astype(F32)).astype(BF16)

    ones_a = jnp.broadcast_to(jnp.where(lo, 1.0, 0.0).astype(BF16), (LANES, LANES))
    ones_b = jnp.broadcast_to(jnp.where(lo, 0.0, 1.0).astype(BF16), (LANES, LANES))
    lane3 = lax.broadcasted_iota(jnp.int32, (1, 3 * LANES), 1)
    key_of_col = jnp.where(lane3 < 2 * LANES, lane3 % LANES, LANES + lane3 % HALF)

    def swa_front(g, c, hk):
        w0 = c * CHUNK
        r0 = g * R + c * CHUNK
        q2 = jnp.concatenate(
            [sq[r0:r0 + CHUNK, hk * 512 + gp * LANES: hk * 512 + (gp + 1) * LANES] for gp in range(4)],
            axis=0)
        kbd = jnp.concatenate([ka[g, hk, w0:w0 + LANES], kb[g, hk, w0:w0 + LANES],
                               ka[g, hk, w0 + LANES:w0 + SWA_L], kb[g, hk, w0 + LANES:w0 + SWA_L]],
                              axis=0)
        return lax.dot_general(q2, kbd, (((1,), (1,)), ((), ())), preferred_element_type=F32)

    def swa_back(g, c, hk, s):
        w0 = c * CHUNK
        r0 = g * R + c * CHUNK
        s = s * (SWA_HD ** -0.5) + bias_ref[hk]
        if not has_history and c < WINDOW // CHUNK:
            kpos = key_of_col + (i * R + c * CHUNK - WINDOW)
            s = jnp.where(kpos >= 0, s, NEG_INF)
        t0, t1, t2 = s[:, :LANES], s[:, LANES:2 * LANES], s[:, 2 * LANES:]
        m_a = jnp.max(jnp.maximum(t0, jnp.where(lo, t2, NEG_INF)), axis=-1, keepdims=True)
        m_b = jnp.max(jnp.maximum(t1, jnp.where(lo, NEG_INF, t2)), axis=-1, keepdims=True)
        sk_a = jnp.concatenate([jnp.full((CHUNK, 1), sink_ref[hk * SWA_GROUP + 2 * gp], F32) for gp in range(4)],
                               axis=0)
        sk_b = jnp.concatenate([jnp.full((CHUNK, 1), sink_ref[hk * SWA_GROUP + 2 * gp + 1], F32) for gp in range(4)],
                               axis=0)
        m_a = jnp.maximum(m_a, sk_a)
        m_b = jnp.maximum(m_b, sk_b)
        e = jnp.concatenate([jnp.exp(t0 - m_a), jnp.exp(t1 - m_b),
                             jnp.exp(t2 - jnp.where(lo, m_a, m_b))], axis=1).astype(BF16)
        vbd = jnp.concatenate([
            jnp.concatenate([va[g, hk, w0:w0 + LANES], ones_a], axis=1),
            jnp.concatenate([vb[g, hk, w0:w0 + LANES], ones_b], axis=1),
            jnp.concatenate([va[g, hk, w0 + LANES:w0 + SWA_L], ones_a[:HALF]], axis=1),
            jnp.concatenate([vb[g, hk, w0 + LANES:w0 + SWA_L], ones_b[:HALF]], axis=1)], axis=0)
        oa = jnp.dot(e, vbd, preferred_element_type=F32)
        den = oa[:, LANES:] + jnp.where(lo, jnp.exp(sk_a - m_a), jnp.exp(sk_b - m_b))
        o2 = (oa[:, :LANES] / den).astype(BF16)
        for gp in range(4):
            swo[r0:r0 + CHUNK, hk * 512 + gp * LANES: hk * 512 + (gp + 1) * LANES] = o2[gp * CHUNK:(gp + 1) * CHUNK]

    def merge_ab(j):
        cols = slice(j * MERGE_COLS, (j + 1) * MERGE_COLS)
        a = jnp.dot(ain[...], wro_ref[:, cols], preferred_element_type=F32)
        b = jnp.dot(swo[...], wso_ref[:, cols], preferred_element_type=F32)
        mixs[:, cols] = (gas[:, cols].astype(F32) * a + gbs[:, cols].astype(F32) * b).astype(BF16)

    def merge_y(j):
        cols = slice(j * MERGE_COLS, (j + 1) * MERGE_COLS)
        h_ref[:, cols] = x_prev[:, cols] + jnp.dot(mixs[...], wmo_ref[:, cols], preferred_element_type=F32)

    def keep_x():
        x_prev[...] = x_ref[...]

    n_mc = D_MODEL // MERGE_COLS
    n_gc = D_MODEL // GATE_COLS
    dense = [(functools.partial(merge_ab, 0), None), (norm, None), (functools.partial(merge_ab, 1), None),
             (functools.partial(proj_head, 0), ("head", 0))]
    dense += [(functools.partial(merge_ab, j), ("merged", 0) if j == n_mc - 1 else None) for j in range(2, n_mc)]
    dense += [(functools.partial(proj_head, h), ("head", h)) for h in range(1, RET_HEADS)]
    dense += [(functools.partial(proj_sq, j), None) for j in range(n_gc)]
    dense.append((proj_kv, ("swa", 0)))
    dense += [(functools.partial(proj_gate, gas, OFF_GA, j), None) for j in range(n_gc)]
    dense += [(functools.partial(proj_gate, gbs, OFF_GB, j), None) for j in range(n_gc)]
    dense += [(functools.partial(merge_y, j), None) for j in range(n_mc)]
    dense.append((keep_x, None))

    items =[(ret_front, ret_back, (g, h), ("head", h)) for h in range(RET_HEADS) for g in range(G)]
    items += [(swa_front, swa_back, (g, c, hk), ("swa", 0))
              for g in range(G) for c in range(R // CHUNK) for hk in range(SWA_KV)]
    tasks = []
    for n in range(len(items) + PIPE_DEPTH):
        if n < len(items):
            tasks.append(("front", n))
        if n >= PIPE_DEPTH:
            tasks.append(("back", n - PIPE_DEPTH))

    pending, done, nxt = {}, set(), 0

    def run_task(kind, n):
        front, back, args, _ = items[n]
        if kind == "front":
            pending[n] = front(*args)
        else:
            back(*args, pending.pop(n))

    for k, (piece, tag) in enumerate(dense):
        piece()
        if tag is not None:
            done.add(tag)
        quota = -(-(len(tasks) - nxt) // (len(dense) - k))
        while nxt < len(tasks) and quota > 0:
            kind, n = tasks[nxt]
            if kind == "front" and items[n][3] not in done:
                break
            if kind == "back" and ("merged", 0) not in done:
                break
            run_task(kind, n)
            nxt += 1
            quota -= 1
    assert nxt == len(tasks)

    @pl.when((i == n_tiles - 1) & (t < n_real))
    def _():
        st_ref[...] = state[...]


def _front(x2d, g, w_in, cos_t, sin_t, kdec_t, st0, hist, mt, qd, gr, bias2, sinks, wro, wso, wmo,
           *, G, R, n_tiles, has_history):
    n = x2d.shape[0]
    tm = G * R
    n_steps = n // tm
    n_rows = n // (R * n_tiles)
    keep = min(R, WINDOW)
    assert n_steps * tm == n and n_steps % n_tiles == 0 and (G == 1 or n_tiles == 1)
    assert cos_t.shape == (n_tiles * tm, LANES) and kdec_t.shape == (tm, D_MODEL)
    cur = lambda t: jnp.minimum(t, n_steps - 1)
    prev = lambda t: jnp.maximum(t - 1, 0)
    per_row = pl.BlockSpec((G, RET_HEADS, RET_DK, RET_DV), lambda t: (cur(t) // n_tiles, 0, 0, 0))
    kw = dict(G=G, R=R, n_tiles=n_tiles, has_history=has_history)
    if has_history:
        kern = functools.partial(_front_kernel, **kw)
        carried = [st0, hist]
        carried_specs = [per_row, pl.BlockSpec((G, WINDOW, KV_COLS), lambda t: (cur(t) // n_tiles, 0, 0))]
    else:
        def kern(x_ref, g_ref, w_ref, cos_ref, sin_ref, kdec_ref, *rest):
            _front_kernel(x_ref, g_ref, w_ref, cos_ref, sin_ref, kdec_ref, None, None, *rest, **kw)
        carried, carried_specs = [], []
    tile_bf = pltpu.VMEM((tm, D_MODEL), BF16)
    return pl.pallas_call(
        kern,
        grid=(n_steps + 1,),
        in_specs=[pl.BlockSpec((tm, D_MODEL), lambda t: (cur(t), 0)),
                  _resident((1, D_MODEL)), _resident(w_in.shape), _resident(cos_t.shape), _resident(sin_t.shape),
                  _resident(kdec_t.shape)]
                 + carried_specs
                 + [_resident(mt.shape), _resident(qd.shape),
                  pl.BlockSpec(memory_space=pltpu.SMEM),
                  _resident(bias2.shape),
                  pl.BlockSpec(memory_space=pltpu.SMEM),
                  _resident(wro.shape), _resident(wso.shape), _resident(wmo.shape)],
        out_specs=[pl.BlockSpec((tm, D_MODEL), lambda t: (prev(t), 0)), per_row,
                   pl.BlockSpec((G, keep, KV_COLS), lambda t: (cur(t) // n_tiles, 0, 0))],
        out_shape=[jax.ShapeDtypeStruct((n, D_MODEL), F32),
                   jax.ShapeDtypeStruct((n_rows, RET_HEADS, RET_DK, RET_DV), F32),
                   jax.ShapeDtypeStruct((n_rows, keep, KV_COLS), F32)],
        scratch_shapes=[tile_bf] * 9
                       + [pltpu.VMEM((G, RET_HEADS, RET_DK, RET_DV), F32)]
                       + [pltpu.VMEM((G, SWA_KV, WINDOW + R, LANES), BF16)] * 4
                       + [tile_bf] * 3
                       + [pltpu.VMEM((tm, D_MODEL), F32)],
        compiler_params=pltpu.CompilerParams(dimension_semantics=("arbitrary",), vmem_limit_bytes=VMEM_LIMIT),
        name="front",
    )(x2d, g, w_in, cos_t, sin_t, kdec_t, *carried, mt, qd, gr, bias2, sinks, wro, wso, wmo)


def _tail_kernel(h_ref, mk_ref, mv_ref, gc_ref, gf_ref, gl_ref, wcq_ref, wco_ref, wg_ref, wu_ref, wd_ref,
                 y_ref, qs, att, h2s, hn2s, h2_prev, hn2_prev, acts, h3s, *, G, R):
    @pl.when(pl.program_id(0) == 0)
    def _():
        h2s[...] = jnp.zeros(h2s.shape, F32)
        hn2s[...] = jnp.zeros(hn2s.shape, BF16)

    h2_prev[...] = h2s[...]
    hn2_prev[...] = hn2s[...]

    def q_proj():
        hn = (_rms(h_ref[...]) * gc_ref[...]).astype(BF16)
        qs[...] = jnp.dot(hn, wcq_ref[...], preferred_element_type=F32).astype(BF16)

    def head_front(g, hd):
        rows = slice(g * R, (g + 1) * R)
        cols = slice(hd * MEM_HD, (hd + 1) * MEM_HD)
        return lax.dot_general(qs[rows, cols], mk_ref[g, :, cols], (((1,), (1,)), ((), ())),
                               preferred_element_type=F32)

    def head_back(g, hd, s):
        rows = slice(g * R, (g + 1) * R)
        cols = slice(hd * MEM_HD, (hd + 1) * MEM_HD)
        s = s * (MEM_HD ** -0.5)
        e = jnp.exp(s - jnp.max(s, axis=-1, keepdims=True))
        den = jnp.sum(e, axis=-1, keepdims=True)
        o = jnp.dot(e.astype(BF16), mv_ref[g, :, cols], preferred_element_type=F32) / den
        att[rows, cols] = o.astype(BF16)

    def attn_out():
        h2 = h_ref[...] + jnp.dot(att[...], wco_ref[...], preferred_element_type=F32)
        h2s[...] = h2
        hn2s[...] = (_rms(h2) * gf_ref[...]).astype(BF16)

    def ff_chunk(j):
        cols = slice(j * FF_COLS, (j + 1) * FF_COLS)
        hn = hn2_prev[...]
        gate = jnp.dot(hn, wg_ref[:, cols], preferred_element_type=F32)
        up = jnp.dot(hn, wu_ref[:, cols], preferred_element_type=F32)
        acts[:, cols] = (jax.nn.silu(gate) * up).astype(BF16)

    def down_chunk(j):
        cols = slice(j * FF_COLS, (j + 1) * FF_COLS)
        h3s[:, cols] = h2_prev[:, cols] + jnp.dot(acts[...], wd_ref[:, cols], preferred_element_type=F32)

    def final_norm():
        y_ref[...] = _rms(h3s[...]) * gl_ref[...]

    heads = [(g, hd) for g in range(G) for hd in range(MEM_HEADS)]
    pending = {}

    def front(n):
        pending[n] = head_front(*heads[n])

    def back(n):
        head_back(*heads[n], pending.pop(n))

    attn = [q_proj]
    for n in range(len(heads) + 1):
        if n < len(heads):
            attn.append(functools.partial(front, n))
        if n >= 1:
            attn.append(functools.partial(back, n - 1))
    attn.append(attn_out)
    ffn = [functools.partial(ff_chunk, j) for j in range(D_FF // FF_COLS)]
    ffn += [functools.partial(down_chunk, j) for j in range(D_MODEL // FF_COLS)]
    ffn.append(final_norm)
    for piece in _interleave(ffn, attn):
        piece()


def _tail(h2d, mk, mv, gc, gf, gl, wcq, wco, wg, wu, wd, *, G, R, tiles_per_mem):
    n = h2d.shape[0]
    tm = G * R
    n_steps = n // tm
    assert n_steps * tm == n and (G == 1 or tiles_per_mem == 1)
    a_tile = lambda t: jnp.minimum(t, n_steps - 1)
    mem = pl.BlockSpec((G, N_MEM, D_MODEL), lambda t: (a_tile(t) // tiles_per_mem, 0, 0))
    vec = _resident((1, D_MODEL))
    kern = functools.partial(_tail_kernel, G=G, R=R)
    return pl.pallas_call(
        kern,
        grid=(n_steps + 1,),
        in_specs=[pl.BlockSpec((tm, D_MODEL), lambda t: (a_tile(t), 0)), mem, mem, vec, vec, vec,
                  _resident(wcq.shape), _resident(wco.shape), _resident(wg.shape), _resident(wu.shape),
                  _resident(wd.shape)],
        out_specs=pl.BlockSpec((tm, D_MODEL), lambda t: (jnp.maximum(t - 1, 0), 0)),
        out_shape=jax.ShapeDtypeStruct((n, D_MODEL), F32),
        scratch_shapes=[pltpu.VMEM((tm, D_MODEL), BF16), pltpu.VMEM((tm, D_MODEL), BF16),
                        pltpu.VMEM((tm, D_MODEL), F32), pltpu.VMEM((tm, D_MODEL), BF16),
                        pltpu.VMEM((tm, D_MODEL), F32), pltpu.VMEM((tm, D_MODEL), BF16),
                        pltpu.VMEM((tm, D_FF), BF16), pltpu.VMEM((tm, D_MODEL), F32)],
        compiler_params=pltpu.CompilerParams(dimension_semantics=("arbitrary",), vmem_limit_bytes=VMEM_LIMIT),
        name="tail",
    )(h2d, mk, mv, gc, gf, gl, wcq, wco, wg, wu, wd)


def _mem_kv_kernel(m_ref, g_ref, wk_ref, wv_ref, k_ref, v_ref, kb_ref, vb_ref):
    mn = (_rms(m_ref[...]) * g_ref[...]).astype(BF16)
    k = jnp.dot(mn, wk_ref[...], preferred_element_type=F32)
    v = jnp.dot(mn, wv_ref[...], preferred_element_type=F32)
    k_ref[...] = k
    v_ref[...] = v
    kb_ref[...] = k.astype(BF16)
    vb_ref[...] = v.astype(BF16)


def _mem_kv(mem2d, g, wk, wv, tm):
    n = mem2d.shape[0]
    assert n % tm == 0
    blk = pl.BlockSpec((tm, D_MODEL), lambda i: (i, 0))
    return pl.pallas_call(
        _mem_kv_kernel,
        grid=(n // tm,),
        in_specs=[blk, _resident((1, D_MODEL)), _resident(wk.shape), _resident(wv.shape)],
        out_specs=[blk] * 4,
        out_shape=[jax.ShapeDtypeStruct((n, D_MODEL), F32)] * 2 + [jax.ShapeDtypeStruct((n, D_MODEL), BF16)] * 2,
        compiler_params=pltpu.CompilerParams(dimension_semantics=("parallel",), vmem_limit_bytes=VMEM_LIMIT),
        name="mem_kv",
    )(mem2d, g, wk, wv)


def _log_gamma():
    return np.log(np.float32(1.0) - np.float32(2.0) ** (np.float32(-5.0) - np.arange(RET_HEADS, dtype=np.float32)))


def _rope_tables(pos):
    half = RET_DK // 2
    inv = np.float32(ROPE_BASE) ** (-np.arange(half, dtype=np.float32) / np.float32(half))
    ang = pos.astype(np.float32)[:, None] * inv[None, :]
    return np.cos(ang).astype(np.float32), np.sin(ang).astype(np.float32)


def _retention_tables(R):
    lg = _log_gamma()[:, None, None]
    idx = np.arange(R, dtype=np.float32)
    diff = idx[:, None] - idx[None, :]
    cn = (np.arange(R) // CHUNK)[:, None]
    cm = (np.arange(R) // CHUNK)[None, :]
    mask = np.where(cm == cn, np.exp(lg * np.abs(diff)), np.where(cm < cn, np.exp(lg * diff), np.float32(0.0)))
    qd = np.exp(_log_gamma()[:, None] * (idx[None, :] + np.float32(1.0)))
    kd = np.exp(_log_gamma()[:, None] * (np.float32(R - 1.0) - idx[None, :]))
    gr = np.exp(_log_gamma() * np.float32(R))
    qd_full = np.broadcast_to(qd[:, :, None], (RET_HEADS, R, RET_DV))
    kd_full = np.broadcast_to(kd.T[:, :, None], (R, RET_HEADS, RET_DK)).reshape(R, RET_HEADS * RET_DK)
    f32 = lambda a: jnp.asarray(np.ascontiguousarray(a, dtype=np.float32))
    return f32(mask), f32(qd_full), kd_full.astype(np.float32), f32(gr)


def _rel_bucket():
    i = jnp.arange(CHUNK, dtype=jnp.int32)[:, None]
    j = jnp.arange(SWA_L, dtype=jnp.int32)[None, :]
    rel = (j - WINDOW) - i
    half = REL_BUCKETS // 2
    max_exact = half // 2
    n = jnp.abs(rel)
    large = max_exact + (jnp.log(jnp.maximum(n, 1).astype(F32) / max_exact)
                         / math.log(REL_MAX_DIST / max_exact) * (half - max_exact)).astype(jnp.int32)
    large = jnp.minimum(large, half - 1)
    return jnp.where(rel > 0, half, 0) + jnp.where(n < max_exact, n, large)


def _tile_rows(t, tm):
    reps = -(-tm // t.shape[0])
    return jnp.asarray(np.tile(t, (reps, 1)) if reps > 1 else t)


def kernel(x_prompt, x_sample, cache_ret_state, cache_swa_k, cache_swa_v, cache_mem_k, cache_mem_v, mem_prompt,
           rel_bias, g_attn, w_in, w_ret_out, w_swa_out, w_mix_out, swa_sinks, g_cross, g_mem, w_cq, w_mk, w_mv,
           w_co, g_ffn, w_gate, w_up, w_down, g_final):
    B, S, D = x_prompt.shape
    Bs, T, _ = x_sample.shape
    assert D == D_MODEL and T == CHUNK and S % R_PROMPT == 0 and cache_swa_k.shape[2] == WINDOW
    assert g_attn.shape[0] == 1, "single layer"
    bf = lambda w: w.astype(BF16)
    vec = lambda g: g.reshape(1, D_MODEL)
    w_in_b, wro, wso, wmo = bf(w_in[0]), bf(w_ret_out[0]), bf(w_swa_out[0]), bf(w_mix_out[0])
    wcq, wco, wmk, wmv = bf(w_cq[0]), bf(w_co[0]), bf(w_mk[0]), bf(w_mv[0])
    wg, wu, wd = bf(w_gate[0]), bf(w_up[0]), bf(w_down[0])

    key_of_col, sub_of_col = _swa_col_maps()
    bucket2 = _rel_bucket()[:, key_of_col]
    bias2 = _rel_bias(rel_bias, bucket2, jnp.asarray(sub_of_col)[None, :])
    sinks = swa_sinks[0]

    def layer(x, pos, R, G, st0, hist, mk, mv, has_history):
        nb, seq, _ = x.shape
        x2d = x.reshape(nb * seq, D_MODEL)
        tm = G * R
        cos_t, sin_t = _rope_tables(pos)
        mt, qd, kd, gr = _retention_tables(R)
        h1, st, kv = _front(x2d, vec(g_attn[0]), w_in_b, _tile_rows(cos_t, tm), _tile_rows(sin_t, tm),
                            _tile_rows(kd, tm), st0, hist, mt, qd, gr, bias2, sinks, wro, wso, wmo,
                            G=G, R=R, n_tiles=seq // R, has_history=has_history)
        r_tail = TAIL_ROWS if (G == 1 and seq % TAIL_ROWS == 0) else R
        y = _tail(h1, mk, mv, vec(g_cross[0]), vec(g_ffn[0]), vec(g_final), wcq, wco, wg, wu, wd,
                  G=G, R=r_tail, tiles_per_mem=seq // r_tail)
        return y.reshape(nb, seq, D_MODEL), st, kv

    mk_f, mv_f, mk_b, mv_b = _mem_kv(mem_prompt.reshape(B * N_MEM, D_MODEL), vec(g_mem[0]), wmk, wmv, 512)
    y_p, st_p, kv_p = layer(
        x_prompt, np.arange(S), R_PROMPT, 1, None, None,
        mk_b.reshape(B, N_MEM, D_MODEL), mv_b.reshape(B, N_MEM, D_MODEL), False)

    Gs = 2 if Bs % 2 == 0 else 1
    hist_s = jnp.concatenate([cache_swa_k[0].reshape(Bs, WINDOW, LANES), cache_swa_v[0].reshape(Bs, WINDOW, LANES)],
                             axis=-1)
    y_s, st_s, kv_s = layer(
        x_sample, PAST_LEN + np.arange(T), CHUNK, Gs,
        cache_ret_state[0].astype(F32), hist_s,
        bf(cache_mem_k[0].reshape(Bs, N_MEM, D_MODEL)), bf(cache_mem_v[0].reshape(Bs, N_MEM, D_MODEL)), True)

    kvshape = (1, B, WINDOW, SWA_KV, SWA_HD)
    k_p = kv_p[:, :, :LANES].reshape(kvshape)
    v_p = kv_p[:, :, LANES:].reshape(kvshape)
    k_s = jnp.concatenate([cache_swa_k[0][:, T:], kv_s[:, :, :LANES].reshape(Bs, T, SWA_KV, SWA_HD)], axis=1)[None]
    v_s = jnp.concatenate([cache_swa_v[0][:, T:], kv_s[:, :, LANES:].reshape(Bs, T, SWA_KV, SWA_HD)], axis=1)[None]
    mem_shape = (1, B, N_MEM, MEM_HEADS, MEM_HD)
    return (y_p, y_s, st_p[None], st_s[None], k_p, k_s, v_p, v_s, mk_f.reshape(mem_shape), mv_f.reshape(mem_shape))
```

```python
import functools
import math

import numpy as np
import jax
import jax.numpy as jnp
from jax import lax
from jax.experimental import pallas as pl
from jax.experimental.pallas import tpu as pltpu

F32 = jnp.float32
BF16 = jnp.bfloat16

D_MODEL = 1024
CHUNK = 64
EPS = 1e-6
NEG_INF = -1e30
PAST_LEN = 4096

RET_HEADS = 4
RET_DK = 256
RET_DV = 256
ROPE_BASE = 10000.0

SWA_HEADS = 16
SWA_KV = 2
SWA_GROUP = 8
SWA_HD = 64
WINDOW = 128
SWA_L = WINDOW + CHUNK

REL_BUCKETS = 32
REL_MAX_DIST = 128

N_MEM = 256
MEM_HEADS = 4
MEM_HD = 256
D_FF = 2816

OFF_RQ, OFF_RK, OFF_RV, OFF_RG, OFF_SQ, OFF_SKV, OFF_GA, OFF_GB = 0, 1024, 2048, 3072, 4096, 5120, 5376, 6400
D_IN = 7424

LANES = 128
HALF = LANES // 2
VMEM_LIMIT = 56 * 1024 * 1024

FF_COLS = 256
MERGE_COLS = 256
GATE_COLS = 512
KV_COLS = 2 * SWA_KV * SWA_HD
PIPE_DEPTH = 2
TAIL_ROWS = 512
R_PROMPT = 256


def _resident(shape):
    nd = len(shape)
    return pl.BlockSpec(shape, lambda *_: (0,) * nd, pipeline_mode=pl.Buffered(1))


def _rms(x):
    return x * lax.rsqrt(jnp.mean(x * x, axis=-1, keepdims=True) + EPS)


def _interleave(primary, secondary):
    out, j = [], 0
    for k, piece in enumerate(primary):
        out.append(piece)
        while j < len(secondary) and (j + 1) * len(primary) <= (k + 1) * len(secondary):
            out.append(secondary[j])
            j += 1
    return out + secondary[j:]


def _swa_col_maps():
    key = np.concatenate([np.arange(LANES), np.arange(LANES), LANES + np.arange(HALF), LANES + np.arange(HALF)])
    sub = np.concatenate([np.zeros(LANES), np.ones(LANES), np.zeros(HALF), np.ones(HALF)]).astype(np.int32)
    return key.astype(np.int32), sub


def _rel_bias_kernel(tab_ref, bucket_ref, sub_ref, out_ref):
    bucket = bucket_ref[...]
    is_b = sub_ref[...] > 0
    for hk in range(SWA_KV):
        for gp in range(SWA_GROUP // 2):
            h0 = hk * SWA_GROUP + gp * 2
            acc = jnp.zeros(bucket.shape, F32)
            for b in range(REL_BUCKETS):
                val = jnp.where(is_b, tab_ref[b, h0 + 1], tab_ref[b, h0])
                acc = jnp.where(bucket == b, val, acc)
            out_ref[hk, gp * CHUNK:(gp + 1) * CHUNK, :] = acc


def _rel_bias(table, bucket2, sub2):
    ncol = bucket2.shape[1]
    return pl.pallas_call(
        _rel_bias_kernel,
        in_specs=[pl.BlockSpec(memory_space=pltpu.SMEM),
                  pl.BlockSpec(memory_space=pltpu.VMEM),
                  pl.BlockSpec(memory_space=pltpu.VMEM)],
        out_specs=pl.BlockSpec(memory_space=pltpu.VMEM),
        out_shape=jax.ShapeDtypeStruct((SWA_KV, 4 * CHUNK, ncol), F32),
        name="rel_bias",
    )(table, bucket2, sub2)


def _front_kernel(x_ref, xp_ref, g_ref, w_ref, cos_ref, sin_ref, kdec_ref,
                  st0_ref, hist_ref, mt_ref, qd_ref, gr_ref, bias_ref, sink_ref,
                  wro_ref, wso_ref, wmo_ref,
                  h_ref, st_ref, kvl_ref,
                  xn, rq, rk, rkd, rv, rgs, sq, gas, gbs, state, ka, kb, va, vb, ain, swo, mixs,
                  *, G, R, n_tiles, has_history):
    t = pl.program_id(0)
    n_real = pl.num_programs(0) - 1
    i = lax.rem(jnp.minimum(t, n_real - 1), n_tiles)
    lane =lax.broadcasted_iota(jnp.int32, (1, LANES), 1)
    lo = lane < HALF

    def split_heads(x):
        r = pltpu.roll(x, HALF, axis=1)
        z = jnp.zeros_like(x)
        a0 = jnp.where(lo, x, z).astype(BF16)
        b0 = jnp.where(lo, z, r).astype(BF16)
        a1 = jnp.where(lo, r, z).astype(BF16)
        b1 = jnp.where(lo, z, x).astype(BF16)
        return (a0, a1), (b0, b1)

    @pl.when(t == 0)
    def _():
        ain[...] = jnp.zeros(ain.shape, BF16)
        swo[...] = jnp.zeros(swo.shape, BF16)
        gas[...] = jnp.zeros(gas.shape, BF16)
        gbs[...] = jnp.zeros(gbs.shape, BF16)

    @pl.when(i == 0)
    def _():
        state[...] = st0_ref[...] if has_history else jnp.zeros(state.shape, F32)
        for g in range(G):
            hist = hist_ref[g] if has_history else jnp.zeros((WINDOW, KV_COLS), F32)
            tail = slice(R, R + WINDOW)
            (a0, a1), (b0, b1) = split_heads(hist[:, :LANES])
            ka[g, 0, tail], ka[g, 1, tail], kb[g, 0, tail], kb[g, 1, tail] = a0, a1, b0, b1
            (a0, a1), (b0, b1) = split_heads(hist[:, LANES:])
            va[g, 0, tail], va[g, 1, tail], vb[g, 0, tail], vb[g, 1, tail] = a0, a1, b0, b1

    def norm():
        xn[...] = (_rms(x_ref[...]) * g_ref[...]).astype(BF16)

    def proj(c0, n):
        return jnp.dot(xn[...], w_ref[:, c0:c0 + n], preferred_element_type=F32)

    def proj_head(h):
        c = h * RET_DK
        tile_rows = pl.ds(pl.multiple_of(i * (G * R), G * R), G * R)
        cos = cos_ref[tile_rows, :]
        sin = sin_ref[tile_rows, :]
        acc = proj(OFF_RQ + c, RET_DK)
        x1, x2 = acc[:, :LANES], acc[:, LANES:]
        rq[:, c:c + LANES] = (x1 * cos - x2 * sin).astype(BF16)
        rq[:, c + LANES:c + RET_DK] = (x2 * cos + x1 * sin).astype(BF16)
        acc = proj(OFF_RK + c, RET_DK)
        x1, x2 = acc[:, :LANES], acc[:, LANES:]
        k1 = (x1 * cos - x2 * sin) * (RET_DK ** -0.5)
        k2 = (x2 * cos + x1 * sin) * (RET_DK ** -0.5)
        rk[:, c:c + LANES] = k1.astype(BF16)
        rk[:, c + LANES:c + RET_DK] = k2.astype(BF16)
        kd = kdec_ref[:, c:c + LANES]
        rkd[:, c:c + LANES] = (k1 * kd).astype(BF16)
        rkd[:, c + LANES:c + RET_DK] = (k2 * kd).astype(BF16)
        rv[:, c:c + RET_DV] = proj(OFF_RV + c, RET_DV).astype(BF16)
        rgs[:, c:c + RET_DV] = jax.nn.silu(proj(OFF_RG + c, RET_DV)).astype(BF16)

    def proj_sq(j):
        c = j * GATE_COLS
        sq[:, c:c + GATE_COLS] = proj(OFF_SQ + c, GATE_COLS).astype(BF16)

    def proj_kv():
        kvt_all = proj(OFF_SKV, KV_COLS)
        for g in range(G):
            kvl_ref[g] = kvt_all[(g + 1) * R - min(R, WINDOW):(g + 1) * R]
        for g in range(G):
            kvt = kvt_all[g * R:(g + 1) * R]
            for buf in (ka, kb, va, vb):
                for hk in range(SWA_KV):
                    buf[g, hk, :WINDOW] = buf[g, hk, R:R + WINDOW]
            (a0, a1), (b0, b1) = split_heads(kvt[:, :LANES])
            ka[g, 0, WINDOW:], ka[g, 1, WINDOW:], kb[g, 0, WINDOW:], kb[g, 1, WINDOW:] = a0, a1, b0, b1
            (a0, a1), (b0, b1) = split_heads(kvt[:, LANES:])
            va[g, 0, WINDOW:], va[g, 1, WINDOW:], vb[g, 0, WINDOW:], vb[g, 1, WINDOW:] = a0, a1, b0, b1

    def proj_gate(dst, off, j):
        c = j * GATE_COLS
        dst[:, c:c + GATE_COLS] = jax.nn.sigmoid(proj(off + c, GATE_COLS)).astype(BF16)

    def ret_front(g, h):
        rows = slice(g * R, (g + 1) * R)
        cols = slice(h * RET_DK, (h + 1) * RET_DK)
        q = rq[rows, cols]
        v = rv[rows, cols]
        s_prev = state[g, h]
        sc = lax.dot_general(q, rk[rows, cols], (((1,), (1,)), ((), ())), preferred_element_type=F32)
        qs = jnp.dot(q, s_prev.astype(BF16), preferred_element_type=F32)
        kv_new = lax.dot_general(rkd[rows, cols], v, (((0,), (0,)), ((), ())), preferred_element_type=F32)
        return sc, qs, kv_new, s_prev, v

    def ret_back(g, h, vals):
        sc, qs, kv_new, s_prev, v = vals
        rows = slice(g * R, (g + 1) * R)
        cols = slice(h * RET_DK, (h + 1) * RET_DK)
        o = jnp.dot((sc * mt_ref[h]).astype(BF16), v, preferred_element_type=F32) + qs * qd_ref[h]
        state[g, h] = s_prev * gr_ref[h] + kv_new
        ain[rows, cols] =(_rms(o) * rgs[rows, cols].astype(F32)).astype(BF16)

    ones_a = jnp.broadcast_to(jnp.where(lo, 1.0, 0.0).astype(BF16), (LANES, LANES))
    ones_b = jnp.broadcast_to(jnp.where(lo, 0.0, 1.0).astype(BF16), (LANES, LANES))
    lane3 = lax.broadcasted_iota(jnp.int32, (1, 3 * LANES), 1)
    key_of_col = jnp.where(lane3 < 2 * LANES, lane3 % LANES, LANES + lane3 % HALF)

    def swa_front(g, c, hk):
        w0 = c * CHUNK
        r0 = g * R + c * CHUNK
        q2 = jnp.concatenate(
            [sq[r0:r0 + CHUNK, hk * 512 + gp * LANES: hk * 512 + (gp + 1) * LANES] for gp in range(4)],
            axis=0)
        kbd = jnp.concatenate([ka[g, hk, w0:w0 + LANES], kb[g, hk, w0:w0 + LANES],
                               ka[g, hk, w0 + LANES:w0 + SWA_L], kb[g, hk, w0 + LANES:w0 + SWA_L]],
                              axis=0)
        return lax.dot_general(q2, kbd, (((1,), (1,)), ((), ())), preferred_element_type=F32)

    def swa_back(g, c, hk, s):
        w0 = c * CHUNK
        r0 = g * R + c * CHUNK
        s = s * (SWA_HD ** -0.5) + bias_ref[hk]
        if not has_history and c < WINDOW // CHUNK:
            kpos = key_of_col + (i * R + c * CHUNK - WINDOW)
            s = jnp.where(kpos >= 0, s, NEG_INF)
        t0, t1, t2 = s[:, :LANES], s[:, LANES:2 * LANES], s[:, 2 * LANES:]
        m_a = jnp.max(jnp.maximum(t0, jnp.where(lo, t2, NEG_INF)), axis=-1, keepdims=True)
        m_b = jnp.max(jnp.maximum(t1, jnp.where(lo, NEG_INF, t2)), axis=-1, keepdims=True)
        sk_a = jnp.concatenate([jnp.full((CHUNK, 1), sink_ref[hk * SWA_GROUP + 2 * gp], F32) for gp in range(4)],
                               axis=0)
        sk_b = jnp.concatenate([jnp.full((CHUNK, 1), sink_ref[hk * SWA_GROUP + 2 * gp + 1], F32) for gp in range(4)],
                               axis=0)
        m_a = jnp.maximum(m_a, sk_a)
        m_b = jnp.maximum(m_b, sk_b)
        e = jnp.concatenate([jnp.exp(t0 - m_a), jnp.exp(t1 - m_b),
                             jnp.exp(t2 - jnp.where(lo, m_a, m_b))], axis=1).astype(BF16)
        vbd = jnp.concatenate([
            jnp.concatenate([va[g, hk, w0:w0 + LANES], ones_a], axis=1),
            jnp.concatenate([vb[g, hk, w0:w0 + LANES], ones_b], axis=1),
            jnp.concatenate([va[g, hk, w0 + LANES:w0 + SWA_L], ones_a[:HALF]], axis=1),
            jnp.concatenate([vb[g, hk, w0 + LANES:w0 + SWA_L], ones_b[:HALF]], axis=1)], axis=0)
        oa = jnp.dot(e, vbd, preferred_element_type=F32)
        den = oa[:, LANES:] + jnp.where(lo, jnp.exp(sk_a - m_a), jnp.exp(sk_b - m_b))
        o2 = (oa[:, :LANES] / den).astype(BF16)
        for gp in range(4):
            swo[r0:r0 + CHUNK, hk * 512 + gp * LANES: hk * 512 + (gp + 1) * LANES] = o2[gp * CHUNK:(gp + 1) * CHUNK]

    def merge_ab(j):
        cols = slice(j * MERGE_COLS, (j + 1) * MERGE_COLS)
        a = jnp.dot(ain[...], wro_ref[:, cols], preferred_element_type=F32)
        b = jnp.dot(swo[...], wso_ref[:, cols], preferred_element_type=F32)
        mixs[:, cols] = (gas[:, cols].astype(F32) * a + gbs[:, cols].astype(F32) * b).astype(BF16)

    def merge_y(j):
        cols = slice(j * MERGE_COLS, (j + 1) * MERGE_COLS)
        h_ref[:, cols] = xp_ref[:, cols] + jnp.dot(mixs[...], wmo_ref[:, cols], preferred_element_type=F32)

    n_mc = D_MODEL // MERGE_COLS
    n_gc = D_MODEL // GATE_COLS
    dense = [(functools.partial(merge_ab, 0), None), (norm, None), (functools.partial(merge_ab, 1), None),
             (functools.partial(proj_head, 0), ("head", 0))]
    dense += [(functools.partial(merge_ab, j), ("merged", 0) if j == n_mc - 1 else None) for j in range(2, n_mc)]
    dense += [(functools.partial(proj_head, h), ("head", h)) for h in range(1, RET_HEADS)]
    dense += [(functools.partial(proj_sq, j), None) for j in range(n_gc)]
    dense.append((proj_kv, ("swa", 0)))
    dense += [(functools.partial(proj_gate, gas, OFF_GA, j), None) for j in range(n_gc)]
    dense += [(functools.partial(proj_gate, gbs, OFF_GB, j), None) for j in range(n_gc)]
    dense += [(functools.partial(merge_y, j), None) for j in range(n_mc)]

    items =[(ret_front, ret_back, (g, h), ("head", h)) for h in range(RET_HEADS) for g in range(G)]
    items += [(swa_front, swa_back, (g, c, hk), ("swa", 0))
              for g in range(G) for c in range(R // CHUNK) for hk in range(SWA_KV)]
    tasks = []
    for n in range(len(items) + PIPE_DEPTH):
        if n < len(items):
            tasks.append(("front", n))
        if n >= PIPE_DEPTH:
            tasks.append(("back", n - PIPE_DEPTH))

    pending, done, nxt = {}, set(), 0

    def run_task(kind, n):
        front, back, args, _ = items[n]
        if kind == "front":
            pending[n] = front(*args)
        else:
            back(*args, pending.pop(n))

    for k, (piece, tag) in enumerate(dense):
        piece()
        if tag is not None:
            done.add(tag)
        quota = -(-(len(tasks) - nxt) // (len(dense) - k))
        while nxt < len(tasks) and quota > 0:
            kind, n = tasks[nxt]
            if kind == "front" and items[n][3] not in done:
                break
            if kind == "back" and ("merged", 0) not in done:
                break
            run_task(kind, n)
            nxt += 1
            quota -= 1
    assert nxt == len(tasks)

    @pl.when((i == n_tiles - 1) & (t < n_real))
    def _():
        st_ref[...] = state[...]


def _front(x2d, g, w_in, cos_t, sin_t, kdec_t, st0, hist, mt, qd, gr, bias2, sinks, wro, wso, wmo,
           *, G, R, n_tiles, has_history):
    n = x2d.shape[0]
    tm = G * R
    n_steps = n // tm
    n_rows = n // (R * n_tiles)
    keep = min(R, WINDOW)
    assert n_steps * tm == n and n_steps % n_tiles == 0 and (G == 1 or n_tiles == 1)
    assert cos_t.shape == (n_tiles * tm, LANES) and kdec_t.shape == (tm, D_MODEL)
    cur = lambda t: jnp.minimum(t, n_steps - 1)
    prev = lambda t: jnp.maximum(t - 1, 0)
    per_row = pl.BlockSpec((G, RET_HEADS, RET_DK, RET_DV), lambda t: (cur(t) // n_tiles, 0, 0, 0))
    kw = dict(G=G, R=R, n_tiles=n_tiles, has_history=has_history)
    if has_history:
        kern = functools.partial(_front_kernel, **kw)
        carried = [st0, hist]
        carried_specs = [per_row, pl.BlockSpec((G, WINDOW, KV_COLS), lambda t: (cur(t) // n_tiles, 0, 0))]
    else:
        def kern(x_ref, xp_ref, g_ref, w_ref, cos_ref, sin_ref, kdec_ref, *rest):
            _front_kernel(x_ref, xp_ref, g_ref, w_ref, cos_ref, sin_ref, kdec_ref, None, None, *rest, **kw)
        carried, carried_specs = [], []
    tile_bf = pltpu.VMEM((tm, D_MODEL), BF16)
    return pl.pallas_call(
        kern,
        grid=(n_steps + 1,),
        in_specs=[pl.BlockSpec((tm, D_MODEL), lambda t: (cur(t), 0)),
                  pl.BlockSpec((tm, D_MODEL), lambda t: (prev(t), 0)),
                  _resident((1, D_MODEL)), _resident(w_in.shape), _resident(cos_t.shape), _resident(sin_t.shape),
                  _resident(kdec_t.shape)]
                 + carried_specs
                 + [_resident(mt.shape), _resident(qd.shape),
                  pl.BlockSpec(memory_space=pltpu.SMEM),
                  _resident(bias2.shape),
                  pl.BlockSpec(memory_space=pltpu.SMEM),
                  _resident(wro.shape), _resident(wso.shape), _resident(wmo.shape)],
        out_specs=[pl.BlockSpec((tm, D_MODEL), lambda t: (prev(t), 0)), per_row,
                   pl.BlockSpec((G, keep, KV_COLS), lambda t: (cur(t) // n_tiles, 0, 0))],
        out_shape=[jax.ShapeDtypeStruct((n, D_MODEL), F32),
                   jax.ShapeDtypeStruct((n_rows, RET_HEADS, RET_DK, RET_DV), F32),
                   jax.ShapeDtypeStruct((n_rows, keep, KV_COLS), F32)],
        scratch_shapes=[tile_bf] * 9
                       + [pltpu.VMEM((G, RET_HEADS, RET_DK, RET_DV), F32)]
                       + [pltpu.VMEM((G, SWA_KV, WINDOW + R, LANES), BF16)] * 4
                       + [tile_bf] * 3,
        compiler_params=pltpu.CompilerParams(dimension_semantics=("arbitrary",), vmem_limit_bytes=VMEM_LIMIT),
        name="front",
    )(x2d, x2d, g, w_in, cos_t, sin_t, kdec_t, *carried, mt, qd, gr, bias2, sinks, wro, wso, wmo)


def _tail_kernel(h_ref, mk_ref, mv_ref, gc_ref, gf_ref, gl_ref, wcq_ref, wco_ref, wg_ref, wu_ref, wd_ref,
                 y_ref, qs, att, h2s, hn2s, h2_prev, hn2_prev, acts, h3s, *, G, R):
    @pl.when(pl.program_id(0) == 0)
    def _():
        h2s[...] = jnp.zeros(h2s.shape, F32)
        hn2s[...] = jnp.zeros(hn2s.shape, BF16)

    h2_prev[...] = h2s[...]
    hn2_prev[...] = hn2s[...]

    def q_proj():
        hn = (_rms(h_ref[...]) * gc_ref[...]).astype(BF16)
        qs[...] = jnp.dot(hn, wcq_ref[...], preferred_element_type=F32).astype(BF16)

    def head_front(g, hd):
        rows = slice(g * R, (g + 1) * R)
        cols = slice(hd * MEM_HD, (hd + 1) * MEM_HD)
        return lax.dot_general(qs[rows, cols], mk_ref[g, :, cols], (((1,), (1,)), ((), ())),
                               preferred_element_type=F32)

    def head_back(g, hd, s):
        rows = slice(g * R, (g + 1) * R)
        cols = slice(hd * MEM_HD, (hd + 1) * MEM_HD)
        s = s * (MEM_HD ** -0.5)
        e = jnp.exp(s - jnp.max(s, axis=-1, keepdims=True))
        den = jnp.sum(e, axis=-1, keepdims=True)
        o = jnp.dot(e.astype(BF16), mv_ref[g, :, cols], preferred_element_type=F32) / den
        att[rows, cols] = o.astype(BF16)

    def attn_out():
        h2 = h_ref[...] + jnp.dot(att[...], wco_ref[...], preferred_element_type=F32)
        h2s[...] = h2
        hn2s[...] = (_rms(h2) * gf_ref[...]).astype(BF16)

    def ff_chunk(j):
        cols = slice(j * FF_COLS, (j + 1) * FF_COLS)
        hn = hn2_prev[...]
        gate = jnp.dot(hn, wg_ref[:, cols], preferred_element_type=F32)
        up = jnp.dot(hn, wu_ref[:, cols], preferred_element_type=F32)
        acts[:, cols] = (jax.nn.silu(gate) * up).astype(BF16)

    sq_sums = []

    def down_chunk(j):
        cols = slice(j * FF_COLS, (j + 1) * FF_COLS)
        h3 = h2_prev[:, cols] + jnp.dot(acts[...], wd_ref[:, cols], preferred_element_type=F32)
        h3s[:, cols] = h3
        sq_sums.append(jnp.sum(h3 * h3, axis=-1, keepdims=True))

    def final_norm():
        mean_sq = functools.reduce(lambda a, b: a + b, sq_sums) * (1.0 / D_MODEL)
        y_ref[...] = h3s[...] * lax.rsqrt(mean_sq + EPS) * gl_ref[...]

    heads = [(g, hd) for g in range(G) for hd in range(MEM_HEADS)]
    pending = {}

    def front(n):
        pending[n] = head_front(*heads[n])

    def back(n):
        head_back(*heads[n], pending.pop(n))

    attn = [q_proj]
    for n in range(len(heads) + 1):
        if n < len(heads):
            attn.append(functools.partial(front, n))
        if n >= 1:
            attn.append(functools.partial(back, n - 1))
    attn.append(attn_out)
    ffn = [functools.partial(ff_chunk, j) for j in range(D_FF // FF_COLS)]
    ffn += [functools.partial(down_chunk, j) for j in range(D_MODEL // FF_COLS)]
    ffn.append(final_norm)
    for piece in _interleave(ffn, attn):
        piece()


def _tail(h2d, mk, mv, gc, gf, gl, wcq, wco, wg, wu, wd, *, G, R, tiles_per_mem):
    n = h2d.shape[0]
    tm = G * R
    n_steps = n // tm
    assert n_steps * tm == n and (G == 1 or tiles_per_mem == 1)
    a_tile = lambda t: jnp.minimum(t, n_steps - 1)
    mem = pl.BlockSpec((G, N_MEM, D_MODEL), lambda t: (a_tile(t) // tiles_per_mem, 0, 0))
    vec = _resident((1, D_MODEL))
    kern = functools.partial(_tail_kernel, G=G, R=R)
    return pl.pallas_call(
        kern,
        grid=(n_steps + 1,),
        in_specs=[pl.BlockSpec((tm, D_MODEL), lambda t: (a_tile(t), 0)), mem, mem, vec, vec, vec,
                  _resident(wcq.shape), _resident(wco.shape), _resident(wg.shape), _resident(wu.shape),
                  _resident(wd.shape)],
        out_specs=pl.BlockSpec((tm, D_MODEL), lambda t: (jnp.maximum(t - 1, 0), 0)),
        out_shape=jax.ShapeDtypeStruct((n, D_MODEL), F32),
        scratch_shapes=[pltpu.VMEM((tm, D_MODEL), BF16), pltpu.VMEM((tm, D_MODEL), BF16),
                        pltpu.VMEM((tm, D_MODEL), F32), pltpu.VMEM((tm, D_MODEL), BF16),
                        pltpu.VMEM((tm, D_MODEL), F32), pltpu.VMEM((tm, D_MODEL), BF16),
                        pltpu.VMEM((tm, D_FF), BF16), pltpu.VMEM((tm, D_MODEL), F32)],
        compiler_params=pltpu.CompilerParams(dimension_semantics=("arbitrary",), vmem_limit_bytes=VMEM_LIMIT),
        name="tail",
    )(h2d, mk, mv, gc, gf, gl, wcq, wco, wg, wu, wd)


def _mem_kv_kernel(m_ref, g_ref, wk_ref, wv_ref, k_ref, v_ref, kb_ref, vb_ref):
    mn = (_rms(m_ref[...]) * g_ref[...]).astype(BF16)
    k = jnp.dot(mn, wk_ref[...], preferred_element_type=F32)
    v = jnp.dot(mn, wv_ref[...], preferred_element_type=F32)
    for hd in range(MEM_HEADS):
        k_ref[:, hd, :] = k[:, hd * MEM_HD:(hd + 1) * MEM_HD]
        v_ref[:, hd, :] = v[:, hd * MEM_HD:(hd + 1) * MEM_HD]
    kb_ref[...] = k.astype(BF16)
    vb_ref[...] = v.astype(BF16)


def _mem_kv(mem2d, g, wk, wv, tm):
    n = mem2d.shape[0]
    assert n % tm == 0
    blk = pl.BlockSpec((tm, D_MODEL), lambda i: (i, 0))
    blk4 = pl.BlockSpec((tm, MEM_HEADS, MEM_HD), lambda i: (i, 0, 0))
    return pl.pallas_call(
        _mem_kv_kernel,
        grid=(n // tm,),
        in_specs=[blk, _resident((1, D_MODEL)), _resident(wk.shape), _resident(wv.shape)],
        out_specs=[blk4, blk4, blk, blk],
        out_shape=[jax.ShapeDtypeStruct((n, MEM_HEADS, MEM_HD), F32)] * 2
                  + [jax.ShapeDtypeStruct((n, D_MODEL), BF16)] * 2,
        compiler_params=pltpu.CompilerParams(dimension_semantics=("parallel",), vmem_limit_bytes=VMEM_LIMIT),
        name="mem_kv",
    )(mem2d, g, wk, wv)


def _log_gamma():
    return np.log(np.float32(1.0) - np.float32(2.0) ** (np.float32(-5.0) - np.arange(RET_HEADS, dtype=np.float32)))


def _rope_tables(pos):
    half = RET_DK // 2
    inv = np.float32(ROPE_BASE) ** (-np.arange(half, dtype=np.float32) / np.float32(half))
    ang = pos.astype(np.float32)[:, None] * inv[None, :]
    return np.cos(ang).astype(np.float32), np.sin(ang).astype(np.float32)


def _retention_tables(R):
    lg = _log_gamma()[:, None, None]
    idx = np.arange(R, dtype=np.float32)
    diff = idx[:, None] - idx[None, :]
    cn = (np.arange(R) // CHUNK)[:, None]
    cm = (np.arange(R) // CHUNK)[None, :]
    mask = np.where(cm == cn, np.exp(lg * np.abs(diff)), np.where(cm < cn, np.exp(lg * diff), np.float32(0.0)))
    qd = np.exp(_log_gamma()[:, None] * (idx[None, :] + np.float32(1.0)))
    kd = np.exp(_log_gamma()[:, None] * (np.float32(R - 1.0) - idx[None, :]))
    gr = np.exp(_log_gamma() * np.float32(R))
    qd_full = np.broadcast_to(qd[:, :, None], (RET_HEADS, R, RET_DV))
    kd_full = np.broadcast_to(kd.T[:, :, None], (R, RET_HEADS, RET_DK)).reshape(R, RET_HEADS * RET_DK)
    f32 = lambda a: jnp.asarray(np.ascontiguousarray(a, dtype=np.float32))
    return f32(mask), f32(qd_full), kd_full.astype(np.float32), f32(gr)


def _rel_bucket():
    i = jnp.arange(CHUNK, dtype=jnp.int32)[:, None]
    j = jnp.arange(SWA_L, dtype=jnp.int32)[None, :]
    rel = (j - WINDOW) - i
    half = REL_BUCKETS // 2
    max_exact = half // 2
    n = jnp.abs(rel)
    large = max_exact + (jnp.log(jnp.maximum(n, 1).astype(F32) / max_exact)
                         / math.log(REL_MAX_DIST / max_exact) * (half - max_exact)).astype(jnp.int32)
    large = jnp.minimum(large, half - 1)
    return jnp.where(rel > 0, half, 0) + jnp.where(n < max_exact, n, large)


def _tile_rows(t, tm):
    reps = -(-tm // t.shape[0])
    return jnp.asarray(np.tile(t, (reps, 1)) if reps > 1 else t)


def kernel(x_prompt, x_sample, cache_ret_state, cache_swa_k, cache_swa_v, cache_mem_k, cache_mem_v, mem_prompt,
           rel_bias, g_attn, w_in, w_ret_out, w_swa_out, w_mix_out, swa_sinks, g_cross, g_mem, w_cq, w_mk, w_mv,
           w_co, g_ffn, w_gate, w_up, w_down, g_final):
    B, S, D = x_prompt.shape
    Bs, T, _ = x_sample.shape
    assert D == D_MODEL and T == CHUNK and S % R_PROMPT == 0 and cache_swa_k.shape[2] == WINDOW
    assert g_attn.shape[0] == 1, "single layer"
    bf = lambda w: w.astype(BF16)
    vec = lambda g: g.reshape(1, D_MODEL)
    w_in_b, wro, wso, wmo = bf(w_in[0]), bf(w_ret_out[0]), bf(w_swa_out[0]), bf(w_mix_out[0])
    wcq, wco, wmk, wmv = bf(w_cq[0]), bf(w_co[0]), bf(w_mk[0]), bf(w_mv[0])
    wg, wu, wd = bf(w_gate[0]), bf(w_up[0]), bf(w_down[0])

    key_of_col, sub_of_col = _swa_col_maps()
    bucket2 = _rel_bucket()[:, key_of_col]
    bias2 = _rel_bias(rel_bias, bucket2, jnp.asarray(sub_of_col)[None, :])
    sinks = swa_sinks[0]

    def layer(x, pos, R, G, st0, hist, mk, mv, has_history):
        nb, seq, _ = x.shape
        x2d = x.reshape(nb * seq, D_MODEL)
        tm = G * R
        cos_t, sin_t = _rope_tables(pos)
        mt, qd, kd, gr = _retention_tables(R)
        h1, st, kv = _front(x2d, vec(g_attn[0]), w_in_b, _tile_rows(cos_t, tm), _tile_rows(sin_t, tm),
                            _tile_rows(kd, tm), st0, hist, mt, qd, gr, bias2, sinks, wro, wso, wmo,
                            G=G, R=R, n_tiles=seq // R, has_history=has_history)
        r_tail = TAIL_ROWS if (G == 1 and seq % TAIL_ROWS == 0) else R
        y = _tail(h1, mk, mv, vec(g_cross[0]), vec(g_ffn[0]), vec(g_final), wcq, wco, wg, wu, wd,
                  G=G, R=r_tail, tiles_per_mem=seq // r_tail)
        return y.reshape(nb, seq, D_MODEL), st, kv

    mk_f, mv_f, mk_b, mv_b = _mem_kv(mem_prompt.reshape(B * N_MEM, D_MODEL), vec(g_mem[0]), wmk, wmv, 512)
    y_p, st_p, kv_p = layer(
        x_prompt, np.arange(S), R_PROMPT, 1, None, None,
        mk_b.reshape(B, N_MEM, D_MODEL), mv_b.reshape(B, N_MEM, D_MODEL), False)

    Gs = 2 if Bs % 2 == 0 else 1
    hist_s = jnp.concatenate([cache_swa_k[0].reshape(Bs, WINDOW, LANES), cache_swa_v[0].reshape(Bs, WINDOW, LANES)],
                             axis=-1)
    y_s, st_s, kv_s = layer(
        x_sample, PAST_LEN + np.arange(T), CHUNK, Gs,
        cache_ret_state[0].astype(F32), hist_s,
        bf(cache_mem_k[0].reshape(Bs, N_MEM, D_MODEL)), bf(cache_mem_v[0].reshape(Bs, N_MEM, D_MODEL)), True)

    kvshape = (1, B, WINDOW, SWA_KV, SWA_HD)
    k_p = kv_p[:, :, :LANES].reshape(kvshape)
    v_p = kv_p[:, :, LANES:].reshape(kvshape)
    k_s = jnp.concatenate([cache_swa_k[0][:, T:], kv_s[:, :, :LANES].reshape(Bs, T, SWA_KV, SWA_HD)], axis=1)[None]
    v_s = jnp.concatenate([cache_swa_v[0][:, T:], kv_s[:, :, LANES:].reshape(Bs, T, SWA_KV, SWA_HD)], axis=1)[None]
    mem_shape = (1, B, N_MEM, MEM_HEADS, MEM_HD)
    return (y_p, y_s, st_p[None], st_s[None], k_p, k_s, v_p, v_s, mk_f.reshape(mem_shape), mv_f.reshape(mem_shape))
```

```python
import functools
import math

import numpy as np
import jax
import jax.numpy as jnp
from jax import lax
from jax.experimental import pallas as pl
from jax.experimental.pallas import tpu as pltpu

F32 = jnp.float32
BF16 = jnp.bfloat16

D_MODEL = 1024
CHUNK = 64
EPS = 1e-6
NEG_INF = -1e30
PAST_LEN = 4096

RET_HEADS = 4
RET_DK = 256
RET_DV = 256
ROPE_BASE = 10000.0

SWA_HEADS = 16
SWA_KV = 2
SWA_GROUP = 8
SWA_HD = 64
WINDOW = 128
SWA_L = WINDOW + CHUNK

REL_BUCKETS = 32
REL_MAX_DIST = 128

N_MEM = 256
MEM_HEADS = 4
MEM_HD = 256
D_FF = 2816

OFF_RQ, OFF_RK, OFF_RV, OFF_RG, OFF_SQ, OFF_SKV, OFF_GA, OFF_GB = 0, 1024, 2048, 3072, 4096, 5120, 5376, 6400
D_IN = 7424

LANES = 128
HALF = LANES // 2
VMEM_LIMIT = 56 * 1024 * 1024

FF_COLS = 256
MERGE_COLS = 256
GATE_COLS = 512
KV_COLS = 2 * SWA_KV * SWA_HD
PIPE_DEPTH = 3
TAIL_ROWS = 512
R_PROMPT = 256


def _resident(shape):
    nd = len(shape)
    return pl.BlockSpec(shape, lambda *_: (0,) * nd, pipeline_mode=pl.Buffered(1))


def _rms(x):
    return x * lax.rsqrt(jnp.mean(x * x, axis=-1, keepdims=True) + EPS)


def _interleave(primary, secondary):
    out, j = [], 0
    for k, piece in enumerate(primary):
        out.append(piece)
        while j < len(secondary) and (j + 1) * len(primary) <= (k + 1) * len(secondary):
            out.append(secondary[j])
            j += 1
    return out + secondary[j:]


def _swa_col_maps():
    key = np.concatenate([np.arange(LANES), np.arange(LANES), LANES + np.arange(HALF), LANES + np.arange(HALF)])
    sub = np.concatenate([np.zeros(LANES), np.ones(LANES), np.zeros(HALF), np.ones(HALF)]).astype(np.int32)
    return key.astype(np.int32), sub


def _rel_bias_kernel(tab_ref, bucket_ref, sub_ref, out_ref):
    bucket = bucket_ref[...]
    is_b = sub_ref[...] > 0
    for hk in range(SWA_KV):
        for gp in range(SWA_GROUP // 2):
            h0 = hk * SWA_GROUP + gp * 2
            acc = jnp.zeros(bucket.shape, F32)
            for b in range(REL_BUCKETS):
                val = jnp.where(is_b, tab_ref[b, h0 + 1], tab_ref[b, h0])
                acc = jnp.where(bucket == b, val, acc)
            out_ref[hk, gp * CHUNK:(gp + 1) * CHUNK, :] = acc


def _rel_bias(table, bucket2, sub2):
    ncol = bucket2.shape[1]
    return pl.pallas_call(
        _rel_bias_kernel,
        in_specs=[pl.BlockSpec(memory_space=pltpu.SMEM),
                  pl.BlockSpec(memory_space=pltpu.VMEM),
                  pl.BlockSpec(memory_space=pltpu.VMEM)],
        out_specs=pl.BlockSpec(memory_space=pltpu.VMEM),
        out_shape=jax.ShapeDtypeStruct((SWA_KV, 4 * CHUNK, ncol), F32),
        name="rel_bias",
    )(table, bucket2, sub2)


def _front_kernel(x_ref, xp_ref, g_ref, w_ref, cos_ref, sin_ref, kdec_ref,
                  st0_ref, hist_ref, mt_ref, qd_ref, gr_ref, bias_ref, sink_ref,
                  wro_ref, wso_ref, wmo_ref,
                  h_ref, st_ref, kvl_ref,
                  xn, rq, rk, rkd, rv, rgs, sq, gas, gbs, state, ka, kb, va, vb, ain, swo, mixs,
                  *, G, R, n_tiles, has_history):
    t = pl.program_id(0)
    n_real = pl.num_programs(0) - 1
    i = lax.rem(jnp.minimum(t, n_real - 1), n_tiles)
    lane =lax.broadcasted_iota(jnp.int32, (1, LANES), 1)
    lo = lane < HALF

    def split_heads(x):
        r = pltpu.roll(x, HALF, axis=1)
        z = jnp.zeros_like(x)
        a0 = jnp.where(lo, x, z).astype(BF16)
        b0 = jnp.where(lo, z, r).astype(BF16)
        a1 = jnp.where(lo, r, z).astype(BF16)
        b1 = jnp.where(lo, z, x).astype(BF16)
        return (a0, a1), (b0, b1)

    @pl.when(t == 0)
    def _():
        ain[...] = jnp.zeros(ain.shape, BF16)
        swo[...] = jnp.zeros(swo.shape, BF16)
        gas[...] = jnp.zeros(gas.shape, BF16)
        gbs[...] = jnp.zeros(gbs.shape, BF16)

    @pl.when(i == 0)
    def _():
        state[...] = st0_ref[...] if has_history else jnp.zeros(state.shape, F32)
        for g in range(G):
            hist = hist_ref[g] if has_history else jnp.zeros((WINDOW, KV_COLS), F32)
            tail = slice(R, R + WINDOW)
            (a0, a1), (b0, b1) = split_heads(hist[:, :LANES])
            ka[g, 0, tail], ka[g, 1, tail], kb[g, 0, tail], kb[g, 1, tail] = a0, a1, b0, b1
            (a0, a1), (b0, b1) = split_heads(hist[:, LANES:])
            va[g, 0, tail], va[g, 1, tail], vb[g, 0, tail], vb[g, 1, tail] = a0, a1, b0, b1

    def norm():
        xn[...] = (_rms(x_ref[...]) * g_ref[...]).astype(BF16)

    def proj(c0, n):
        return jnp.dot(xn[...], w_ref[:, c0:c0 + n], preferred_element_type=F32)

    def proj_head(h):
        c = h * RET_DK
        tile_rows = pl.ds(pl.multiple_of(i * (G * R), G * R), G * R)
        cos = cos_ref[tile_rows, :]
        sin = sin_ref[tile_rows, :]
        acc = proj(OFF_RQ + c, RET_DK)
        x1, x2 = acc[:, :LANES], acc[:, LANES:]
        rq[:, c:c + LANES] = (x1 * cos - x2 * sin).astype(BF16)
        rq[:, c + LANES:c + RET_DK] = (x2 * cos + x1 * sin).astype(BF16)
        acc = proj(OFF_RK + c, RET_DK)
        x1, x2 = acc[:, :LANES], acc[:, LANES:]
        k1 = (x1 * cos - x2 * sin) * (RET_DK ** -0.5)
        k2 = (x2 * cos + x1 * sin) * (RET_DK ** -0.5)
        rk[:, c:c + LANES] = k1.astype(BF16)
        rk[:, c + LANES:c + RET_DK] = k2.astype(BF16)
        kd = kdec_ref[:, c:c + LANES]
        rkd[:, c:c + LANES] = (k1 * kd).astype(BF16)
        rkd[:, c + LANES:c + RET_DK] = (k2 * kd).astype(BF16)
        rv[:, c:c + RET_DV] = proj(OFF_RV + c, RET_DV).astype(BF16)
        rgs[:, c:c + RET_DV] = jax.nn.silu(proj(OFF_RG + c, RET_DV)).astype(BF16)

    def proj_sq(j):
        c = j * GATE_COLS
        sq[:, c:c + GATE_COLS] = proj(OFF_SQ + c, GATE_COLS).astype(BF16)

    def proj_kv():
        kvt_all = proj(OFF_SKV, KV_COLS)
        for g in range(G):
            kvl_ref[g] = kvt_all[(g + 1) * R - min(R, WINDOW):(g + 1) * R]
        for g in range(G):
            kvt = kvt_all[g * R:(g + 1) * R]
            for buf in (ka, kb, va, vb):
                for hk in range(SWA_KV):
                    buf[g, hk, :WINDOW] = buf[g, hk, R:R + WINDOW]
            (a0, a1), (b0, b1) = split_heads(kvt[:, :LANES])
            ka[g, 0, WINDOW:], ka[g, 1, WINDOW:], kb[g, 0, WINDOW:], kb[g, 1, WINDOW:] = a0, a1, b0, b1
            (a0, a1), (b0, b1) = split_heads(kvt[:, LANES:])
            va[g, 0, WINDOW:], va[g, 1, WINDOW:], vb[g, 0, WINDOW:], vb[g, 1, WINDOW:] = a0, a1, b0, b1

    def proj_gate(dst, off, j):
        c = j * GATE_COLS
        dst[:, c:c + GATE_COLS] = jax.nn.sigmoid(proj(off + c, GATE_COLS)).astype(BF16)

    def ret_front(g, h):
        rows = slice(g * R, (g + 1) * R)
        cols = slice(h * RET_DK, (h + 1) * RET_DK)
        q = rq[rows, cols]
        v = rv[rows, cols]
        s_prev = state[g, h]
        sc = lax.dot_general(q, rk[rows, cols], (((1,), (1,)), ((), ())), preferred_element_type=F32)
        qs = jnp.dot(q, s_prev.astype(BF16), preferred_element_type=F32)
        kv_new = lax.dot_general(rkd[rows, cols], v, (((0,), (0,)), ((), ())), preferred_element_type=F32)
        return sc, qs, kv_new, s_prev, v

    def ret_back(g, h, vals):
        sc, qs, kv_new, s_prev, v = vals
        rows = slice(g * R, (g + 1) * R)
        cols = slice(h * RET_DK, (h + 1) * RET_DK)
        o = jnp.dot((sc * mt_ref[h]).astype(BF16), v, preferred_element_type=F32) + qs * qd_ref[h]
        state[g, h] = s_prev * gr_ref[h] + kv_new
        ain[rows, cols] =(_rms(o) * rgs[rows, cols].astype(F32)).astype(BF16)

    ones_a = jnp.broadcast_to(jnp.where(lo, 1.0, 0.0).astype(BF16), (LANES, LANES))
    ones_b = jnp.broadcast_to(jnp.where(lo, 0.0, 1.0).astype(BF16), (LANES, LANES))
    lane3 = lax.broadcasted_iota(jnp.int32, (1, 3 * LANES), 1)
    key_of_col = jnp.where(lane3 < 2 * LANES, lane3 % LANES, LANES + lane3 % HALF)

    def swa_front(g, c, hk):
        w0 = c * CHUNK
        r0 = g * R + c * CHUNK
        q2 = jnp.concatenate(
            [sq[r0:r0 + CHUNK, hk * 512 + gp * LANES: hk * 512 + (gp + 1) * LANES] for gp in range(4)],
            axis=0)
        kbd = jnp.concatenate([ka[g, hk, w0:w0 + LANES], kb[g, hk, w0:w0 + LANES],
                               ka[g, hk, w0 + LANES:w0 + SWA_L], kb[g, hk, w0 + LANES:w0 + SWA_L]],
                              axis=0)
        return lax.dot_general(q2, kbd, (((1,), (1,)), ((), ())), preferred_element_type=F32)

    def swa_back(g, c, hk, s):
        w0 = c * CHUNK
        r0 = g * R + c * CHUNK
        s = s * (SWA_HD ** -0.5) + bias_ref[hk]
        if not has_history and c < WINDOW // CHUNK:
            kpos = key_of_col + (i * R + c * CHUNK - WINDOW)
            s = jnp.where(kpos >= 0, s, NEG_INF)
        t0, t1, t2 = s[:, :LANES], s[:, LANES:2 * LANES], s[:, 2 * LANES:]
        m_a = jnp.max(jnp.maximum(t0, jnp.where(lo, t2, NEG_INF)), axis=-1, keepdims=True)
        m_b = jnp.max(jnp.maximum(t1, jnp.where(lo, NEG_INF, t2)), axis=-1, keepdims=True)
        sk_a = jnp.concatenate([jnp.full((CHUNK, 1), sink_ref[hk * SWA_GROUP + 2 * gp], F32) for gp in range(4)],
                               axis=0)
        sk_b = jnp.concatenate([jnp.full((CHUNK, 1), sink_ref[hk * SWA_GROUP + 2 * gp + 1], F32) for gp in range(4)],
                               axis=0)
        m_a = jnp.maximum(m_a, sk_a)
        m_b = jnp.maximum(m_b, sk_b)
        e = jnp.concatenate([jnp.exp(t0 - m_a), jnp.exp(t1 - m_b),
                             jnp.exp(t2 - jnp.where(lo, m_a, m_b))], axis=1).astype(BF16)
        vbd = jnp.concatenate([
            jnp.concatenate([va[g, hk, w0:w0 + LANES], ones_a], axis=1),
            jnp.concatenate([vb[g, hk, w0:w0 + LANES], ones_b], axis=1),
            jnp.concatenate([va[g, hk, w0 + LANES:w0 + SWA_L], ones_a[:HALF]], axis=1),
            jnp.concatenate([vb[g, hk, w0 + LANES:w0 + SWA_L], ones_b[:HALF]], axis=1)], axis=0)
        oa = jnp.dot(e, vbd, preferred_element_type=F32)
        den = oa[:, LANES:] + jnp.where(lo, jnp.exp(sk_a - m_a), jnp.exp(sk_b - m_b))
        o2 = (oa[:, :LANES] / den).astype(BF16)
        for gp in range(4):
            swo[r0:r0 + CHUNK, hk * 512 + gp * LANES: hk * 512 + (gp + 1) * LANES] = o2[gp * CHUNK:(gp + 1) * CHUNK]

    def merge_ab(j):
        cols = slice(j * MERGE_COLS, (j + 1) * MERGE_COLS)
        a = jnp.dot(ain[...], wro_ref[:, cols], preferred_element_type=F32)
        b = jnp.dot(swo[...], wso_ref[:, cols], preferred_element_type=F32)
        mixs[:, cols] = (gas[:, cols].astype(F32) * a + gbs[:, cols].astype(F32) * b).astype(BF16)

    def merge_y(j):
        cols = slice(j * MERGE_COLS, (j + 1) * MERGE_COLS)
        h_ref[:, cols] = xp_ref[:, cols] + jnp.dot(mixs[...], wmo_ref[:, cols], preferred_element_type=F32)

    n_mc = D_MODEL // MERGE_COLS
    n_gc = D_MODEL // GATE_COLS
    dense = [(functools.partial(merge_ab, 0), None), (norm, None), (functools.partial(merge_ab, 1), None),
             (functools.partial(proj_head, 0), ("head", 0))]
    dense += [(functools.partial(merge_ab, j), ("merged", 0) if j == n_mc - 1 else None) for j in range(2, n_mc)]
    dense += [(functools.partial(proj_head, h), ("head", h)) for h in range(1, RET_HEADS)]
    dense += [(functools.partial(proj_sq, j), None) for j in range(n_gc)]
    dense.append((proj_kv, ("swa", 0)))
    dense += [(functools.partial(proj_gate, gas, OFF_GA, j), None) for j in range(n_gc)]
    dense += [(functools.partial(proj_gate, gbs, OFF_GB, j), None) for j in range(n_gc)]
    dense += [(functools.partial(merge_y, j), None) for j in range(n_mc)]

    items = [(ret_front, ret_back, (g, h), ("head", h)) for h in range(RET_HEADS) for g in range(G)]
    items += [(swa_front, swa_back, (g, c, hk), ("swa", 0))
              for g in range(G) for c in range(R // CHUNK) for hk in range(SWA_KV)]
    tasks = []
    for n in range(len(items) + PIPE_DEPTH):
        if n < len(items):
            tasks.append(("front", n))
        if n >= PIPE_DEPTH:
            tasks.append(("back", n - PIPE_DEPTH))

    pending, done, nxt = {}, set(), 0

    def run_task(kind, n):
        front, back, args, _ = items[n]
        if kind == "front":
            pending[n] = front(*args)
        else:
            back(*args, pending.pop(n))

    for k, (piece, tag) in enumerate(dense):
        piece()
        if tag is not None:
            done.add(tag)
        quota = -(-(len(tasks) - nxt) // (len(dense) - k))
        while nxt < len(tasks) and quota > 0:
            kind, n = tasks[nxt]
            if kind == "front" and items[n][3] not in done:
                break
            if kind == "back" and ("merged", 0) not in done:
                break
            run_task(kind, n)
            nxt += 1
            quota -= 1
    assert nxt == len(tasks)

    @pl.when((i == n_tiles - 1) & (t < n_real))
    def _():
        st_ref[...] = state[...]


def _front(x2d, g, w_in, cos_t, sin_t, kdec_t, st0, hist, mt, qd, gr, bias2, sinks, wro, wso, wmo,
           *, G, R, n_tiles, has_history):
    n = x2d.shape[0]
    tm = G * R
    n_steps = n // tm
    n_rows = n // (R * n_tiles)
    keep = min(R, WINDOW)
    assert n_steps * tm == n and n_steps % n_tiles == 0 and (G == 1 or n_tiles == 1)
    assert cos_t.shape == (n_tiles * tm, LANES) and kdec_t.shape == (tm, D_MODEL)
    cur = lambda t: jnp.minimum(t, n_steps - 1)
    prev = lambda t: jnp.maximum(t - 1, 0)
    per_row = pl.BlockSpec((G, RET_HEADS, RET_DK, RET_DV), lambda t: (cur(t) // n_tiles, 0, 0, 0))
    kw = dict(G=G, R=R, n_tiles=n_tiles, has_history=has_history)
    if has_history:
        kern = functools.partial(_front_kernel, **kw)
        carried = [st0, hist]
        carried_specs = [per_row, pl.BlockSpec((G, WINDOW, KV_COLS), lambda t: (cur(t) // n_tiles, 0, 0))]
    else:
        def kern(x_ref, xp_ref, g_ref, w_ref, cos_ref, sin_ref, kdec_ref, *rest):
            _front_kernel(x_ref, xp_ref, g_ref, w_ref, cos_ref, sin_ref, kdec_ref, None, None, *rest, **kw)
        carried, carried_specs = [], []
    tile_bf = pltpu.VMEM((tm, D_MODEL), BF16)
    return pl.pallas_call(
        kern,
        grid=(n_steps + 1,),
        in_specs=[pl.BlockSpec((tm, D_MODEL), lambda t: (cur(t), 0)),
                  pl.BlockSpec((tm, D_MODEL), lambda t: (prev(t), 0)),
                  _resident((1, D_MODEL)), _resident(w_in.shape), _resident(cos_t.shape), _resident(sin_t.shape),
                  _resident(kdec_t.shape)]
                 + carried_specs
                 + [_resident(mt.shape), _resident(qd.shape),
                  pl.BlockSpec(memory_space=pltpu.SMEM),
                  _resident(bias2.shape),
                  pl.BlockSpec(memory_space=pltpu.SMEM),
                  _resident(wro.shape), _resident(wso.shape), _resident(wmo.shape)],
        out_specs=[pl.BlockSpec((tm, D_MODEL), lambda t: (prev(t), 0)), per_row,
                   pl.BlockSpec((G, keep, KV_COLS), lambda t: (cur(t) // n_tiles, 0, 0))],
        out_shape=[jax.ShapeDtypeStruct((n, D_MODEL), F32),
                   jax.ShapeDtypeStruct((n_rows, RET_HEADS, RET_DK, RET_DV), F32),
                   jax.ShapeDtypeStruct((n_rows, keep, KV_COLS), F32)],
        scratch_shapes=[tile_bf] * 9
                       + [pltpu.VMEM((G, RET_HEADS, RET_DK, RET_DV), F32)]
                       + [pltpu.VMEM((G, SWA_KV, WINDOW + R, LANES), BF16)] * 4
                       + [tile_bf] * 3,
        compiler_params=pltpu.CompilerParams(dimension_semantics=("arbitrary",), vmem_limit_bytes=VMEM_LIMIT),
        name="front",
    )(x2d, x2d, g, w_in, cos_t, sin_t, kdec_t, *carried, mt, qd, gr, bias2, sinks, wro, wso, wmo)


def _tail_kernel(h_ref, mk_ref, mv_ref, gc_ref, gf_ref, gl_ref, wcq_ref, wco_ref, wg_ref, wu_ref, wd_ref,
                 y_ref, qs, att, h2s, hn2s, h2_prev, hn2_prev, acts, h3s, *, G, R):
    @pl.when(pl.program_id(0) == 0)
    def _():
        h2s[...] = jnp.zeros(h2s.shape, F32)
        hn2s[...] = jnp.zeros(hn2s.shape, BF16)

    h2_prev[...] = h2s[...]
    hn2_prev[...] = hn2s[...]

    def q_proj():
        hn = (_rms(h_ref[...]) * gc_ref[...]).astype(BF16)
        qs[...] = jnp.dot(hn, wcq_ref[...], preferred_element_type=F32).astype(BF16)

    def head_front(g, hd):
        rows = slice(g * R, (g + 1) * R)
        cols = slice(hd * MEM_HD, (hd + 1) * MEM_HD)
        return lax.dot_general(qs[rows, cols], mk_ref[g, :, cols], (((1,), (1,)), ((), ())),
                               preferred_element_type=F32)

    def head_back(g, hd, s):
        rows = slice(g * R, (g + 1) * R)
        cols = slice(hd * MEM_HD, (hd + 1) * MEM_HD)
        s = s * (MEM_HD ** -0.5)
        e = jnp.exp(s - jnp.max(s, axis=-1, keepdims=True))
        den = jnp.sum(e, axis=-1, keepdims=True)
        o = jnp.dot(e.astype(BF16), mv_ref[g, :, cols], preferred_element_type=F32) / den
        att[rows, cols] = o.astype(BF16)

    def attn_out():
        h2 = h_ref[...] + jnp.dot(att[...], wco_ref[...], preferred_element_type=F32)
        h2s[...] = h2
        hn2s[...] = (_rms(h2) * gf_ref[...]).astype(BF16)

    def ff_chunk(j):
        cols = slice(j * FF_COLS, (j + 1) * FF_COLS)
        hn = hn2_prev[...]
        gate = jnp.dot(hn, wg_ref[:, cols], preferred_element_type=F32)
        up = jnp.dot(hn, wu_ref[:, cols], preferred_element_type=F32)
        acts[:, cols] = (jax.nn.silu(gate) * up).astype(BF16)

    sq_sums = []

    def down_chunk(j):
        cols = slice(j * FF_COLS, (j + 1) * FF_COLS)
        h3 = h2_prev[:, cols] + jnp.dot(acts[...], wd_ref[:, cols], preferred_element_type=F32)
        h3s[:, cols] = h3
        sq_sums.append(jnp.sum(h3 * h3, axis=-1, keepdims=True))

    def final_norm():
        mean_sq = functools.reduce(lambda a, b: a + b, sq_sums) * (1.0 / D_MODEL)
        y_ref[...] = h3s[...] * lax.rsqrt(mean_sq + EPS) * gl_ref[...]

    heads = [(g, hd) for g in range(G) for hd in range(MEM_HEADS)]
    pending = {}

    def front(n):
        pending[n] = head_front(*heads[n])

    def back(n):
        head_back(*heads[n], pending.pop(n))

    attn = [q_proj]
    for n in range(len(heads) + 1):
        if n < len(heads):
            attn.append(functools.partial(front, n))
        if n >= 1:
            attn.append(functools.partial(back, n - 1))
    attn.append(attn_out)
    ffn = [functools.partial(ff_chunk, j) for j in range(D_FF // FF_COLS)]
    ffn += [functools.partial(down_chunk, j) for j in range(D_MODEL // FF_COLS)]
    ffn.append(final_norm)
    for piece in _interleave(ffn, attn):
        piece()


def _tail(h2d, mk, mv, gc, gf, gl, wcq, wco, wg, wu, wd, *, G, R, tiles_per_mem):
    n = h2d.shape[0]
    tm = G * R
    n_steps = n // tm
    assert n_steps * tm == n and (G == 1 or tiles_per_mem == 1)
    a_tile = lambda t: jnp.minimum(t, n_steps - 1)
    mem = pl.BlockSpec((G, N_MEM, D_MODEL), lambda t: (a_tile(t) // tiles_per_mem, 0, 0))
    vec = _resident((1, D_MODEL))
    kern = functools.partial(_tail_kernel, G=G, R=R)
    return pl.pallas_call(
        kern,
        grid=(n_steps + 1,),
        in_specs=[pl.BlockSpec((tm, D_MODEL), lambda t: (a_tile(t), 0)), mem, mem, vec, vec, vec,
                  _resident(wcq.shape), _resident(wco.shape), _resident(wg.shape), _resident(wu.shape),
                  _resident(wd.shape)],
        out_specs=pl.BlockSpec((tm, D_MODEL), lambda t: (jnp.maximum(t - 1, 0), 0)),
        out_shape=jax.ShapeDtypeStruct((n, D_MODEL), F32),
        scratch_shapes=[pltpu.VMEM((tm, D_MODEL), BF16), pltpu.VMEM((tm, D_MODEL), BF16),
                        pltpu.VMEM((tm, D_MODEL), F32), pltpu.VMEM((tm, D_MODEL), BF16),
                        pltpu.VMEM((tm, D_MODEL), F32), pltpu.VMEM((tm, D_MODEL), BF16),
                        pltpu.VMEM((tm, D_FF), BF16), pltpu.VMEM((tm, D_MODEL), F32)],
        compiler_params=pltpu.CompilerParams(dimension_semantics=("arbitrary",), vmem_limit_bytes=VMEM_LIMIT),
        name="tail",
    )(h2d, mk, mv, gc, gf, gl, wcq, wco, wg, wu, wd)


def _mem_kv_kernel(m_ref, g_ref, wk_ref, wv_ref, k_ref, v_ref, kb_ref, vb_ref):
    mn = (_rms(m_ref[...]) * g_ref[...]).astype(BF16)
    k = jnp.dot(mn, wk_ref[...], preferred_element_type=F32)
    v = jnp.dot(mn, wv_ref[...], preferred_element_type=F32)
    for hd in range(MEM_HEADS):
        k_ref[:, hd, :] = k[:, hd * MEM_HD:(hd + 1) * MEM_HD]
        v_ref[:, hd, :] = v[:, hd * MEM_HD:(hd + 1) * MEM_HD]
    kb_ref[...] = k.astype(BF16)
    vb_ref[...] = v.astype(BF16)


def _mem_kv(mem2d, g, wk, wv, tm):
    n = mem2d.shape[0]
    assert n % tm == 0
    blk = pl.BlockSpec((tm, D_MODEL), lambda i: (i, 0))
    blk4 = pl.BlockSpec((tm, MEM_HEADS, MEM_HD), lambda i: (i, 0, 0))
    return pl.pallas_call(
        _mem_kv_kernel,
        grid=(n // tm,),
        in_specs=[blk, _resident((1, D_MODEL)), _resident(wk.shape), _resident(wv.shape)],
        out_specs=[blk4, blk4, blk, blk],
        out_shape=[jax.ShapeDtypeStruct((n, MEM_HEADS, MEM_HD), F32)] * 2
                  + [jax.ShapeDtypeStruct((n, D_MODEL), BF16)] * 2,
        compiler_params=pltpu.CompilerParams(dimension_semantics=("parallel",), vmem_limit_bytes=VMEM_LIMIT),
        name="mem_kv",
    )(mem2d, g, wk, wv)


def _log_gamma():
    return np.log(np.float32(1.0) - np.float32(2.0) ** (np.float32(-5.0) - np.arange(RET_HEADS, dtype=np.float32)))


def _rope_tables(pos):
    half = RET_DK // 2
    inv = np.float32(ROPE_BASE) ** (-np.arange(half, dtype=np.float32) / np.float32(half))
    ang = pos.astype(np.float32)[:, None] * inv[None, :]
    return np.cos(ang).astype(np.float32), np.sin(ang).astype(np.float32)


def _retention_tables(R):
    lg = _log_gamma()[:, None, None]
    idx = np.arange(R, dtype=np.float32)
    diff = idx[:, None] - idx[None, :]
    cn = (np.arange(R) // CHUNK)[:, None]
    cm = (np.arange(R) // CHUNK)[None, :]
    mask = np.where(cm == cn, np.exp(lg * np.abs(diff)), np.where(cm < cn, np.exp(lg * diff), np.float32(0.0)))
    qd = np.exp(_log_gamma()[:, None] * (idx[None, :] + np.float32(1.0)))
    kd = np.exp(_log_gamma()[:, None] * (np.float32(R - 1.0) - idx[None, :]))
    gr = np.exp(_log_gamma() * np.float32(R))
    qd_full = np.broadcast_to(qd[:, :, None], (RET_HEADS, R, RET_DV))
    kd_full = np.broadcast_to(kd.T[:, :, None], (R, RET_HEADS, RET_DK)).reshape(R, RET_HEADS * RET_DK)
    f32 = lambda a: jnp.asarray(np.ascontiguousarray(a, dtype=np.float32))
    return f32(mask), f32(qd_full), kd_full.astype(np.float32), f32(gr)


def _rel_bucket():
    i = jnp.arange(CHUNK, dtype=jnp.int32)[:, None]
    j = jnp.arange(SWA_L, dtype=jnp.int32)[None, :]
    rel = (j - WINDOW) - i
    half = REL_BUCKETS // 2
    max_exact = half // 2
    n = jnp.abs(rel)
    large = max_exact + (jnp.log(jnp.maximum(n, 1).astype(F32) / max_exact)
                         / math.log(REL_MAX_DIST / max_exact) * (half - max_exact)).astype(jnp.int32)
    large = jnp.minimum(large, half - 1)
    return jnp.where(rel > 0, half, 0) + jnp.where(n < max_exact, n, large)


def _tile_rows(t, tm):
    reps = -(-tm // t.shape[0])
    return jnp.asarray(np.tile(t, (reps, 1)) if reps > 1 else t)


def kernel(x_prompt, x_sample, cache_ret_state, cache_swa_k, cache_swa_v, cache_mem_k, cache_mem_v, mem_prompt,
           rel_bias, g_attn, w_in, w_ret_out, w_swa_out, w_mix_out, swa_sinks, g_cross, g_mem, w_cq, w_mk, w_mv,
           w_co, g_ffn, w_gate, w_up, w_down, g_final):
    B, S, D = x_prompt.shape
    Bs, T, _ = x_sample.shape
    assert D == D_MODEL and T == CHUNK and S % R_PROMPT == 0 and cache_swa_k.shape[2] == WINDOW
    assert g_attn.shape[0] == 1, "single layer"
    bf = lambda w: w.astype(BF16)
    vec = lambda g: g.reshape(1, D_MODEL)
    w_in_b, wro, wso, wmo = bf(w_in[0]), bf(w_ret_out[0]), bf(w_swa_out[0]), bf(w_mix_out[0])
    wcq, wco, wmk, wmv = bf(w_cq[0]), bf(w_co[0]), bf(w_mk[0]), bf(w_mv[0])
    wg, wu, wd = bf(w_gate[0]), bf(w_up[0]), bf(w_down[0])

    key_of_col, sub_of_col = _swa_col_maps()
    bucket2 = _rel_bucket()[:, key_of_col]
    bias2 = _rel_bias(rel_bias, bucket2, jnp.asarray(sub_of_col)[None, :])
    sinks = swa_sinks[0]

    def layer(x, pos, R, G, st0, hist, mk, mv, has_history):
        nb, seq, _ = x.shape
        x2d = x.reshape(nb * seq, D_MODEL)
        tm = G * R
        cos_t, sin_t = _rope_tables(pos)
        mt, qd, kd, gr = _retention_tables(R)
        h1, st, kv = _front(x2d, vec(g_attn[0]), w_in_b, _tile_rows(cos_t, tm), _tile_rows(sin_t, tm),
                            _tile_rows(kd, tm), st0, hist, mt, qd, gr, bias2, sinks, wro, wso, wmo,
                            G=G, R=R, n_tiles=seq // R, has_history=has_history)
        r_tail = TAIL_ROWS if (G == 1 and seq % TAIL_ROWS == 0) else R
        y = _tail(h1, mk, mv, vec(g_cross[0]), vec(g_ffn[0]), vec(g_final), wcq, wco, wg, wu, wd,
                  G=G, R=r_tail, tiles_per_mem=seq // r_tail)
        return y.reshape(nb, seq, D_MODEL), st, kv

    mk_f, mv_f, mk_b, mv_b = _mem_kv(mem_prompt.reshape(B * N_MEM, D_MODEL), vec(g_mem[0]), wmk, wmv, 512)
    y_p, st_p, kv_p = layer(
        x_prompt, np.arange(S), R_PROMPT, 1, None, None,
        mk_b.reshape(B, N_MEM, D_MODEL), mv_b.reshape(B, N_MEM, D_MODEL), False)

    Gs = 2 if Bs % 2 == 0 else 1
    hist_s = jnp.concatenate([cache_swa_k[0].reshape(Bs, WINDOW, LANES), cache_swa_v[0].reshape(Bs, WINDOW, LANES)],
                             axis=-1)
    y_s, st_s, kv_s = layer(
        x_sample, PAST_LEN + np.arange(T), CHUNK, Gs,
        cache_ret_state[0].astype(F32), hist_s,
        bf(cache_mem_k[0].reshape(Bs, N_MEM, D_MODEL)), bf(cache_mem_v[0].reshape(Bs, N_MEM, D_MODEL)), True)

    kvshape = (1, B, WINDOW, SWA_KV, SWA_HD)
    k_p = kv_p[:, :, :LANES].reshape(kvshape)
    v_p = kv_p[:, :, LANES:].reshape(kvshape)
    k_s = jnp.concatenate([cache_swa_k[0][:, T:], kv_s[:, :, :LANES].reshape(Bs, T, SWA_KV, SWA_HD)], axis=1)[None]
    v_s = jnp.concatenate([cache_swa_v[0][:, T:], kv_s[:, :, LANES:].reshape(Bs, T, SWA_KV, SWA_HD)], axis=1)[None]
    mem_shape = (1, B, N_MEM, MEM_HEADS, MEM_HD)
    return (y_p, y_s, st_p[None], st_s[None], k_p, k_s, v_p, v_s, mk_f.reshape(mem_shape), mv_f.reshape(mem_shape))
```

```python
import functools
import math

import numpy as np
import jax
import jax.numpy as jnp
from jax import lax
from jax.experimental import pallas as pl
from jax.experimental.pallas import tpu as pltpu

F32 = jnp.float32
BF16 = jnp.bfloat16

D_MODEL = 1024
CHUNK = 64
EPS = 1e-6
NEG_INF = -1e30
PAST_LEN = 4096

RET_HEADS = 4
RET_DK = 256
RET_DV = 256
ROPE_BASE = 10000.0

SWA_HEADS = 16
SWA_KV = 2
SWA_GROUP = 8
SWA_HD = 64
WINDOW = 128
SWA_L = WINDOW + CHUNK

REL_BUCKETS = 32
REL_MAX_DIST = 128

N_MEM = 256
MEM_HEADS = 4
MEM_HD = 256
D_FF = 2816

OFF_RQ, OFF_RK, OFF_RV, OFF_RG, OFF_SQ, OFF_SKV, OFF_GA, OFF_GB = 0, 1024, 2048, 3072, 4096, 5120, 5376, 6400
D_IN = 7424

LANES = 128
HALF = LANES // 2
VMEM_LIMIT = 56 * 1024 * 1024

FF_COLS = 256
MERGE_COLS = 256
GATE_COLS = 512
KV_COLS = 2 * SWA_KV * SWA_HD
PIPE_DEPTH = 3
TAIL_ROWS = 512
FRONT_SUBTILES = 2
R_PROMPT = 256


def _resident(shape):
    nd = len(shape)
    return pl.BlockSpec(shape, lambda *_: (0,) * nd, pipeline_mode=pl.Buffered(1))


def _rms(x):
    return x * lax.rsqrt(jnp.mean(x * x, axis=-1, keepdims=True) + EPS)


def _interleave(primary, secondary):
    out, j = [], 0
    for k, piece in enumerate(primary):
        out.append(piece)
        while j < len(secondary) and (j + 1) * len(primary) <= (k + 1) * len(secondary):
            out.append(secondary[j])
            j += 1
    return out + secondary[j:]


def _swa_col_maps():
    key = np.concatenate([np.arange(LANES), np.arange(LANES), LANES + np.arange(HALF), LANES + np.arange(HALF)])
    sub = np.concatenate([np.zeros(LANES), np.ones(LANES), np.zeros(HALF), np.ones(HALF)]).astype(np.int32)
    return key.astype(np.int32), sub


def _rel_bias_kernel(tab_ref, bucket_ref, sub_ref, out_ref):
    bucket = bucket_ref[...]
    is_b = sub_ref[...] > 0
    for hk in range(SWA_KV):
        for gp in range(SWA_GROUP // 2):
            h0 = hk * SWA_GROUP + gp * 2
            acc = jnp.zeros(bucket.shape, F32)
            for b in range(REL_BUCKETS):
                val = jnp.where(is_b, tab_ref[b, h0 + 1], tab_ref[b, h0])
                acc = jnp.where(bucket == b, val, acc)
            out_ref[hk, gp * CHUNK:(gp + 1) * CHUNK, :] = acc


def _rel_bias(table, bucket2, sub2):
    ncol = bucket2.shape[1]
    return pl.pallas_call(
        _rel_bias_kernel,
        in_specs=[pl.BlockSpec(memory_space=pltpu.SMEM),
                  pl.BlockSpec(memory_space=pltpu.VMEM),
                  pl.BlockSpec(memory_space=pltpu.VMEM)],
        out_specs=pl.BlockSpec(memory_space=pltpu.VMEM),
        out_shape=jax.ShapeDtypeStruct((SWA_KV, 4 * CHUNK, ncol), F32),
        name="rel_bias",
    )(table, bucket2, sub2)


def _front_kernel(x_ref, xp_ref, g_ref, w_ref, cos_ref, sin_ref, kdec_ref,
                  st0_ref, hist_ref, mt_ref, qd_ref, gr_ref, bias_ref, sink_ref,
                  wro_ref, wso_ref, wmo_ref,
                  h_ref, st_ref, kvl_ref,
                  xn, rq, rk, rkd, rv, rgs, sq, gas, gbs, state, ka, kb, va, vb, ain, swo, mixs,
                  *, G, R, S, n_tiles, has_history):
    Rg = S * R
    t = pl.program_id(0)
    n_real = pl.num_programs(0) - 1
    i = lax.rem(jnp.minimum(t, n_real - 1), n_tiles)
    lane =lax.broadcasted_iota(jnp.int32, (1, LANES), 1)
    lo = lane < HALF

    def split_heads(x):
        r = pltpu.roll(x, HALF, axis=1)
        z = jnp.zeros_like(x)
        a0 = jnp.where(lo, x, z).astype(BF16)
        b0 = jnp.where(lo, z, r).astype(BF16)
        a1 = jnp.where(lo, r, z).astype(BF16)
        b1 = jnp.where(lo, z, x).astype(BF16)
        return (a0, a1), (b0, b1)

    @pl.when(t == 0)
    def _():
        ain[...] = jnp.zeros(ain.shape, BF16)
        swo[...] = jnp.zeros(swo.shape, BF16)
        gas[...] = jnp.zeros(gas.shape, BF16)
        gbs[...] = jnp.zeros(gbs.shape, BF16)

    @pl.when(i == 0)
    def _():
        state[...] = st0_ref[...] if has_history else jnp.zeros(state.shape, F32)
        for g in range(G):
            hist = hist_ref[g] if has_history else jnp.zeros((WINDOW, KV_COLS), F32)
            tail = slice(Rg, Rg + WINDOW)
            (a0, a1), (b0, b1) = split_heads(hist[:, :LANES])
            ka[g, 0, tail], ka[g, 1, tail], kb[g, 0, tail], kb[g, 1, tail] = a0, a1, b0, b1
            (a0, a1), (b0, b1) = split_heads(hist[:, LANES:])
            va[g, 0, tail], va[g, 1, tail], vb[g, 0, tail], vb[g, 1, tail] = a0, a1, b0, b1

    def norm():
        xn[...] = (_rms(x_ref[...]) * g_ref[...]).astype(BF16)

    def proj(c0, n):
        return jnp.dot(xn[...], w_ref[:, c0:c0 + n], preferred_element_type=F32)

    def proj_head(h):
        c = h * RET_DK
        cos = cos_ref[...]
        sin = sin_ref[...]
        acc = proj(OFF_RQ + c, RET_DK)
        x1, x2 = acc[:, :LANES], acc[:, LANES:]
        rq[:, c:c + LANES] = (x1 * cos - x2 * sin).astype(BF16)
        rq[:, c + LANES:c + RET_DK] = (x2 * cos + x1 * sin).astype(BF16)
        acc = proj(OFF_RK + c, RET_DK)
        x1, x2 = acc[:, :LANES], acc[:, LANES:]
        k1 = (x1 * cos - x2 * sin) * (RET_DK ** -0.5)
        k2 = (x2 * cos + x1 * sin) * (RET_DK ** -0.5)
        rk[:, c:c + LANES] = k1.astype(BF16)
        rk[:, c + LANES:c + RET_DK] = k2.astype(BF16)
        kd = jnp.concatenate([kdec_ref[:, h * LANES:(h + 1) * LANES]] * (G * S), axis=0)
        rkd[:, c:c + LANES] = (k1 * kd).astype(BF16)
        rkd[:, c + LANES:c + RET_DK] = (k2 * kd).astype(BF16)
        rv[:, c:c + RET_DV] = proj(OFF_RV + c, RET_DV).astype(BF16)
        rgs[:, c:c + RET_DV] = jax.nn.silu(proj(OFF_RG + c, RET_DV)).astype(BF16)

    def proj_sq(j):
        c = j * GATE_COLS
        sq[:, c:c + GATE_COLS] = proj(OFF_SQ + c, GATE_COLS).astype(BF16)

    def proj_kv():
        kvt_all = proj(OFF_SKV, KV_COLS)
        for g in range(G):
            kvl_ref[g] = kvt_all[(g + 1) * Rg - min(Rg, WINDOW):(g + 1) * Rg]
        for g in range(G):
            kvt = kvt_all[g * Rg:(g + 1) * Rg]
            for buf in (ka, kb, va, vb):
                for hk in range(SWA_KV):
                    buf[g, hk, :WINDOW] = buf[g, hk, Rg:Rg + WINDOW]
            (a0, a1), (b0, b1) = split_heads(kvt[:, :LANES])
            ka[g, 0, WINDOW:], ka[g, 1, WINDOW:], kb[g, 0, WINDOW:], kb[g, 1, WINDOW:] = a0, a1, b0, b1
            (a0, a1), (b0, b1) = split_heads(kvt[:, LANES:])
            va[g, 0, WINDOW:], va[g, 1, WINDOW:], vb[g, 0, WINDOW:], vb[g, 1, WINDOW:] = a0, a1, b0, b1

    def proj_gate(dst, off, j):
        c = j * GATE_COLS
        dst[:, c:c + GATE_COLS] = jax.nn.sigmoid(proj(off + c, GATE_COLS)).astype(BF16)

    def ret_front(g, sub, h):
        rows = slice(g * Rg + sub * R, g * Rg + (sub + 1) * R)
        cols = slice(h * RET_DK, (h + 1) * RET_DK)
        q = rq[rows, cols]
        v = rv[rows, cols]
        s_prev = state[g, h]
        sc = lax.dot_general(q, rk[rows, cols], (((1,), (1,)), ((), ())), preferred_element_type=F32)
        qs = jnp.dot(q, s_prev.astype(BF16), preferred_element_type=F32)
        kv_new = lax.dot_general(rkd[rows, cols], v, (((0,), (0,)), ((), ())), preferred_element_type=F32)
        return sc, qs, kv_new, s_prev, v

    def ret_back(g, sub, h, vals):
        sc, qs, kv_new, s_prev, v = vals
        rows = slice(g * Rg + sub * R, g * Rg + (sub + 1) * R)
        cols = slice(h * RET_DK, (h + 1) * RET_DK)
        qd = jnp.concatenate([qd_ref[h]] * (RET_DV // LANES), axis=1)
        o = jnp.dot((sc * mt_ref[h]).astype(BF16), v, preferred_element_type=F32) + qs * qd
        state[g, h] = s_prev * gr_ref[h] + kv_new
        ain[rows, cols] =(_rms(o) * rgs[rows, cols].astype(F32)).astype(BF16)

    ones_a = jnp.broadcast_to(jnp.where(lo, 1.0, 0.0).astype(BF16), (LANES, LANES))
    ones_b = jnp.broadcast_to(jnp.where(lo, 0.0, 1.0).astype(BF16), (LANES, LANES))
    lane3 = lax.broadcasted_iota(jnp.int32, (1, 3 * LANES), 1)
    key_of_col = jnp.where(lane3 < 2 * LANES, lane3 % LANES, LANES + lane3 % HALF)

    def swa_front(g, c, hk):
        w0 = c * CHUNK
        r0 = g * Rg + c * CHUNK
        q2 = jnp.concatenate(
            [sq[r0:r0 + CHUNK, hk * 512 + gp * LANES: hk * 512 + (gp + 1) * LANES] for gp in range(4)],
            axis=0)
        kbd = jnp.concatenate([ka[g, hk, w0:w0 + LANES], kb[g, hk, w0:w0 + LANES],
                               ka[g, hk, w0 + LANES:w0 + SWA_L], kb[g, hk, w0 + LANES:w0 + SWA_L]],
                              axis=0)
        return lax.dot_general(q2, kbd, (((1,), (1,)), ((), ())), preferred_element_type=F32)

    def swa_back(g, c, hk, s):
        w0 = c * CHUNK
        r0 = g * Rg + c * CHUNK
        s = s * (SWA_HD ** -0.5) + bias_ref[hk]
        if not has_history and c < WINDOW // CHUNK:
            kpos = key_of_col + (i * Rg + c * CHUNK - WINDOW)
            s = jnp.where(kpos >= 0, s, NEG_INF)
        t0, t1, t2 = s[:, :LANES], s[:, LANES:2 * LANES], s[:, 2 * LANES:]
        m_a = jnp.max(jnp.maximum(t0, jnp.where(lo, t2, NEG_INF)), axis=-1, keepdims=True)
        m_b = jnp.max(jnp.maximum(t1, jnp.where(lo, NEG_INF, t2)), axis=-1, keepdims=True)
        sk_a = jnp.concatenate([jnp.full((CHUNK, 1), sink_ref[hk * SWA_GROUP + 2 * gp], F32) for gp in range(4)],
                               axis=0)
        sk_b = jnp.concatenate([jnp.full((CHUNK, 1), sink_ref[hk * SWA_GROUP + 2 * gp + 1], F32) for gp in range(4)],
                               axis=0)
        m_a = jnp.maximum(m_a, sk_a)
        m_b = jnp.maximum(m_b, sk_b)
        e = jnp.concatenate([jnp.exp(t0 - m_a), jnp.exp(t1 - m_b),
                             jnp.exp(t2 - jnp.where(lo, m_a, m_b))], axis=1).astype(BF16)
        vbd = jnp.concatenate([
            jnp.concatenate([va[g, hk, w0:w0 + LANES], ones_a], axis=1),
            jnp.concatenate([vb[g, hk, w0:w0 + LANES], ones_b], axis=1),
            jnp.concatenate([va[g, hk, w0 + LANES:w0 + SWA_L], ones_a[:HALF]], axis=1),
            jnp.concatenate([vb[g, hk, w0 + LANES:w0 + SWA_L], ones_b[:HALF]], axis=1)], axis=0)
        oa = jnp.dot(e, vbd, preferred_element_type=F32)
        den = oa[:, LANES:] + jnp.where(lo, jnp.exp(sk_a - m_a), jnp.exp(sk_b - m_b))
        o2 = (oa[:, :LANES] / den).astype(BF16)
        for gp in range(4):
            swo[r0:r0 + CHUNK, hk * 512 + gp * LANES: hk * 512 + (gp + 1) * LANES] = o2[gp * CHUNK:(gp + 1) * CHUNK]

    def merge_ab(j):
        cols = slice(j * MERGE_COLS, (j + 1) * MERGE_COLS)
        a = jnp.dot(ain[...], wro_ref[:, cols], preferred_element_type=F32)
        b = jnp.dot(swo[...], wso_ref[:, cols], preferred_element_type=F32)
        mixs[:, cols] = (gas[:, cols].astype(F32) * a + gbs[:, cols].astype(F32) * b).astype(BF16)

    def merge_y(j):
        cols = slice(j * MERGE_COLS, (j + 1) * MERGE_COLS)
        h_ref[:, cols] = xp_ref[:, cols] + jnp.dot(mixs[...], wmo_ref[:, cols], preferred_element_type=F32)

    n_mc = D_MODEL // MERGE_COLS
    n_gc = D_MODEL // GATE_COLS
    dense = [(functools.partial(merge_ab, 0), None), (norm, None), (functools.partial(merge_ab, 1), None),
             (functools.partial(proj_head, 0), ("head", 0))]
    dense += [(functools.partial(merge_ab, j), ("merged", 0) if j == n_mc - 1 else None) for j in range(2, n_mc)]
    dense += [(functools.partial(proj_head, h), ("head", h)) for h in range(1, RET_HEADS)]
    dense += [(functools.partial(proj_sq, j), None) for j in range(n_gc)]
    dense.append((proj_kv, ("swa", 0)))
    dense += [(functools.partial(proj_gate, gas, OFF_GA, j), None) for j in range(n_gc)]
    dense += [(functools.partial(proj_gate, gbs, OFF_GB, j), None) for j in range(n_gc)]
    dense += [(functools.partial(merge_y, j), None) for j in range(n_mc)]

    assert S == 1 or PIPE_DEPTH < RET_HEADS * G
    items = [(ret_front, ret_back, (g, sub, h), ("head", h))
             for sub in range(S) for h in range(RET_HEADS) for g in range(G)]
    items += [(swa_front, swa_back, (g, c, hk), ("swa", 0))
              for g in range(G) for c in range(Rg // CHUNK) for hk in range(SWA_KV)]
    tasks = []
    for n in range(len(items) + PIPE_DEPTH):
        if n < len(items):
            tasks.append(("front", n))
        if n >= PIPE_DEPTH:
            tasks.append(("back", n - PIPE_DEPTH))

    pending, done, nxt = {}, set(), 0

    def run_task(kind, n):
        front, back, args, _ = items[n]
        if kind == "front":
            pending[n] = front(*args)
        else:
            back(*args, pending.pop(n))

    for k, (piece, tag) in enumerate(dense):
        piece()
        if tag is not None:
            done.add(tag)
        quota = -(-(len(tasks) - nxt) // (len(dense) - k))
        while nxt < len(tasks) and quota > 0:
            kind, n = tasks[nxt]
            if kind == "front" and items[n][3] not in done:
                break
            if kind == "back" and ("merged", 0) not in done:
                break
            run_task(kind, n)
            nxt += 1
            quota -= 1
    assert nxt == len(tasks)

    @pl.when((i == n_tiles - 1) & (t < n_real))
    def _():
        st_ref[...] = state[...]


def _front(x2d, g, w_in, cos_t, sin_t, kdec_t, st0, hist, mt, qd, gr, bias2, sinks, wro, wso, wmo,
           *, G, R, S, n_tiles, has_history):
    n = x2d.shape[0]
    Rg = S * R
    tm = G * Rg
    n_steps = n // tm
    n_rows = n // (Rg * n_tiles)
    keep = min(Rg, WINDOW)
    assert n_steps * tm == n and n_steps % n_tiles == 0 and (G == 1 or n_tiles == 1)
    assert cos_t.shape == (n_tiles * tm, LANES) and kdec_t.shape == (R, RET_HEADS * LANES)
    cur = lambda t: jnp.minimum(t, n_steps - 1)
    prev = lambda t: jnp.maximum(t - 1, 0)
    per_row = pl.BlockSpec((G, RET_HEADS, RET_DK, RET_DV), lambda t: (cur(t) // n_tiles, 0, 0, 0))
    rope = pl.BlockSpec((tm, LANES), lambda t: (cur(t) % n_tiles, 0))
    kw = dict(G=G, R=R, S=S, n_tiles=n_tiles, has_history=has_history)
    if has_history:
        kern = functools.partial(_front_kernel, **kw)
        carried = [st0, hist]
        carried_specs = [per_row, pl.BlockSpec((G, WINDOW, KV_COLS), lambda t: (cur(t) // n_tiles, 0, 0))]
    else:
        def kern(x_ref, xp_ref, g_ref, w_ref, cos_ref, sin_ref, kdec_ref, *rest):
            _front_kernel(x_ref, xp_ref, g_ref, w_ref, cos_ref, sin_ref, kdec_ref, None, None, *rest, **kw)
        carried, carried_specs = [], []
    tile_bf = pltpu.VMEM((tm, D_MODEL), BF16)
    return pl.pallas_call(
        kern,
        grid=(n_steps + 1,),
        in_specs=[pl.BlockSpec((tm, D_MODEL), lambda t: (cur(t), 0)),
                  pl.BlockSpec((tm, D_MODEL), lambda t: (prev(t), 0)),
                  _resident((1, D_MODEL)), _resident(w_in.shape), rope, rope, _resident(kdec_t.shape)]
                 + carried_specs
                 + [_resident(mt.shape), _resident(qd.shape),
                  pl.BlockSpec(memory_space=pltpu.SMEM),
                  _resident(bias2.shape),
                  pl.BlockSpec(memory_space=pltpu.SMEM),
                  _resident(wro.shape), _resident(wso.shape), _resident(wmo.shape)],
        out_specs=[pl.BlockSpec((tm, D_MODEL), lambda t: (prev(t), 0)), per_row,
                   pl.BlockSpec((G, keep, KV_COLS), lambda t: (cur(t) // n_tiles, 0, 0))],
        out_shape=[jax.ShapeDtypeStruct((n, D_MODEL), F32),
                   jax.ShapeDtypeStruct((n_rows, RET_HEADS, RET_DK, RET_DV), F32),
                   jax.ShapeDtypeStruct((n_rows, keep, KV_COLS), F32)],
        scratch_shapes=[tile_bf] * 9
                       + [pltpu.VMEM((G, RET_HEADS, RET_DK, RET_DV), F32)]
                       + [pltpu.VMEM((G, SWA_KV, WINDOW + Rg, LANES), BF16)] * 4
                       + [tile_bf] * 3,
        compiler_params=pltpu.CompilerParams(dimension_semantics=("arbitrary",), vmem_limit_bytes=VMEM_LIMIT),
        name="front",
    )(x2d, x2d, g, w_in, cos_t, sin_t, kdec_t, *carried, mt, qd, gr, bias2, sinks, wro, wso, wmo)


def _tail_kernel(h_ref, mk_ref, mv_ref, gc_ref, gf_ref, gl_ref, wcq_ref, wco_ref, wg_ref, wu_ref, wd_ref,
                 y_ref, qs, att, h2s, hn2s, h2_prev, hn2_prev, acts, h3s, *, G, R):
    @pl.when(pl.program_id(0) == 0)
    def _():
        h2s[...] = jnp.zeros(h2s.shape, F32)
        hn2s[...] = jnp.zeros(hn2s.shape, BF16)

    h2_prev[...] = h2s[...]
    hn2_prev[...] = hn2s[...]

    def q_proj():
        hn = (_rms(h_ref[...]) * gc_ref[...]).astype(BF16)
        qs[...] = jnp.dot(hn, wcq_ref[...], preferred_element_type=F32).astype(BF16)

    def head_front(g, hd):
        rows = slice(g * R, (g + 1) * R)
        cols = slice(hd * MEM_HD, (hd + 1) * MEM_HD)
        return lax.dot_general(qs[rows, cols], mk_ref[g, :, cols], (((1,), (1,)), ((), ())),
                               preferred_element_type=F32)

    def head_back(g, hd, s):
        rows = slice(g * R, (g + 1) * R)
        cols = slice(hd * MEM_HD, (hd + 1) * MEM_HD)
        s = s * (MEM_HD ** -0.5)
        e = jnp.exp(s - jnp.max(s, axis=-1, keepdims=True))
        den = jnp.sum(e, axis=-1, keepdims=True)
        o = jnp.dot(e.astype(BF16), mv_ref[g, :, cols], preferred_element_type=F32) / den
        att[rows, cols] = o.astype(BF16)

    def attn_out():
        h2 = h_ref[...] + jnp.dot(att[...], wco_ref[...], preferred_element_type=F32)
        h2s[...] = h2
        hn2s[...] = (_rms(h2) * gf_ref[...]).astype(BF16)

    def ff_chunk(j):
        cols = slice(j * FF_COLS, (j + 1) * FF_COLS)
        hn = hn2_prev[...]
        gate = jnp.dot(hn, wg_ref[:, cols], preferred_element_type=F32)
        up = jnp.dot(hn, wu_ref[:, cols], preferred_element_type=F32)
        acts[:, cols] = (jax.nn.silu(gate) * up).astype(BF16)

    sq_sums = []

    def down_chunk(j):
        cols = slice(j * FF_COLS, (j + 1) * FF_COLS)
        h3 = h2_prev[:, cols] + jnp.dot(acts[...], wd_ref[:, cols], preferred_element_type=F32)
        h3s[:, cols] = h3
        sq_sums.append(jnp.sum(h3 * h3, axis=-1, keepdims=True))

    def final_norm():
        mean_sq = functools.reduce(lambda a, b: a + b, sq_sums) * (1.0 / D_MODEL)
        y_ref[...] = h3s[...] * lax.rsqrt(mean_sq + EPS) * gl_ref[...]

    heads = [(g, hd) for g in range(G) for hd in range(MEM_HEADS)]
    pending = {}

    def front(n):
        pending[n] = head_front(*heads[n])

    def back(n):
        head_back(*heads[n], pending.pop(n))

    attn = [q_proj]
    for n in range(len(heads) + 1):
        if n < len(heads):
            attn.append(functools.partial(front, n))
        if n >= 1:
            attn.append(functools.partial(back, n - 1))
    attn.append(attn_out)
    ffn = [functools.partial(ff_chunk, j) for j in range(D_FF // FF_COLS)]
    ffn += [functools.partial(down_chunk, j) for j in range(D_MODEL // FF_COLS)]
    ffn.append(final_norm)
    for piece in _interleave(ffn, attn):
        piece()


def _tail(h2d, mk, mv, gc, gf, gl, wcq, wco, wg, wu, wd, *, G, R, tiles_per_mem):
    n = h2d.shape[0]
    tm = G * R
    n_steps = n // tm
    assert n_steps * tm == n and (G == 1 or tiles_per_mem == 1)
    a_tile = lambda t: jnp.minimum(t, n_steps - 1)
    mem = pl.BlockSpec((G, N_MEM, D_MODEL), lambda t: (a_tile(t) // tiles_per_mem, 0, 0))
    vec = _resident((1, D_MODEL))
    kern = functools.partial(_tail_kernel, G=G, R=R)
    return pl.pallas_call(
        kern,
        grid=(n_steps + 1,),
        in_specs=[pl.BlockSpec((tm, D_MODEL), lambda t: (a_tile(t), 0)), mem, mem, vec, vec, vec,
                  _resident(wcq.shape), _resident(wco.shape), _resident(wg.shape), _resident(wu.shape),
                  _resident(wd.shape)],
        out_specs=pl.BlockSpec((tm, D_MODEL), lambda t: (jnp.maximum(t - 1, 0), 0)),
        out_shape=jax.ShapeDtypeStruct((n, D_MODEL), F32),
        scratch_shapes=[pltpu.VMEM((tm, D_MODEL), BF16), pltpu.VMEM((tm, D_MODEL), BF16),
                        pltpu.VMEM((tm, D_MODEL), F32), pltpu.VMEM((tm, D_MODEL), BF16),
                        pltpu.VMEM((tm, D_MODEL), F32), pltpu.VMEM((tm, D_MODEL), BF16),
                        pltpu.VMEM((tm, D_FF), BF16), pltpu.VMEM((tm, D_MODEL), F32)],
        compiler_params=pltpu.CompilerParams(dimension_semantics=("arbitrary",), vmem_limit_bytes=VMEM_LIMIT),
        name="tail",
    )(h2d, mk, mv, gc, gf, gl, wcq, wco, wg, wu, wd)


def _mem_kv_kernel(m_ref, g_ref, wk_ref, wv_ref, k_ref, v_ref, kb_ref, vb_ref):
    mn = (_rms(m_ref[...]) * g_ref[...]).astype(BF16)
    k = jnp.dot(mn, wk_ref[...], preferred_element_type=F32)
    v = jnp.dot(mn, wv_ref[...], preferred_element_type=F32)
    for hd in range(MEM_HEADS):
        k_ref[:, hd, :] = k[:, hd * MEM_HD:(hd + 1) * MEM_HD]
        v_ref[:, hd, :] = v[:, hd * MEM_HD:(hd + 1) * MEM_HD]
    kb_ref[...] = k.astype(BF16)
    vb_ref[...] = v.astype(BF16)


def _mem_kv(mem2d, g, wk, wv, tm):
    n = mem2d.shape[0]
    assert n % tm == 0
    blk = pl.BlockSpec((tm, D_MODEL), lambda i: (i, 0))
    blk4 = pl.BlockSpec((tm, MEM_HEADS, MEM_HD), lambda i: (i, 0, 0))
    return pl.pallas_call(
        _mem_kv_kernel,
        grid=(n // tm,),
        in_specs=[blk, _resident((1, D_MODEL)), _resident(wk.shape), _resident(wv.shape)],
        out_specs=[blk4, blk4, blk, blk],
        out_shape=[jax.ShapeDtypeStruct((n, MEM_HEADS, MEM_HD), F32)] * 2
                  + [jax.ShapeDtypeStruct((n, D_MODEL), BF16)] * 2,
        compiler_params=pltpu.CompilerParams(dimension_semantics=("parallel",), vmem_limit_bytes=VMEM_LIMIT),
        name="mem_kv",
    )(mem2d, g, wk, wv)


def _log_gamma():
    return np.log(np.float32(1.0) - np.float32(2.0) ** (np.float32(-5.0) - np.arange(RET_HEADS, dtype=np.float32)))


def _rope_tables(pos):
    half = RET_DK // 2
    inv = np.float32(ROPE_BASE) ** (-np.arange(half, dtype=np.float32) / np.float32(half))
    ang = pos.astype(np.float32)[:, None] * inv[None, :]
    return np.cos(ang).astype(np.float32), np.sin(ang).astype(np.float32)


def _retention_tables(R):
    lg = _log_gamma()[:, None, None]
    idx = np.arange(R, dtype=np.float32)
    diff = idx[:, None] - idx[None, :]
    cn = (np.arange(R) // CHUNK)[:, None]
    cm = (np.arange(R) // CHUNK)[None, :]
    mask = np.where(cm == cn, np.exp(lg * np.abs(diff)), np.where(cm < cn, np.exp(lg * diff), np.float32(0.0)))
    qd = np.exp(_log_gamma()[:, None] * (idx[None, :] + np.float32(1.0)))
    kd = np.exp(_log_gamma()[:, None] * (np.float32(R - 1.0) - idx[None, :]))
    gr = np.exp(_log_gamma() * np.float32(R))
    qd_full = np.broadcast_to(qd[:, :, None], (RET_HEADS, R, LANES))
    kd_full = np.broadcast_to(kd.T[:, :, None], (R, RET_HEADS, LANES)).reshape(R, RET_HEADS * LANES)
    f32 = lambda a: jnp.asarray(np.ascontiguousarray(a, dtype=np.float32))
    return f32(mask), f32(qd_full), f32(kd_full), f32(gr)


def _rel_bucket():
    i = jnp.arange(CHUNK, dtype=jnp.int32)[:, None]
    j = jnp.arange(SWA_L, dtype=jnp.int32)[None, :]
    rel = (j - WINDOW) - i
    half = REL_BUCKETS // 2
    max_exact = half // 2
    n = jnp.abs(rel)
    large = max_exact + (jnp.log(jnp.maximum(n, 1).astype(F32) / max_exact)
                         / math.log(REL_MAX_DIST / max_exact) * (half - max_exact)).astype(jnp.int32)
    large = jnp.minimum(large, half - 1)
    return jnp.where(rel > 0, half, 0) + jnp.where(n < max_exact, n, large)


def _tile_rows(t, tm):
    reps = -(-tm // t.shape[0])
    return jnp.asarray(np.tile(t, (reps, 1)) if reps > 1 else t)


def kernel(x_prompt, x_sample, cache_ret_state, cache_swa_k, cache_swa_v, cache_mem_k, cache_mem_v, mem_prompt,
           rel_bias, g_attn, w_in, w_ret_out, w_swa_out, w_mix_out, swa_sinks, g_cross, g_mem, w_cq, w_mk, w_mv,
           w_co, g_ffn, w_gate, w_up, w_down, g_final):
    B, S, D = x_prompt.shape
    Bs, T, _ = x_sample.shape
    assert D == D_MODEL and T == CHUNK and S % R_PROMPT == 0 and cache_swa_k.shape[2] == WINDOW
    assert g_attn.shape[0] == 1, "single layer"
    bf = lambda w: w.astype(BF16)
    vec = lambda g: g.reshape(1, D_MODEL)
    w_in_b, wro, wso, wmo = bf(w_in[0]), bf(w_ret_out[0]), bf(w_swa_out[0]), bf(w_mix_out[0])
    wcq, wco, wmk, wmv = bf(w_cq[0]), bf(w_co[0]), bf(w_mk[0]), bf(w_mv[0])
    wg, wu, wd = bf(w_gate[0]), bf(w_up[0]), bf(w_down[0])

    key_of_col, sub_of_col = _swa_col_maps()
    bucket2 = _rel_bucket()[:, key_of_col]
    bias2 = _rel_bias(rel_bias, bucket2, jnp.asarray(sub_of_col)[None, :])
    sinks = swa_sinks[0]

    def layer(x, pos, R, sub, G, st0, hist, mk, mv, has_history):
        nb, seq, _ = x.shape
        x2d = x.reshape(nb * seq, D_MODEL)
        tm = G * sub * R
        cos_t, sin_t = _rope_tables(pos)
        mt, qd, kd, gr = _retention_tables(R)
        h1, st, kv = _front(x2d, vec(g_attn[0]), w_in_b, _tile_rows(cos_t, tm), _tile_rows(sin_t, tm),
                            kd, st0, hist, mt, qd, gr, bias2, sinks, wro, wso, wmo,
                            G=G, R=R, S=sub, n_tiles=seq // (sub * R), has_history=has_history)
        r_tail = TAIL_ROWS if (G == 1 and seq % TAIL_ROWS == 0) else R
        y = _tail(h1, mk, mv, vec(g_cross[0]), vec(g_ffn[0]), vec(g_final), wcq, wco, wg, wu, wd,
                  G=G, R=r_tail, tiles_per_mem=seq // r_tail)
        return y.reshape(nb, seq, D_MODEL), st, kv

    mk_f, mv_f, mk_b, mv_b = _mem_kv(mem_prompt.reshape(B * N_MEM, D_MODEL), vec(g_mem[0]), wmk, wmv, 512)
    y_p, st_p, kv_p = layer(
        x_prompt, np.arange(S), R_PROMPT, FRONT_SUBTILES if S % (FRONT_SUBTILES * R_PROMPT) == 0 else 1, 1, None, None,
        mk_b.reshape(B, N_MEM, D_MODEL), mv_b.reshape(B, N_MEM, D_MODEL), False)

    Gs = 2 if Bs % 2 == 0 else 1
    hist_s = jnp.concatenate([cache_swa_k[0].reshape(Bs, WINDOW, LANES), cache_swa_v[0].reshape(Bs, WINDOW, LANES)],
                             axis=-1)
    y_s, st_s, kv_s = layer(
        x_sample, PAST_LEN + np.arange(T), CHUNK, 1, Gs,
        cache_ret_state[0].astype(F32), hist_s,
        bf(cache_mem_k[0].reshape(Bs, N_MEM, D_MODEL)), bf(cache_mem_v[0].reshape(Bs, N_MEM, D_MODEL)), True)

    kvshape = (1, B, WINDOW, SWA_KV, SWA_HD)
    k_p = kv_p[:, :, :LANES].reshape(kvshape)
    v_p = kv_p[:, :, LANES:].reshape(kvshape)
    k_s = jnp.concatenate([cache_swa_k[0][:, T:], kv_s[:, :, :LANES].reshape(Bs, T, SWA_KV, SWA_HD)], axis=1)[None]
    v_s = jnp.concatenate([cache_swa_v[0][:, T:], kv_s[:, :, LANES:].reshape(Bs, T, SWA_KV, SWA_HD)], axis=1)[None]
    mem_shape = (1, B, N_MEM, MEM_HEADS, MEM_HD)
    return (y_p, y_s, st_p[None], st_s[None], k_p, k_s, v_p, v_s, mk_f.reshape(mem_shape), mv_f.reshape(mem_shape))
```

```python
import functools
import math

import numpy as np
import jax
import jax.numpy as jnp
from jax import lax
from jax.experimental import pallas as pl
from jax.experimental.pallas import tpu as pltpu

F32 = jnp.float32
BF16 = jnp.bfloat16

D_MODEL = 1024
CHUNK = 64
EPS = 1e-6
NEG_INF = -1e30
PAST_LEN = 4096

RET_HEADS = 4
RET_DK = 256
RET_DV = 256
ROPE_BASE = 10000.0

SWA_HEADS = 16
SWA_KV = 2
SWA_GROUP = 8
SWA_HD = 64
WINDOW = 128
SWA_L = WINDOW + CHUNK

REL_BUCKETS = 32
REL_MAX_DIST = 128

N_MEM = 256
MEM_HEADS = 4
MEM_HD = 256
D_FF = 2816

OFF_RQ, OFF_RK, OFF_RV, OFF_RG, OFF_SQ, OFF_SKV, OFF_GA, OFF_GB = 0, 1024, 2048, 3072, 4096, 5120, 5376, 6400
D_IN = 7424

LANES = 128
HALF = LANES // 2
VMEM_LIMIT = 56 * 1024 * 1024

FF_COLS = 256
MERGE_COLS = 256
GATE_COLS = 512
KV_COLS = 2 * SWA_KV * SWA_HD
PIPE_DEPTH = 3
TAIL_ROWS = 512
FRONT_SUBTILES = 2
R_PROMPT = 256


def _resident(shape):
    nd = len(shape)
    return pl.BlockSpec(shape, lambda *_: (0,) * nd, pipeline_mode=pl.Buffered(1))


def _rms(x):
    return x * lax.rsqrt(jnp.mean(x * x, axis=-1, keepdims=True) + EPS)


def _interleave(primary, secondary):
    out, j = [], 0
    for k, piece in enumerate(primary):
        out.append(piece)
        while j < len(secondary) and (j + 1) * len(primary) <= (k + 1) * len(secondary):
            out.append(secondary[j])
            j += 1
    return out + secondary[j:]


def _swa_col_maps():
    key = np.concatenate([np.arange(LANES), np.arange(LANES), LANES + np.arange(HALF), LANES + np.arange(HALF)])
    sub = np.concatenate([np.zeros(LANES), np.ones(LANES), np.zeros(HALF), np.ones(HALF)]).astype(np.int32)
    return key.astype(np.int32), sub


def _rel_bias_kernel(tab_ref, bucket_ref, sub_ref, out_ref):
    bucket = bucket_ref[...]
    is_b = sub_ref[...] > 0
    for hk in range(SWA_KV):
        for gp in range(SWA_GROUP // 2):
            h0 = hk * SWA_GROUP + gp * 2
            acc = jnp.zeros(bucket.shape, F32)
            for b in range(REL_BUCKETS):
                val = jnp.where(is_b, tab_ref[b, h0 + 1], tab_ref[b, h0])
                acc = jnp.where(bucket == b, val, acc)
            out_ref[hk, gp * CHUNK:(gp + 1) * CHUNK, :] = acc


def _rel_bias(table, bucket2, sub2):
    ncol = bucket2.shape[1]
    return pl.pallas_call(
        _rel_bias_kernel,
        in_specs=[pl.BlockSpec(memory_space=pltpu.SMEM),
                  pl.BlockSpec(memory_space=pltpu.VMEM),
                  pl.BlockSpec(memory_space=pltpu.VMEM)],
        out_specs=pl.BlockSpec(memory_space=pltpu.VMEM),
        out_shape=jax.ShapeDtypeStruct((SWA_KV, 4 * CHUNK, ncol), F32),
        name="rel_bias",
    )(table, bucket2, sub2)


def _front_kernel(x_ref, xp_ref, g_ref, w_ref, cos_ref, sin_ref, kdec_ref,
                  st0_ref, hist_ref, mt_ref, qd_ref, gr_ref, bias_ref, sink_ref,
                  wro_ref, wso_ref, wmo_ref,
                  h_ref, st_ref, kvl_ref,
                  xn, rq, rk, rkd, rv, rgs, sq, gas, gbs, state, ka, kb, va, vb, ain, swo, mixs,
                  *, G, R, S, n_tiles, has_history):
    Rg = S * R
    t = pl.program_id(0)
    n_real = pl.num_programs(0) - 1
    i = lax.rem(jnp.minimum(t, n_real - 1), n_tiles)
    lane =lax.broadcasted_iota(jnp.int32, (1, LANES), 1)
    lo = lane < HALF

    def split_heads(x):
        r = pltpu.roll(x, HALF, axis=1)
        z = jnp.zeros_like(x)
        a0 = jnp.where(lo, x, z).astype(BF16)
        b0 = jnp.where(lo, z, r).astype(BF16)
        a1 = jnp.where(lo, r, z).astype(BF16)
        b1 = jnp.where(lo, z, x).astype(BF16)
        return (a0, a1), (b0, b1)

    @pl.when(t == 0)
    def _():
        ain[...] = jnp.zeros(ain.shape, BF16)
        swo[...] = jnp.zeros(swo.shape, BF16)
        gas[...] = jnp.zeros(gas.shape, BF16)
        gbs[...] = jnp.zeros(gbs.shape, BF16)

    @pl.when(i == 0)
    def _():
        state[...] = st0_ref[...] if has_history else jnp.zeros(state.shape, F32)
        for g in range(G):
            hist = hist_ref[g] if has_history else jnp.zeros((WINDOW, KV_COLS), F32)
            tail = slice(Rg, Rg + WINDOW)
            (a0, a1), (b0, b1) = split_heads(hist[:, :LANES])
            ka[g, 0, tail], ka[g, 1, tail], kb[g, 0, tail], kb[g, 1, tail] = a0, a1, b0, b1
            (a0, a1), (b0, b1) = split_heads(hist[:, LANES:])
            va[g, 0, tail], va[g, 1, tail], vb[g, 0, tail], vb[g, 1, tail] = a0, a1, b0, b1

    def norm():
        xn[...] = (_rms(x_ref[...]) * g_ref[...]).astype(BF16)

    def proj(c0, n):
        return jnp.dot(xn[...], w_ref[:, c0:c0 + n], preferred_element_type=F32)

    def proj_head(h):
        c = h * RET_DK
        cos = cos_ref[...]
        sin = sin_ref[...]
        acc = proj(OFF_RQ + c, RET_DK)
        x1, x2 = acc[:, :LANES], acc[:, LANES:]
        rq[:, c:c + LANES] = (x1 * cos - x2 * sin).astype(BF16)
        rq[:, c + LANES:c + RET_DK] = (x2 * cos + x1 * sin).astype(BF16)
        acc = proj(OFF_RK + c, RET_DK)
        x1, x2 = acc[:, :LANES], acc[:, LANES:]
        k1 = (x1 * cos - x2 * sin) * (RET_DK ** -0.5)
        k2 = (x2 * cos + x1 * sin) * (RET_DK ** -0.5)
        rk[:, c:c + LANES] = k1.astype(BF16)
        rk[:, c + LANES:c + RET_DK] = k2.astype(BF16)
        kd = jnp.concatenate([kdec_ref[:, h * LANES:(h + 1) * LANES]] * (G * S), axis=0)
        rkd[:, c:c + LANES] = (k1 * kd).astype(BF16)
        rkd[:, c + LANES:c + RET_DK] = (k2 * kd).astype(BF16)
        rv[:, c:c + RET_DV] = proj(OFF_RV + c, RET_DV).astype(BF16)
        rgs[:, c:c + RET_DV] = jax.nn.silu(proj(OFF_RG + c, RET_DV)).astype(BF16)

    def proj_sq(j):
        c = j * GATE_COLS
        sq[:, c:c + GATE_COLS] = proj(OFF_SQ + c, GATE_COLS).astype(BF16)

    def proj_kv():
        half = (G * Rg) // 2
        w_kv = w_ref[:, OFF_SKV:OFF_SKV + KV_COLS]
        kvt_all = jnp.concatenate([jnp.dot(xn[:half], w_kv, preferred_element_type=F32),
                                   jnp.dot(xn[half:], w_kv, preferred_element_type=F32)], axis=0)
        for g in range(G):
            kvl_ref[g] = kvt_all[(g + 1) * Rg - min(Rg, WINDOW):(g + 1) * Rg]
        for g in range(G):
            kvt = kvt_all[g * Rg:(g + 1) * Rg]
            for buf in (ka, kb, va, vb):
                for hk in range(SWA_KV):
                    buf[g, hk, :WINDOW] = buf[g, hk, Rg:Rg + WINDOW]
            (a0, a1), (b0, b1) = split_heads(kvt[:, :LANES])
            ka[g, 0, WINDOW:], ka[g, 1, WINDOW:], kb[g, 0, WINDOW:], kb[g, 1, WINDOW:] = a0, a1, b0, b1
            (a0, a1), (b0, b1) = split_heads(kvt[:, LANES:])
            va[g, 0, WINDOW:], va[g, 1, WINDOW:], vb[g, 0, WINDOW:], vb[g, 1, WINDOW:] = a0, a1, b0, b1

    def proj_gate(dst, off, j):
        c = j * GATE_COLS
        dst[:, c:c + GATE_COLS] = jax.nn.sigmoid(proj(off + c, GATE_COLS)).astype(BF16)

    def ret_front(g, sub, h):
        rows = slice(g * Rg + sub * R, g * Rg + (sub + 1) * R)
        cols = slice(h * RET_DK, (h + 1) * RET_DK)
        q = rq[rows, cols]
        v = rv[rows, cols]
        s_prev = state[g, h]
        sc = lax.dot_general(q, rk[rows, cols], (((1,), (1,)), ((), ())), preferred_element_type=F32)
        qs = jnp.dot(q, s_prev.astype(BF16), preferred_element_type=F32)
        kv_new = lax.dot_general(rkd[rows, cols], v, (((0,), (0,)), ((), ())), preferred_element_type=F32)
        return sc, qs, kv_new, s_prev, v

    def ret_back(g, sub, h, vals):
        sc, qs, kv_new, s_prev, v = vals
        rows = slice(g * Rg + sub * R, g * Rg + (sub + 1) * R)
        cols = slice(h * RET_DK, (h + 1) * RET_DK)
        qd = jnp.concatenate([qd_ref[h]] * (RET_DV // LANES), axis=1)
        o = jnp.dot((sc * mt_ref[h]).astype(BF16), v, preferred_element_type=F32) + qs * qd
        state[g, h] = s_prev * gr_ref[h] + kv_new
        ain[rows, cols] =(_rms(o) * rgs[rows, cols].astype(F32)).astype(BF16)

    ones_a = jnp.broadcast_to(jnp.where(lo, 1.0, 0.0).astype(BF16), (LANES, LANES))
    ones_b = jnp.broadcast_to(jnp.where(lo, 0.0, 1.0).astype(BF16), (LANES, LANES))
    lane3 = lax.broadcasted_iota(jnp.int32, (1, 3 * LANES), 1)
    key_of_col = jnp.where(lane3 < 2 * LANES, lane3 % LANES, LANES + lane3 % HALF)

    def swa_front(g, c, hk):
        w0 = c * CHUNK
        r0 = g * Rg + c * CHUNK
        q2 = jnp.concatenate(
            [sq[r0:r0 + CHUNK, hk * 512 + gp * LANES: hk * 512 + (gp + 1) * LANES] for gp in range(4)],
            axis=0)
        kbd = jnp.concatenate([ka[g, hk, w0:w0 + LANES], kb[g, hk, w0:w0 + LANES],
                               ka[g, hk, w0 + LANES:w0 + SWA_L], kb[g, hk, w0 + LANES:w0 + SWA_L]],
                              axis=0)
        return lax.dot_general(q2, kbd, (((1,), (1,)), ((), ())), preferred_element_type=F32)

    def swa_back(g, c, hk, s):
        w0 = c * CHUNK
        r0 = g * Rg + c * CHUNK
        s = s * (SWA_HD ** -0.5) + bias_ref[hk]
        if not has_history and c < WINDOW // CHUNK:
            kpos = key_of_col + (i * Rg + c * CHUNK - WINDOW)
            s = jnp.where(kpos >= 0, s, NEG_INF)
        t0, t1, t2 = s[:, :LANES], s[:, LANES:2 * LANES], s[:, 2 * LANES:]
        m_a = jnp.max(jnp.maximum(t0, jnp.where(lo, t2, NEG_INF)), axis=-1, keepdims=True)
        m_b = jnp.max(jnp.maximum(t1, jnp.where(lo, NEG_INF, t2)), axis=-1, keepdims=True)
        sk_a = jnp.concatenate([jnp.full((CHUNK, 1), sink_ref[hk * SWA_GROUP + 2 * gp], F32) for gp in range(4)],
                               axis=0)
        sk_b = jnp.concatenate([jnp.full((CHUNK, 1), sink_ref[hk * SWA_GROUP + 2 * gp + 1], F32) for gp in range(4)],
                               axis=0)
        m_a = jnp.maximum(m_a, sk_a)
        m_b = jnp.maximum(m_b, sk_b)
        e = jnp.concatenate([jnp.exp(t0 - m_a), jnp.exp(t1 - m_b),
                             jnp.exp(t2 - jnp.where(lo, m_a, m_b))], axis=1).astype(BF16)
        vbd = jnp.concatenate([
            jnp.concatenate([va[g, hk, w0:w0 + LANES], ones_a], axis=1),
            jnp.concatenate([vb[g, hk, w0:w0 + LANES], ones_b], axis=1),
            jnp.concatenate([va[g, hk, w0 + LANES:w0 + SWA_L], ones_a[:HALF]], axis=1),
            jnp.concatenate([vb[g, hk, w0 + LANES:w0 + SWA_L], ones_b[:HALF]], axis=1)], axis=0)
        oa = jnp.dot(e, vbd, preferred_element_type=F32)
        den = oa[:, LANES:] + jnp.where(lo, jnp.exp(sk_a - m_a), jnp.exp(sk_b - m_b))
        o2 = (oa[:, :LANES] / den).astype(BF16)
        for gp in range(4):
            swo[r0:r0 + CHUNK, hk * 512 + gp * LANES: hk * 512 + (gp + 1) * LANES] = o2[gp * CHUNK:(gp + 1) * CHUNK]

    def merge_ab(j):
        cols = slice(j * MERGE_COLS, (j + 1) * MERGE_COLS)
        a = jnp.dot(ain[...], wro_ref[:, cols], preferred_element_type=F32)
        b = jnp.dot(swo[...], wso_ref[:, cols], preferred_element_type=F32)
        mixs[:, cols] = (gas[:, cols].astype(F32) * a + gbs[:, cols].astype(F32) * b).astype(BF16)

    def merge_y(j):
        cols = slice(j * MERGE_COLS, (j + 1) * MERGE_COLS)
        h_ref[:, cols] = xp_ref[:, cols] + jnp.dot(mixs[...], wmo_ref[:, cols], preferred_element_type=F32)

    n_mc = D_MODEL // MERGE_COLS
    n_gc = D_MODEL // GATE_COLS
    dense = [(functools.partial(merge_ab, 0), None), (norm, None), (functools.partial(merge_ab, 1), None),
             (functools.partial(proj_head, 0), ("head", 0))]
    dense += [(functools.partial(merge_ab, j), ("merged", 0) if j == n_mc - 1 else None) for j in range(2, n_mc)]
    dense += [(functools.partial(proj_head, h), ("head", h)) for h in range(1, RET_HEADS)]
    dense += [(functools.partial(proj_sq, j), None) for j in range(n_gc)]
    dense.append((proj_kv, ("swa", 0)))
    dense += [(functools.partial(proj_gate, gas, OFF_GA, j), None) for j in range(n_gc)]
    dense += [(functools.partial(proj_gate, gbs, OFF_GB, j), None) for j in range(n_gc)]
    dense += [(functools.partial(merge_y, j), None) for j in range(n_mc)]

    assert S == 1 or PIPE_DEPTH < RET_HEADS * G
    items = [(ret_front, ret_back, (g, sub, h), ("head", h))
             for sub in range(S) for h in range(RET_HEADS) for g in range(G)]
    items += [(swa_front, swa_back, (g, c, hk), ("swa", 0))
              for g in range(G) for c in range(Rg // CHUNK) for hk in range(SWA_KV)]
    tasks = []
    for n in range(len(items) + PIPE_DEPTH):
        if n < len(items):
            tasks.append(("front", n))
        if n >= PIPE_DEPTH:
            tasks.append(("back", n - PIPE_DEPTH))

    pending, done, nxt = {}, set(), 0

    def run_task(kind, n):
        front, back, args, _ = items[n]
        if kind == "front":
            pending[n] = front(*args)
        else:
            back(*args, pending.pop(n))

    for k, (piece, tag) in enumerate(dense):
        piece()
        if tag is not None:
            done.add(tag)
        quota = -(-(len(tasks) - nxt) // (len(dense) - k))
        while nxt < len(tasks) and quota > 0:
            kind, n = tasks[nxt]
            if kind == "front" and items[n][3] not in done:
                break
            if kind == "back" and ("merged", 0) not in done:
                break
            run_task(kind, n)
            nxt += 1
            quota -= 1
    assert nxt == len(tasks)

    @pl.when((i == n_tiles - 1) & (t < n_real))
    def _():
        st_ref[...] = state[...]


def _front(x2d, g, w_in, cos_t, sin_t, kdec_t, st0, hist, mt, qd, gr, bias2, sinks, wro, wso, wmo,
           *, G, R, S, n_tiles, has_history):
    n = x2d.shape[0]
    Rg = S * R
    tm = G * Rg
    n_steps = n // tm
    n_rows = n // (Rg * n_tiles)
    keep = min(Rg, WINDOW)
    assert n_steps * tm == n and n_steps % n_tiles == 0 and (G == 1 or n_tiles == 1)
    assert cos_t.shape == (n_tiles * tm, LANES) and kdec_t.shape == (R, RET_HEADS * LANES)
    cur = lambda t: jnp.minimum(t, n_steps - 1)
    prev = lambda t: jnp.maximum(t - 1, 0)
    per_row = pl.BlockSpec((G, RET_HEADS, RET_DK, RET_DV), lambda t: (cur(t) // n_tiles, 0, 0, 0))
    rope = pl.BlockSpec((tm, LANES), lambda t: (cur(t) % n_tiles, 0))
    kw = dict(G=G, R=R, S=S, n_tiles=n_tiles, has_history=has_history)
    if has_history:
        kern = functools.partial(_front_kernel, **kw)
        carried = [st0, hist]
        carried_specs = [per_row, pl.BlockSpec((G, WINDOW, KV_COLS), lambda t: (cur(t) // n_tiles, 0, 0))]
    else:
        def kern(x_ref, xp_ref, g_ref, w_ref, cos_ref, sin_ref, kdec_ref, *rest):
            _front_kernel(x_ref, xp_ref, g_ref, w_ref, cos_ref, sin_ref, kdec_ref, None, None, *rest, **kw)
        carried, carried_specs = [], []
    tile_bf = pltpu.VMEM((tm, D_MODEL), BF16)
    return pl.pallas_call(
        kern,
        grid=(n_steps + 1,),
        in_specs=[pl.BlockSpec((tm, D_MODEL), lambda t: (cur(t), 0)),
                  pl.BlockSpec((tm, D_MODEL), lambda t: (prev(t), 0)),
                  _resident((1, D_MODEL)), _resident(w_in.shape), rope, rope, _resident(kdec_t.shape)]
                 + carried_specs
                 + [_resident(mt.shape), _resident(qd.shape),
                  pl.BlockSpec(memory_space=pltpu.SMEM),
                  _resident(bias2.shape),
                  pl.BlockSpec(memory_space=pltpu.SMEM),
                  _resident(wro.shape), _resident(wso.shape), _resident(wmo.shape)],
        out_specs=[pl.BlockSpec((tm, D_MODEL), lambda t: (prev(t), 0)), per_row,
                   pl.BlockSpec((G, keep, KV_COLS), lambda t: (cur(t) // n_tiles, 0, 0))],
        out_shape=[jax.ShapeDtypeStruct((n, D_MODEL), F32),
                   jax.ShapeDtypeStruct((n_rows, RET_HEADS, RET_DK, RET_DV), F32),
                   jax.ShapeDtypeStruct((n_rows, keep, KV_COLS), F32)],
        scratch_shapes=[tile_bf] * 9
                       + [pltpu.VMEM((G, RET_HEADS, RET_DK, RET_DV), F32)]
                       + [pltpu.VMEM((G, SWA_KV, WINDOW + Rg, LANES), BF16)] * 4
                       + [tile_bf] * 3,
        compiler_params=pltpu.CompilerParams(dimension_semantics=("arbitrary",), vmem_limit_bytes=VMEM_LIMIT),
        name="front",
    )(x2d, x2d, g, w_in, cos_t, sin_t, kdec_t, *carried, mt, qd, gr, bias2, sinks, wro, wso, wmo)


def _tail_kernel(h_ref, mk_ref, mv_ref, gc_ref, gf_ref, gl_ref, wcq_ref, wco_ref, wg_ref, wu_ref, wd_ref,
                 y_ref, qs, att, h2s, hn2s, h2_prev, hn2_prev, acts, h3s, *, G, R):
    @pl.when(pl.program_id(0) == 0)
    def _():
        h2s[...] = jnp.zeros(h2s.shape, F32)
        hn2s[...] = jnp.zeros(hn2s.shape, BF16)

    h2_prev[...] = h2s[...]
    hn2_prev[...] = hn2s[...]

    def q_proj():
        hn = (_rms(h_ref[...]) * gc_ref[...]).astype(BF16)
        qs[...] = jnp.dot(hn, wcq_ref[...], preferred_element_type=F32).astype(BF16)

    def head_front(g, hd):
        rows = slice(g * R, (g + 1) * R)
        cols = slice(hd * MEM_HD, (hd + 1) * MEM_HD)
        return lax.dot_general(qs[rows, cols], mk_ref[g, :, cols], (((1,), (1,)), ((), ())),
                               preferred_element_type=F32)

    def head_back(g, hd, s):
        rows = slice(g * R, (g + 1) * R)
        cols = slice(hd * MEM_HD, (hd + 1) * MEM_HD)
        s = s * (MEM_HD ** -0.5)
        e = jnp.exp(s - jnp.max(s, axis=-1, keepdims=True))
        den = jnp.sum(e, axis=-1, keepdims=True)
        o = jnp.dot(e.astype(BF16), mv_ref[g, :, cols], preferred_element_type=F32) / den
        att[rows, cols] = o.astype(BF16)

    def attn_out():
        h2 = h_ref[...] + jnp.dot(att[...], wco_ref[...], preferred_element_type=F32)
        h2s[...] = h2
        hn2s[...] = (_rms(h2) * gf_ref[...]).astype(BF16)

    def ff_chunk(j):
        cols = slice(j * FF_COLS, (j + 1) * FF_COLS)
        hn = hn2_prev[...]
        gate = jnp.dot(hn, wg_ref[:, cols], preferred_element_type=F32)
        up = jnp.dot(hn, wu_ref[:, cols], preferred_element_type=F32)
        acts[:, cols] = (jax.nn.silu(gate) * up).astype(BF16)

    sq_sums = []

    def down_chunk(j):
        cols = slice(j * FF_COLS, (j + 1) * FF_COLS)
        h3 = h2_prev[:, cols] + jnp.dot(acts[...], wd_ref[:, cols], preferred_element_type=F32)
        h3s[:, cols] = h3
        sq_sums.append(jnp.sum(h3 * h3, axis=-1, keepdims=True))

    def final_norm():
        mean_sq = functools.reduce(lambda a, b: a + b, sq_sums) * (1.0 / D_MODEL)
        y_ref[...] = h3s[...] * lax.rsqrt(mean_sq + EPS) * gl_ref[...]

    heads = [(g, hd) for g in range(G) for hd in range(MEM_HEADS)]
    pending = {}

    def front(n):
        pending[n] = head_front(*heads[n])

    def back(n):
        head_back(*heads[n], pending.pop(n))

    attn = [q_proj]
    for n in range(len(heads) + 1):
        if n < len(heads):
            attn.append(functools.partial(front, n))
        if n >= 1:
            attn.append(functools.partial(back, n - 1))
    attn.append(attn_out)
    ffn = [functools.partial(ff_chunk, j) for j in range(D_FF // FF_COLS)]
    ffn += [functools.partial(down_chunk, j) for j in range(D_MODEL // FF_COLS)]
    ffn.append(final_norm)
    for piece in _interleave(ffn, attn):
        piece()


def _tail(h2d, mk, mv, gc, gf, gl, wcq, wco, wg, wu, wd, *, G, R, tiles_per_mem):
    n = h2d.shape[0]
    tm = G * R
    n_steps = n // tm
    assert n_steps * tm == n and (G == 1 or tiles_per_mem == 1)
    a_tile = lambda t: jnp.minimum(t, n_steps - 1)
    mem = pl.BlockSpec((G, N_MEM, D_MODEL), lambda t: (a_tile(t) // tiles_per_mem, 0, 0))
    vec = _resident((1, D_MODEL))
    kern = functools.partial(_tail_kernel, G=G, R=R)
    return pl.pallas_call(
        kern,
        grid=(n_steps + 1,),
        in_specs=[pl.BlockSpec((tm, D_MODEL), lambda t: (a_tile(t), 0)), mem, mem, vec, vec, vec,
                  _resident(wcq.shape), _resident(wco.shape), _resident(wg.shape), _resident(wu.shape),
                  _resident(wd.shape)],
        out_specs=pl.BlockSpec((tm, D_MODEL), lambda t: (jnp.maximum(t - 1, 0), 0)),
        out_shape=jax.ShapeDtypeStruct((n, D_MODEL), F32),
        scratch_shapes=[pltpu.VMEM((tm, D_MODEL), BF16), pltpu.VMEM((tm, D_MODEL), BF16),
                        pltpu.VMEM((tm, D_MODEL), F32), pltpu.VMEM((tm, D_MODEL), BF16),
                        pltpu.VMEM((tm, D_MODEL), F32), pltpu.VMEM((tm, D_MODEL), BF16),
                        pltpu.VMEM((tm, D_FF), BF16), pltpu.VMEM((tm, D_MODEL), F32)],
        compiler_params=pltpu.CompilerParams(dimension_semantics=("arbitrary",), vmem_limit_bytes=VMEM_LIMIT),
        name="tail",
    )(h2d, mk, mv, gc, gf, gl, wcq, wco, wg, wu, wd)


def _mem_kv_kernel(m_ref, g_ref, wk_ref, wv_ref, k_ref, v_ref, kb_ref, vb_ref):
    mn = (_rms(m_ref[...]) * g_ref[...]).astype(BF16)
    k = jnp.dot(mn, wk_ref[...], preferred_element_type=F32)
    v = jnp.dot(mn, wv_ref[...], preferred_element_type=F32)
    for hd in range(MEM_HEADS):
        k_ref[:, hd, :] = k[:, hd * MEM_HD:(hd + 1) * MEM_HD]
        v_ref[:, hd, :] = v[:, hd * MEM_HD:(hd + 1) * MEM_HD]
    kb_ref[...] = k.astype(BF16)
    vb_ref[...] = v.astype(BF16)


def _mem_kv(mem2d, g, wk, wv, tm):
    n = mem2d.shape[0]
    assert n % tm == 0
    blk = pl.BlockSpec((tm, D_MODEL), lambda i: (i, 0))
    blk4 = pl.BlockSpec((tm, MEM_HEADS, MEM_HD), lambda i: (i, 0, 0))
    return pl.pallas_call(
        _mem_kv_kernel,
        grid=(n // tm,),
        in_specs=[blk, _resident((1, D_MODEL)), _resident(wk.shape), _resident(wv.shape)],
        out_specs=[blk4, blk4, blk, blk],
        out_shape=[jax.ShapeDtypeStruct((n, MEM_HEADS, MEM_HD), F32)] * 2
                  + [jax.ShapeDtypeStruct((n, D_MODEL), BF16)] * 2,
        compiler_params=pltpu.CompilerParams(dimension_semantics=("parallel",), vmem_limit_bytes=VMEM_LIMIT),
        name="mem_kv",
    )(mem2d, g, wk, wv)


def _log_gamma():
    return np.log(np.float32(1.0) - np.float32(2.0) ** (np.float32(-5.0) - np.arange(RET_HEADS, dtype=np.float32)))


def _rope_tables(pos):
    half = RET_DK // 2
    inv = np.float32(ROPE_BASE) ** (-np.arange(half, dtype=np.float32) / np.float32(half))
    ang = pos.astype(np.float32)[:, None] * inv[None, :]
    return np.cos(ang).astype(np.float32), np.sin(ang).astype(np.float32)


def _retention_tables(R):
    lg = _log_gamma()[:, None, None]
    idx = np.arange(R, dtype=np.float32)
    diff = idx[:, None] - idx[None, :]
    cn = (np.arange(R) // CHUNK)[:, None]
    cm = (np.arange(R) // CHUNK)[None, :]
    mask = np.where(cm == cn, np.exp(lg * np.abs(diff)), np.where(cm < cn, np.exp(lg * diff), np.float32(0.0)))
    qd = np.exp(_log_gamma()[:, None] * (idx[None, :] + np.float32(1.0)))
    kd = np.exp(_log_gamma()[:, None] * (np.float32(R - 1.0) - idx[None, :]))
    gr = np.exp(_log_gamma() * np.float32(R))
    qd_full = np.broadcast_to(qd[:, :, None], (RET_HEADS, R, LANES))
    kd_full = np.broadcast_to(kd.T[:, :, None], (R, RET_HEADS, LANES)).reshape(R, RET_HEADS * LANES)
    f32 = lambda a: jnp.asarray(np.ascontiguousarray(a, dtype=np.float32))
    return f32(mask), f32(qd_full), f32(kd_full), f32(gr)


def _rel_bucket():
    i = jnp.arange(CHUNK, dtype=jnp.int32)[:, None]
    j = jnp.arange(SWA_L, dtype=jnp.int32)[None, :]
    rel = (j - WINDOW) - i
    half = REL_BUCKETS // 2
    max_exact = half // 2
    n = jnp.abs(rel)
    large = max_exact + (jnp.log(jnp.maximum(n, 1).astype(F32) / max_exact)
                         / math.log(REL_MAX_DIST / max_exact) * (half - max_exact)).astype(jnp.int32)
    large = jnp.minimum(large, half - 1)
    return jnp.where(rel > 0, half, 0) + jnp.where(n < max_exact, n, large)


def _tile_rows(t, tm):
    reps = -(-tm // t.shape[0])
    return jnp.asarray(np.tile(t, (reps, 1)) if reps > 1 else t)


def kernel(x_prompt, x_sample, cache_ret_state, cache_swa_k, cache_swa_v, cache_mem_k, cache_mem_v, mem_prompt,
           rel_bias, g_attn, w_in, w_ret_out, w_swa_out, w_mix_out, swa_sinks, g_cross, g_mem, w_cq, w_mk, w_mv,
           w_co, g_ffn, w_gate, w_up, w_down, g_final):
    B, S, D = x_prompt.shape
    Bs, T, _ = x_sample.shape
    assert D == D_MODEL and T == CHUNK and S % R_PROMPT == 0 and cache_swa_k.shape[2] == WINDOW
    assert g_attn.shape[0] == 1, "single layer"
    bf = lambda w: w.astype(BF16)
    vec = lambda g: g.reshape(1, D_MODEL)
    w_in_b, wro, wso, wmo = bf(w_in[0]), bf(w_ret_out[0]), bf(w_swa_out[0]), bf(w_mix_out[0])
    wcq, wco, wmk, wmv = bf(w_cq[0]), bf(w_co[0]), bf(w_mk[0]), bf(w_mv[0])
    wg, wu, wd = bf(w_gate[0]), bf(w_up[0]), bf(w_down[0])

    key_of_col, sub_of_col = _swa_col_maps()
    bucket2 = _rel_bucket()[:, key_of_col]
    bias2 = _rel_bias(rel_bias, bucket2, jnp.asarray(sub_of_col)[None, :])
    sinks = swa_sinks[0]

    def layer(x, pos, R, sub, G, st0, hist, mk, mv, has_history):
        nb, seq, _ = x.shape
        x2d = x.reshape(nb * seq, D_MODEL)
        tm = G * sub * R
        cos_t, sin_t = _rope_tables(pos)
        mt, qd, kd, gr = _retention_tables(R)
        h1, st, kv = _front(x2d, vec(g_attn[0]), w_in_b, _tile_rows(cos_t, tm), _tile_rows(sin_t, tm),
                            kd, st0, hist, mt, qd, gr, bias2, sinks, wro, wso, wmo,
                            G=G, R=R, S=sub, n_tiles=seq // (sub * R), has_history=has_history)
        r_tail = TAIL_ROWS if (G == 1 and seq % TAIL_ROWS == 0) else R
        y = _tail(h1, mk, mv, vec(g_cross[0]), vec(g_ffn[0]), vec(g_final), wcq, wco, wg, wu, wd,
                  G=G, R=r_tail, tiles_per_mem=seq // r_tail)
        return y.reshape(nb, seq, D_MODEL), st, kv

    mk_f, mv_f, mk_b, mv_b = _mem_kv(mem_prompt.reshape(B * N_MEM, D_MODEL), vec(g_mem[0]), wmk, wmv, 512)
    y_p, st_p, kv_p = layer(
        x_prompt, np.arange(S), R_PROMPT, FRONT_SUBTILES if S % (FRONT_SUBTILES * R_PROMPT) == 0 else 1, 1, None, None,
        mk_b.reshape(B, N_MEM, D_MODEL), mv_b.reshape(B, N_MEM, D_MODEL), False)

    Gs = 2 if Bs % 2 == 0 else 1
    hist_s = jnp.concatenate([cache_swa_k[0].reshape(Bs, WINDOW, LANES), cache_swa_v[0].reshape(Bs, WINDOW, LANES)],
                             axis=-1)
    y_s, st_s, kv_s = layer(
        x_sample, PAST_LEN + np.arange(T), CHUNK, 1, Gs,
        cache_ret_state[0].astype(F32), hist_s,
        bf(cache_mem_k[0].reshape(Bs, N_MEM, D_MODEL)), bf(cache_mem_v[0].reshape(Bs, N_MEM, D_MODEL)), True)

    kvshape = (1, B, WINDOW, SWA_KV, SWA_HD)
    k_p = kv_p[:, :, :LANES].reshape(kvshape)
    v_p = kv_p[:, :, LANES:].reshape(kvshape)
    k_s = jnp.concatenate([cache_swa_k[0][:, T:], kv_s[:, :, :LANES].reshape(Bs, T, SWA_KV, SWA_HD)], axis=1)[None]
    v_s = jnp.concatenate([cache_swa_v[0][:, T:], kv_s[:, :, LANES:].reshape(Bs, T, SWA_KV, SWA_HD)], axis=1)[None]
    mem_shape = (1, B, N_MEM, MEM_HEADS, MEM_HD)
    return (y_p, y_s, st_p[None], st_s[None], k_p, k_s, v_p, v_s, mk_f.reshape(mem_shape), mv_f.reshape(mem_shape))
```

```python
import functools
import math

import numpy as np
import jax
import jax.numpy as jnp
from jax import lax
from jax.experimental import pallas as pl
from jax.experimental.pallas import tpu as pltpu

F32 = jnp.float32
BF16 = jnp.bfloat16

D_MODEL = 1024
CHUNK = 64
EPS = 1e-6
NEG_INF = -1e30
PAST_LEN = 4096

RET_HEADS = 4
RET_DK = 256
RET_DV = 256
ROPE_BASE = 10000.0

SWA_HEADS = 16
SWA_KV = 2
SWA_GROUP = 8
SWA_HD = 64
WINDOW = 128
SWA_L = WINDOW + CHUNK

REL_BUCKETS = 32
REL_MAX_DIST = 128

N_MEM = 256
MEM_HEADS = 4
MEM_HD = 256
D_FF = 2816

OFF_RQ, OFF_RK, OFF_RV, OFF_RG, OFF_SQ, OFF_SKV, OFF_GA, OFF_GB = 0, 1024, 2048, 3072, 4096, 5120, 5376, 6400
D_IN = 7424

LANES = 128
HALF = LANES // 2
VMEM_LIMIT = 58 * 1024 * 1024

FF_COLS = 256
MERGE_COLS = 256
GATE_COLS = 512
KV_COLS = 2 * SWA_KV * SWA_HD
PIPE_DEPTH = 3
TAIL_ROWS = 512
FRONT_SUBTILES = 2
R_PROMPT = 256


def _resident(shape):
    nd = len(shape)
    return pl.BlockSpec(shape, lambda *_: (0,) * nd, pipeline_mode=pl.Buffered(1))


def _rms(x):
    return x * lax.rsqrt(jnp.mean(x * x, axis=-1, keepdims=True) + EPS)


def _interleave(primary, secondary):
    out, j = [], 0
    for k, piece in enumerate(primary):
        out.append(piece)
        while j < len(secondary) and (j + 1) * len(primary) <= (k + 1) * len(secondary):
            out.append(secondary[j])
            j += 1
    return out + secondary[j:]


def _swa_col_maps():
    key = np.concatenate([np.arange(LANES), np.arange(LANES), LANES + np.arange(HALF), LANES + np.arange(HALF)])
    sub = np.concatenate([np.zeros(LANES), np.ones(LANES), np.zeros(HALF), np.ones(HALF)]).astype(np.int32)
    return key.astype(np.int32), sub


def _rel_bias_kernel(tab_ref, bucket_ref, sub_ref, out_ref):
    bucket = bucket_ref[...]
    is_b = sub_ref[...] > 0
    for hk in range(SWA_KV):
        for gp in range(SWA_GROUP // 2):
            h0 = hk * SWA_GROUP + gp * 2
            acc = jnp.zeros(bucket.shape, F32)
            for b in range(REL_BUCKETS):
                val = jnp.where(is_b, tab_ref[b, h0 + 1], tab_ref[b, h0])
                acc = jnp.where(bucket == b, val, acc)
            out_ref[hk, gp * CHUNK:(gp + 1) * CHUNK, :] = acc


def _rel_bias(table, bucket2, sub2):
    ncol = bucket2.shape[1]
    return pl.pallas_call(
        _rel_bias_kernel,
        in_specs=[pl.BlockSpec(memory_space=pltpu.SMEM),
                  pl.BlockSpec(memory_space=pltpu.VMEM),
                  pl.BlockSpec(memory_space=pltpu.VMEM)],
        out_specs=pl.BlockSpec(memory_space=pltpu.VMEM),
        out_shape=jax.ShapeDtypeStruct((SWA_KV, 4 * CHUNK, ncol), F32),
        name="rel_bias",
    )(table, bucket2, sub2)


def _front_kernel(x_ref, xp_ref, g_ref, w_ref, cos_ref, sin_ref, kdec_ref,
                  st0_ref, hist_ref, mt_ref, qd_ref, gr_ref, bias_ref, sink_ref,
                  wro_ref, wso_ref, wmo_ref,
                  h_ref, st_ref, kvl_ref,
                  xn, rq, rk, rkd, rv, rgs, sq, gas, gbs, state, ka, kb, va, vb, ain, swo, mixs,
                  *, G, R, S, n_tiles, has_history):
    Rg = S * R
    t = pl.program_id(0)
    n_real = pl.num_programs(0) - 1
    i = lax.rem(jnp.minimum(t, n_real - 1), n_tiles)
    lane =lax.broadcasted_iota(jnp.int32, (1, LANES), 1)
    lo = lane < HALF

    def split_heads(x):
        r = pltpu.roll(x, HALF, axis=1)
        z = jnp.zeros_like(x)
        a0 = jnp.where(lo, x, z).astype(BF16)
        b0 = jnp.where(lo, z, r).astype(BF16)
        a1 = jnp.where(lo, r, z).astype(BF16)
        b1 = jnp.where(lo, z, x).astype(BF16)
        return (a0, a1), (b0, b1)

    @pl.when(t == 0)
    def _():
        ain[...] = jnp.zeros(ain.shape, BF16)
        swo[...] = jnp.zeros(swo.shape, BF16)
        gas[...] = jnp.zeros(gas.shape, BF16)
        gbs[...] = jnp.zeros(gbs.shape, BF16)

    @pl.when(i == 0)
    def _():
        state[...] = st0_ref[...] if has_history else jnp.zeros(state.shape, F32)
        for g in range(G):
            hist = hist_ref[g] if has_history else jnp.zeros((WINDOW, KV_COLS), F32)
            tail = slice(Rg, Rg + WINDOW)
            (a0, a1), (b0, b1) = split_heads(hist[:, :LANES])
            ka[g, 0, tail], ka[g, 1, tail], kb[g, 0, tail], kb[g, 1, tail] = a0, a1, b0, b1
            (a0, a1), (b0, b1) = split_heads(hist[:, LANES:])
            va[g, 0, tail], va[g, 1, tail], vb[g, 0, tail], vb[g, 1, tail] = a0, a1, b0, b1

    def norm():
        xn[...] = (_rms(x_ref[...]) * g_ref[...]).astype(BF16)

    def proj(c0, n):
        return jnp.dot(xn[...], w_ref[:, c0:c0 + n], preferred_element_type=F32)

    def proj_head(h):
        c = h * RET_DK
        cos = cos_ref[...]
        sin = sin_ref[...]
        acc = proj(OFF_RQ + c, RET_DK)
        x1, x2 = acc[:, :LANES], acc[:, LANES:]
        rq[:, c:c + LANES] = (x1 * cos - x2 * sin).astype(BF16)
        rq[:, c + LANES:c + RET_DK] = (x2 * cos + x1 * sin).astype(BF16)
        acc = proj(OFF_RK + c, RET_DK)
        x1, x2 = acc[:, :LANES], acc[:, LANES:]
        k1 = (x1 * cos - x2 * sin) * (RET_DK ** -0.5)
        k2 = (x2 * cos + x1 * sin) * (RET_DK ** -0.5)
        rk[:, c:c + LANES] = k1.astype(BF16)
        rk[:, c + LANES:c + RET_DK] = k2.astype(BF16)
        kd = jnp.concatenate([kdec_ref[:, h * LANES:(h + 1) * LANES]] * (G * S), axis=0)
        rkd[:, c:c + LANES] = (k1 * kd).astype(BF16)
        rkd[:, c + LANES:c + RET_DK] = (k2 * kd).astype(BF16)
        rv[:, c:c + RET_DV] = proj(OFF_RV + c, RET_DV).astype(BF16)
        rgs[:, c:c + RET_DV] = jax.nn.silu(proj(OFF_RG + c, RET_DV)).astype(BF16)

    def proj_sq(j):
        c = j * GATE_COLS
        sq[:, c:c + GATE_COLS] = proj(OFF_SQ + c, GATE_COLS).astype(BF16)

    def proj_kv():
        half = (G * Rg) // 2
        w_kv = w_ref[:, OFF_SKV:OFF_SKV + KV_COLS]
        kvt_all = jnp.concatenate([jnp.dot(xn[:half], w_kv, preferred_element_type=F32),
                                   jnp.dot(xn[half:], w_kv, preferred_element_type=F32)], axis=0)
        for g in range(G):
            kvl_ref[g] = kvt_all[(g + 1) * Rg - min(Rg, WINDOW):(g + 1) * Rg]
        for g in range(G):
            kvt = kvt_all[g * Rg:(g + 1) * Rg]
            for buf in (ka, kb, va, vb):
                for hk in range(SWA_KV):
                    buf[g, hk, :WINDOW] = buf[g, hk, Rg:Rg + WINDOW]
            (a0, a1), (b0, b1) = split_heads(kvt[:, :LANES])
            ka[g, 0, WINDOW:], ka[g, 1, WINDOW:], kb[g, 0, WINDOW:], kb[g, 1, WINDOW:] = a0, a1, b0, b1
            (a0, a1), (b0, b1) = split_heads(kvt[:, LANES:])
            va[g, 0, WINDOW:], va[g, 1, WINDOW:], vb[g, 0, WINDOW:], vb[g, 1, WINDOW:] = a0, a1, b0, b1

    def proj_gate(dst, off, j):
        c = j * GATE_COLS
        dst[:, c:c + GATE_COLS] = jax.nn.sigmoid(proj(off + c, GATE_COLS)).astype(BF16)

    def ret_front(g, sub, h):
        rows = slice(g * Rg + sub * R, g * Rg + (sub + 1) * R)
        cols = slice(h * RET_DK, (h + 1) * RET_DK)
        q = rq[rows, cols]
        v = rv[rows, cols]
        s_prev = state[g, h]
        sc = lax.dot_general(q, rk[rows, cols], (((1,), (1,)), ((), ())), preferred_element_type=F32)
        qs = jnp.dot(q, s_prev.astype(BF16), preferred_element_type=F32)
        kv_new = lax.dot_general(rkd[rows, cols], v, (((0,), (0,)), ((), ())), preferred_element_type=F32)
        return sc, qs, kv_new, s_prev, v

    def ret_back(g, sub, h, vals):
        sc, qs, kv_new, s_prev, v = vals
        rows = slice(g * Rg + sub * R, g * Rg + (sub + 1) * R)
        cols = slice(h * RET_DK, (h + 1) * RET_DK)
        qd = jnp.concatenate([qd_ref[h]] * (RET_DV // LANES), axis=1)
        o = jnp.dot((sc * mt_ref[h]).astype(BF16), v, preferred_element_type=F32) + qs * qd
        state[g, h] = s_prev * gr_ref[h] + kv_new
        ain[rows, cols] =(_rms(o) * rgs[rows, cols].astype(F32)).astype(BF16)

    ones_a = jnp.broadcast_to(jnp.where(lo, 1.0, 0.0).astype(BF16), (LANES, LANES))
    ones_b = jnp.broadcast_to(jnp.where(lo, 0.0, 1.0).astype(BF16), (LANES, LANES))
    lane3 = lax.broadcasted_iota(jnp.int32, (1, 3 * LANES), 1)
    key_of_col = jnp.where(lane3 < 2 * LANES, lane3 % LANES, LANES + lane3 % HALF)

    def swa_front(g, c, hk):
        w0 = c * CHUNK
        r0 = g * Rg + c * CHUNK
        q2 = jnp.concatenate(
            [sq[r0:r0 + CHUNK, hk * 512 + gp * LANES: hk * 512 + (gp + 1) * LANES] for gp in range(4)],
            axis=0)
        kbd = jnp.concatenate([ka[g, hk, w0:w0 + LANES], kb[g, hk, w0:w0 + LANES],
                               ka[g, hk, w0 + LANES:w0 + SWA_L], kb[g, hk, w0 + LANES:w0 + SWA_L]],
                              axis=0)
        return lax.dot_general(q2, kbd, (((1,), (1,)), ((), ())), preferred_element_type=F32)

    def swa_back(g, c, hk, s):
        w0 = c * CHUNK
        r0 = g * Rg + c * CHUNK
        s = s * (SWA_HD ** -0.5) + bias_ref[hk]
        if not has_history and c < WINDOW // CHUNK:
            kpos = key_of_col + (i * Rg + c * CHUNK - WINDOW)
            s = jnp.where(kpos >= 0, s, NEG_INF)
        t0, t1, t2 = s[:, :LANES], s[:, LANES:2 * LANES], s[:, 2 * LANES:]
        m_a = jnp.max(jnp.maximum(t0, jnp.where(lo, t2, NEG_INF)), axis=-1, keepdims=True)
        m_b = jnp.max(jnp.maximum(t1, jnp.where(lo, NEG_INF, t2)), axis=-1, keepdims=True)
        sk_a = jnp.concatenate([jnp.full((CHUNK, LANES), sink_ref[hk * SWA_GROUP + 2 * gp], F32)
                                for gp in range(4)], axis=0)
        sk_b = jnp.concatenate([jnp.full((CHUNK, LANES), sink_ref[hk * SWA_GROUP + 2 * gp + 1], F32)
                                for gp in range(4)], axis=0)
        m_a = jnp.maximum(jnp.broadcast_to(m_a, sk_a.shape), sk_a)
        m_b = jnp.maximum(jnp.broadcast_to(m_b, sk_b.shape), sk_b)
        m_ab = jnp.where(lo, m_a, m_b)
        e = jnp.concatenate([jnp.exp(t0 - m_a), jnp.exp(t1 - m_b), jnp.exp(t2 - m_ab)], axis=1).astype(BF16)
        vbd = jnp.concatenate([
            jnp.concatenate([va[g, hk, w0:w0 + LANES], ones_a], axis=1),
            jnp.concatenate([vb[g, hk, w0:w0 + LANES], ones_b], axis=1),
            jnp.concatenate([va[g, hk, w0 + LANES:w0 + SWA_L], ones_a[:HALF]], axis=1),
            jnp.concatenate([vb[g, hk, w0 + LANES:w0 + SWA_L], ones_b[:HALF]], axis=1)], axis=0)
        oa = jnp.dot(e, vbd, preferred_element_type=F32)
        den = oa[:, LANES:] + jnp.exp(jnp.where(lo, sk_a, sk_b) - m_ab)
        o2 = (oa[:, :LANES] / den).astype(BF16)
        for gp in range(4):
            swo[r0:r0 + CHUNK, hk * 512 + gp * LANES: hk * 512 + (gp + 1) * LANES] = o2[gp * CHUNK:(gp + 1) * CHUNK]

    def merge_ab(j):
        cols = slice(j * MERGE_COLS, (j + 1) * MERGE_COLS)
        a = jnp.dot(ain[...], wro_ref[:, cols], preferred_element_type=F32)
        b = jnp.dot(swo[...], wso_ref[:, cols], preferred_element_type=F32)
        mixs[:, cols] = (gas[:, cols].astype(F32) * a + gbs[:, cols].astype(F32) * b).astype(BF16)

    def merge_y(j):
        cols = slice(j * MERGE_COLS, (j + 1) * MERGE_COLS)
        h_ref[:, cols] = xp_ref[:, cols] + jnp.dot(mixs[...], wmo_ref[:, cols], preferred_element_type=F32)

    n_mc = D_MODEL // MERGE_COLS
    n_gc = D_MODEL // GATE_COLS
    dense = [(functools.partial(merge_ab, 0), None), (norm, None), (functools.partial(merge_ab, 1), None),
             (functools.partial(proj_head, 0), ("head", 0))]
    dense += [(functools.partial(merge_ab, j), ("merged", 0) if j == n_mc - 1 else None) for j in range(2, n_mc)]
    dense += [(functools.partial(proj_head, h), ("head", h)) for h in range(1, RET_HEADS)]
    dense += [(functools.partial(proj_sq, j), None) for j in range(n_gc)]
    dense.append((proj_kv, ("swa", 0)))
    dense += [(functools.partial(proj_gate, gas, OFF_GA, j), None) for j in range(n_gc)]
    dense += [(functools.partial(proj_gate, gbs, OFF_GB, j), None) for j in range(n_gc)]
    dense += [(functools.partial(merge_y, j), None) for j in range(n_mc)]

    assert S == 1 or PIPE_DEPTH < RET_HEADS * G
    items = [(ret_front, ret_back, (g, sub, h), ("head", h))
             for sub in range(S) for h in range(RET_HEADS) for g in range(G)]
    items += [(swa_front, swa_back, (g, c, hk), ("swa", 0))
              for g in range(G) for c in range(Rg // CHUNK) for hk in range(SWA_KV)]
    tasks = []
    for n in range(len(items) + PIPE_DEPTH):
        if n < len(items):
            tasks.append(("front", n))
        if n >= PIPE_DEPTH:
            tasks.append(("back", n - PIPE_DEPTH))

    pending, done, nxt = {}, set(), 0

    def run_task(kind, n):
        front, back, args, _ = items[n]
        if kind == "front":
            pending[n] = front(*args)
        else:
            back(*args, pending.pop(n))

    for k, (piece, tag) in enumerate(dense):
        piece()
        if tag is not None:
            done.add(tag)
        quota = -(-(len(tasks) - nxt) // (len(dense) - k))
        while nxt < len(tasks) and quota > 0:
            kind, n = tasks[nxt]
            if kind == "front" and items[n][3] not in done:
                break
            if kind == "back" and ("merged", 0) not in done:
                break
            run_task(kind, n)
            nxt += 1
            quota -= 1
    assert nxt == len(tasks)

    @pl.when((i == n_tiles - 1) & (t < n_real))
    def _():
        st_ref[...] = state[...]


def _front(x2d, g, w_in, cos_t, sin_t, kdec_t, st0, hist, mt, qd, gr, bias2, sinks, wro, wso, wmo,
           *, G, R, S, n_tiles, has_history):
    n = x2d.shape[0]
    Rg = S * R
    tm = G * Rg
    n_steps = n // tm
    n_rows = n // (Rg * n_tiles)
    keep = min(Rg, WINDOW)
    assert n_steps * tm == n and n_steps % n_tiles == 0 and (G == 1 or n_tiles == 1)
    assert cos_t.shape == (n_tiles * tm, LANES) and kdec_t.shape == (R, RET_HEADS * LANES)
    cur = lambda t: jnp.minimum(t, n_steps - 1)
    prev = lambda t: jnp.maximum(t - 1, 0)
    per_row = pl.BlockSpec((G, RET_HEADS, RET_DK, RET_DV), lambda t: (cur(t) // n_tiles, 0, 0, 0))
    rope = pl.BlockSpec((tm, LANES), lambda t: (cur(t) % n_tiles, 0))
    kw = dict(G=G, R=R, S=S, n_tiles=n_tiles, has_history=has_history)
    if has_history:
        kern = functools.partial(_front_kernel, **kw)
        carried = [st0, hist]
        carried_specs = [per_row, pl.BlockSpec((G, WINDOW, KV_COLS), lambda t: (cur(t) // n_tiles, 0, 0))]
    else:
        def kern(x_ref, xp_ref, g_ref, w_ref, cos_ref, sin_ref, kdec_ref, *rest):
            _front_kernel(x_ref, xp_ref, g_ref, w_ref, cos_ref, sin_ref, kdec_ref, None, None, *rest, **kw)
        carried, carried_specs = [], []
    tile_bf = pltpu.VMEM((tm, D_MODEL), BF16)
    return pl.pallas_call(
        kern,
        grid=(n_steps + 1,),
        in_specs=[pl.BlockSpec((tm, D_MODEL), lambda t: (cur(t), 0)),
                  pl.BlockSpec((tm, D_MODEL), lambda t: (prev(t), 0)),
                  _resident((1, D_MODEL)), _resident(w_in.shape), rope, rope, _resident(kdec_t.shape)]
                 + carried_specs
                 + [_resident(mt.shape), _resident(qd.shape),
                  pl.BlockSpec(memory_space=pltpu.SMEM),
                  _resident(bias2.shape),
                  pl.BlockSpec(memory_space=pltpu.SMEM),
                  _resident(wro.shape), _resident(wso.shape), _resident(wmo.shape)],
        out_specs=[pl.BlockSpec((tm, D_MODEL), lambda t: (prev(t), 0)), per_row,
                   pl.BlockSpec((G, keep, KV_COLS), lambda t: (cur(t) // n_tiles, 0, 0))],
        out_shape=[jax.ShapeDtypeStruct((n, D_MODEL), F32),
                   jax.ShapeDtypeStruct((n_rows, RET_HEADS, RET_DK, RET_DV), F32),
                   jax.ShapeDtypeStruct((n_rows, keep, KV_COLS), F32)],
        scratch_shapes=[tile_bf] * 9
                       + [pltpu.VMEM((G, RET_HEADS, RET_DK, RET_DV), F32)]
                       + [pltpu.VMEM((G, SWA_KV, WINDOW + Rg, LANES), BF16)] * 4
                       + [tile_bf] * 3,
        compiler_params=pltpu.CompilerParams(dimension_semantics=("arbitrary",), vmem_limit_bytes=VMEM_LIMIT),
        name="front",
    )(x2d, x2d, g, w_in, cos_t, sin_t, kdec_t, *carried, mt, qd, gr, bias2, sinks, wro, wso, wmo)


def _tail_kernel(h_ref, mk_ref, mv_ref, gc_ref, gf_ref, gl_ref, wcq_ref, wco_ref, wg_ref, wu_ref, wd_ref,
                 y_ref, qs, att, h2s, hn2s, h2_prev, hn2_prev, acts, h3s, *, G, R):
    @pl.when(pl.program_id(0) == 0)
    def _():
        h2s[...] = jnp.zeros(h2s.shape, F32)
        hn2s[...] = jnp.zeros(hn2s.shape, BF16)

    h2_prev[...] = h2s[...]
    hn2_prev[...] = hn2s[...]

    def q_proj():
        hn = (_rms(h_ref[...]) * gc_ref[...]).astype(BF16)
        qs[...] = jnp.dot(hn, wcq_ref[...], preferred_element_type=F32).astype(BF16)

    def head_front(g, hd):
        rows = slice(g * R, (g + 1) * R)
        cols = slice(hd * MEM_HD, (hd + 1) * MEM_HD)
        return lax.dot_general(qs[rows, cols], mk_ref[g, :, cols], (((1,), (1,)), ((), ())),
                               preferred_element_type=F32)

    def head_back(g, hd, s):
        rows = slice(g * R, (g + 1) * R)
        cols = slice(hd * MEM_HD, (hd + 1) * MEM_HD)
        s = s * (MEM_HD ** -0.5)
        e = jnp.exp(s - jnp.max(s, axis=-1, keepdims=True))
        den = jnp.sum(e, axis=-1, keepdims=True)
        o = jnp.dot(e.astype(BF16), mv_ref[g, :, cols], preferred_element_type=F32) / den
        att[rows, cols] = o.astype(BF16)

    h2_sq = []

    def attn_out(j):
        cols = slice(j * GATE_COLS, (j + 1) * GATE_COLS)
        h2 = h_ref[:, cols] + jnp.dot(att[...], wco_ref[:, cols], preferred_element_type=F32)
        h2s[:, cols] = h2
        h2_sq.append(jnp.sum(h2 * h2, axis=-1, keepdims=True))

    def attn_norm():
        mean_sq = functools.reduce(lambda a, b: a + b, h2_sq) * (1.0 / D_MODEL)
        hn2s[...] = (h2s[...] * lax.rsqrt(mean_sq + EPS) * gf_ref[...]).astype(BF16)

    def ff_chunk(j):
        cols = slice(j * FF_COLS, (j + 1) * FF_COLS)
        hn = hn2_prev[...]
        gate = jnp.dot(hn, wg_ref[:, cols], preferred_element_type=F32)
        up = jnp.dot(hn, wu_ref[:, cols], preferred_element_type=F32)
        acts[:, cols] = (jax.nn.silu(gate) * up).astype(BF16)

    sq_sums = []

    def down_chunk(j):
        cols = slice(j * FF_COLS, (j + 1) * FF_COLS)
        h3 = h2_prev[:, cols] + jnp.dot(acts[...], wd_ref[:, cols], preferred_element_type=F32)
        h3s[:, cols] = h3
        sq_sums.append(jnp.sum(h3 * h3, axis=-1, keepdims=True))

    def final_norm():
        mean_sq = functools.reduce(lambda a, b: a + b, sq_sums) * (1.0 / D_MODEL)
        y_ref[...] = h3s[...] * lax.rsqrt(mean_sq + EPS) * gl_ref[...]

    heads = [(g, hd) for g in range(G) for hd in range(MEM_HEADS)]
    pending = {}

    def front(n):
        pending[n] = head_front(*heads[n])

    def back(n):
        head_back(*heads[n], pending.pop(n))

    attn = [q_proj]
    for n in range(len(heads) + 1):
        if n < len(heads):
            attn.append(functools.partial(front, n))
        if n >= 1:
            attn.append(functools.partial(back, n - 1))
    attn += [functools.partial(attn_out, j) for j in range(D_MODEL // GATE_COLS)] + [attn_norm]
    ffn = [functools.partial(ff_chunk, j) for j in range(D_FF // FF_COLS)]
    down = [functools.partial(down_chunk, j) for j in range(D_MODEL // FF_COLS)]
    for piece in _interleave(ffn, attn) + down + [final_norm]:
        piece()


def _tail(h2d, mk, mv, gc, gf, gl, wcq, wco, wg, wu, wd, *, G, R, tiles_per_mem):
    n = h2d.shape[0]
    tm = G * R
    n_steps = n // tm
    assert n_steps * tm == n and (G == 1 or tiles_per_mem == 1)
    a_tile = lambda t: jnp.minimum(t, n_steps - 1)
    mem = pl.BlockSpec((G, N_MEM, D_MODEL), lambda t: (a_tile(t) // tiles_per_mem, 0, 0))
    vec = _resident((1, D_MODEL))
    kern = functools.partial(_tail_kernel, G=G, R=R)
    return pl.pallas_call(
        kern,
        grid=(n_steps + 1,),
        in_specs=[pl.BlockSpec((tm, D_MODEL), lambda t: (a_tile(t), 0)), mem, mem, vec, vec, vec,
                  _resident(wcq.shape), _resident(wco.shape), _resident(wg.shape), _resident(wu.shape),
                  _resident(wd.shape)],
        out_specs=pl.BlockSpec((tm, D_MODEL), lambda t: (jnp.maximum(t - 1, 0), 0)),
        out_shape=jax.ShapeDtypeStruct((n, D_MODEL), F32),
        scratch_shapes=[pltpu.VMEM((tm, D_MODEL), BF16), pltpu.VMEM((tm, D_MODEL), BF16),
                        pltpu.VMEM((tm, D_MODEL), F32), pltpu.VMEM((tm, D_MODEL), BF16),
                        pltpu.VMEM((tm, D_MODEL), F32), pltpu.VMEM((tm, D_MODEL), BF16),
                        pltpu.VMEM((tm, D_FF), BF16), pltpu.VMEM((tm, D_MODEL), F32)],
        compiler_params=pltpu.CompilerParams(dimension_semantics=("arbitrary",), vmem_limit_bytes=VMEM_LIMIT),
        name="tail",
    )(h2d, mk, mv, gc, gf, gl, wcq, wco, wg, wu, wd)


def _mem_kv_kernel(m_ref, g_ref, wk_ref, wv_ref, k_ref, v_ref, kb_ref, vb_ref):
    mn = (_rms(m_ref[...]) * g_ref[...]).astype(BF16)
    k = jnp.dot(mn, wk_ref[...], preferred_element_type=F32)
    v = jnp.dot(mn, wv_ref[...], preferred_element_type=F32)
    for hd in range(MEM_HEADS):
        k_ref[:, hd, :] = k[:, hd * MEM_HD:(hd + 1) * MEM_HD]
        v_ref[:, hd, :] = v[:, hd * MEM_HD:(hd + 1) * MEM_HD]
    kb_ref[...] = k.astype(BF16)
    vb_ref[...] = v.astype(BF16)


def _mem_kv(mem2d, g, wk, wv, tm):
    n = mem2d.shape[0]
    assert n % tm == 0
    blk = pl.BlockSpec((tm, D_MODEL), lambda i: (i, 0))
    blk4 = pl.BlockSpec((tm, MEM_HEADS, MEM_HD), lambda i: (i, 0, 0))
    return pl.pallas_call(
        _mem_kv_kernel,
        grid=(n // tm,),
        in_specs=[blk, _resident((1, D_MODEL)), _resident(wk.shape), _resident(wv.shape)],
        out_specs=[blk4, blk4, blk, blk],
        out_shape=[jax.ShapeDtypeStruct((n, MEM_HEADS, MEM_HD), F32)] * 2
                  + [jax.ShapeDtypeStruct((n, D_MODEL), BF16)] * 2,
        compiler_params=pltpu.CompilerParams(dimension_semantics=("parallel",), vmem_limit_bytes=VMEM_LIMIT),
        name="mem_kv",
    )(mem2d, g, wk, wv)


def _log_gamma():
    return np.log(np.float32(1.0) - np.float32(2.0) ** (np.float32(-5.0) - np.arange(RET_HEADS, dtype=np.float32)))


def _rope_tables(pos):
    half = RET_DK // 2
    inv = np.float32(ROPE_BASE) ** (-np.arange(half, dtype=np.float32) / np.float32(half))
    ang = pos.astype(np.float32)[:, None] * inv[None, :]
    return np.cos(ang).astype(np.float32), np.sin(ang).astype(np.float32)


def _retention_tables(R):
    lg = _log_gamma()[:, None, None]
    idx = np.arange(R, dtype=np.float32)
    diff = idx[:, None] - idx[None, :]
    cn = (np.arange(R) // CHUNK)[:, None]
    cm = (np.arange(R) // CHUNK)[None, :]
    mask = np.where(cm == cn, np.exp(lg * np.abs(diff)), np.where(cm < cn, np.exp(lg * diff), np.float32(0.0)))
    qd = np.exp(_log_gamma()[:, None] * (idx[None, :] + np.float32(1.0)))
    kd = np.exp(_log_gamma()[:, None] * (np.float32(R - 1.0) - idx[None, :]))
    gr = np.exp(_log_gamma() * np.float32(R))
    qd_full = np.broadcast_to(qd[:, :, None], (RET_HEADS, R, LANES))
    kd_full = np.broadcast_to(kd.T[:, :, None], (R, RET_HEADS, LANES)).reshape(R, RET_HEADS * LANES)
    f32 = lambda a: jnp.asarray(np.ascontiguousarray(a, dtype=np.float32))
    return f32(mask), f32(qd_full), f32(kd_full), f32(gr)


def _rel_bucket():
    i = jnp.arange(CHUNK, dtype=jnp.int32)[:, None]
    j = jnp.arange(SWA_L, dtype=jnp.int32)[None, :]
    rel = (j - WINDOW) - i
    half = REL_BUCKETS // 2
    max_exact = half // 2
    n = jnp.abs(rel)
    large = max_exact + (jnp.log(jnp.maximum(n, 1).astype(F32) / max_exact)
                         / math.log(REL_MAX_DIST / max_exact) * (half - max_exact)).astype(jnp.int32)
    large = jnp.minimum(large, half - 1)
    return jnp.where(rel > 0, half, 0) + jnp.where(n < max_exact, n, large)


def _tile_rows(t, tm):
    reps = -(-tm // t.shape[0])
    return jnp.asarray(np.tile(t, (reps, 1)) if reps > 1 else t)


def kernel(x_prompt, x_sample, cache_ret_state, cache_swa_k, cache_swa_v, cache_mem_k, cache_mem_v, mem_prompt,
           rel_bias, g_attn, w_in, w_ret_out, w_swa_out, w_mix_out, swa_sinks, g_cross, g_mem, w_cq, w_mk, w_mv,
           w_co, g_ffn, w_gate, w_up, w_down, g_final):
    B, S, D = x_prompt.shape
    Bs, T, _ = x_sample.shape
    assert D == D_MODEL and T == CHUNK and S % R_PROMPT == 0 and cache_swa_k.shape[2] == WINDOW
    assert g_attn.shape[0] == 1, "single layer"
    bf = lambda w: w.astype(BF16)
    vec = lambda g: g.reshape(1, D_MODEL)
    w_in_b, wro, wso, wmo = bf(w_in[0]), bf(w_ret_out[0]), bf(w_swa_out[0]), bf(w_mix_out[0])
    wcq, wco, wmk, wmv = bf(w_cq[0]), bf(w_co[0]), bf(w_mk[0]), bf(w_mv[0])
    wg, wu, wd = bf(w_gate[0]), bf(w_up[0]), bf(w_down[0])

    key_of_col, sub_of_col = _swa_col_maps()
    bucket2 = _rel_bucket()[:, key_of_col]
    bias2 = _rel_bias(rel_bias, bucket2, jnp.asarray(sub_of_col)[None, :])
    sinks = swa_sinks[0]

    def layer(x, pos, R, sub, G, st0, hist, mk, mv, has_history):
        nb, seq, _ = x.shape
        x2d = x.reshape(nb * seq, D_MODEL)
        tm = G * sub * R
        cos_t, sin_t = _rope_tables(pos)
        mt, qd, kd, gr = _retention_tables(R)
        h1, st, kv = _front(x2d, vec(g_attn[0]), w_in_b, _tile_rows(cos_t, tm), _tile_rows(sin_t, tm),
                            kd, st0, hist, mt, qd, gr, bias2, sinks, wro, wso, wmo,
                            G=G, R=R, S=sub, n_tiles=seq // (sub * R), has_history=has_history)
        r_tail = TAIL_ROWS if (G == 1 and seq % TAIL_ROWS == 0) else R
        y = _tail(h1, mk, mv, vec(g_cross[0]), vec(g_ffn[0]), vec(g_final), wcq, wco, wg, wu, wd,
                  G=G, R=r_tail, tiles_per_mem=seq // r_tail)
        return y.reshape(nb, seq, D_MODEL), st, kv

    mk_f, mv_f, mk_b, mv_b = _mem_kv(mem_prompt.reshape(B * N_MEM, D_MODEL), vec(g_mem[0]), wmk, wmv, 512)
    y_p, st_p, kv_p = layer(
        x_prompt, np.arange(S), R_PROMPT, FRONT_SUBTILES if S % (FRONT_SUBTILES * R_PROMPT) == 0 else 1, 1, None, None,
        mk_b.reshape(B, N_MEM, D_MODEL), mv_b.reshape(B, N_MEM, D_MODEL), False)

    Gs = 2 if Bs % 2 == 0 else 1
    hist_s = jnp.concatenate([cache_swa_k[0].reshape(Bs, WINDOW, LANES), cache_swa_v[0].reshape(Bs, WINDOW, LANES)],
                             axis=-1)
    y_s, st_s, kv_s = layer(
        x_sample, PAST_LEN + np.arange(T), CHUNK, 1, Gs,
        cache_ret_state[0].astype(F32), hist_s,
        bf(cache_mem_k[0].reshape(Bs, N_MEM, D_MODEL)), bf(cache_mem_v[0].reshape(Bs, N_MEM, D_MODEL)), True)

    kvshape = (1, B, WINDOW, SWA_KV, SWA_HD)
    k_p = kv_p[:, :, :LANES].reshape(kvshape)
    v_p = kv_p[:, :, LANES:].reshape(kvshape)
    k_s = jnp.concatenate([cache_swa_k[0][:, T:], kv_s[:, :, :LANES].reshape(Bs, T, SWA_KV, SWA_HD)], axis=1)[None]
    v_s = jnp.concatenate([cache_swa_v[0][:, T:], kv_s[:, :, LANES:].reshape(Bs, T, SWA_KV, SWA_HD)], axis=1)[None]
    mem_shape = (1, B, N_MEM, MEM_HEADS, MEM_HD)
    return (y_p, y_s, st_p[None], st_s[None], k_p, k_s, v_p, v_s, mk_f.reshape(mem_shape), mv_f.reshape(mem_shape))
```

```python
import functools
import math

import numpy as np
import jax
import jax.numpy as jnp
from jax import lax
from jax.experimental import pallas as pl
from jax.experimental.pallas import tpu as pltpu

F32 = jnp.float32
BF16 = jnp.bfloat16

D_MODEL = 1024
CHUNK = 64
EPS = 1e-6
NEG_INF = -1e30
PAST_LEN = 4096

RET_HEADS = 4
RET_DK = 256
RET_DV = 256
ROPE_BASE = 10000.0

SWA_HEADS = 16
SWA_KV = 2
SWA_GROUP = 8
SWA_HD = 64
WINDOW = 128
SWA_L = WINDOW + CHUNK
SWA_KV_COLS = SWA_GROUP * SWA_HD
SWA_PAIRS = SWA_GROUP // 2

REL_BUCKETS = 32
REL_MAX_DIST = 128

N_MEM = 256
MEM_HEADS = 4
MEM_HD = 256
D_FF = 2816

OFF_RQ, OFF_RK, OFF_RV, OFF_RG, OFF_SQ, OFF_SKV, OFF_GA, OFF_GB = 0, 1024, 2048, 3072, 4096, 5120, 5376, 6400
D_IN = 7424

LANES = 128
HALF = LANES // 2
VMEM_LIMIT = 56 * 1024 * 1024

FF_COLS = 256
MERGE_COLS = 256
GATE_COLS = 512
KV_COLS = 2 * SWA_KV * SWA_HD
PIPE_DEPTH = 3
TAIL_ROWS = 512
FRONT_SUBTILES = 2
R_PROMPT = 256


def _resident(shape):
    nd = len(shape)
    return pl.BlockSpec(shape, lambda *_: (0,) * nd, pipeline_mode=pl.Buffered(1))


def _rms(x):
    return x * lax.rsqrt(jnp.mean(x * x, axis=-1, keepdims=True) + EPS)


def _interleave(primary, secondary):
    out, j = [], 0
    for k, piece in enumerate(primary):
        out.append(piece)
        while j < len(secondary) and (j + 1) * len(primary) <= (k + 1) * len(secondary):
            out.append(secondary[j])
            j += 1
    return out + secondary[j:]


def _swa_col_maps():
    key = np.concatenate([np.arange(LANES), np.arange(LANES), LANES + np.arange(HALF), LANES + np.arange(HALF)])
    sub = np.concatenate([np.zeros(LANES), np.ones(LANES), np.zeros(HALF), np.ones(HALF)]).astype(np.int32)
    return key.astype(np.int32), sub


def _rel_bias_kernel(tab_ref, bucket_ref, sub_ref, out_ref):
    bucket = bucket_ref[...]
    is_b = sub_ref[...] > 0
    for hk in range(SWA_KV):
        for gp in range(SWA_GROUP // 2):
            h0 = hk * SWA_GROUP + gp * 2
            acc = jnp.zeros(bucket.shape, F32)
            for b in range(REL_BUCKETS):
                val = jnp.where(is_b, tab_ref[b, h0 + 1], tab_ref[b, h0])
                acc = jnp.where(bucket == b, val, acc)
            out_ref[hk, gp * CHUNK:(gp + 1) * CHUNK, :] = acc


def _rel_bias(table, bucket2, sub2):
    ncol = bucket2.shape[1]
    return pl.pallas_call(
        _rel_bias_kernel,
        in_specs=[pl.BlockSpec(memory_space=pltpu.SMEM),
                  pl.BlockSpec(memory_space=pltpu.VMEM),
                  pl.BlockSpec(memory_space=pltpu.VMEM)],
        out_specs=pl.BlockSpec(memory_space=pltpu.VMEM),
        out_shape=jax.ShapeDtypeStruct((SWA_KV, SWA_PAIRS * CHUNK, ncol), F32),
        name="rel_bias",
    )(table, bucket2, sub2)


def _front_kernel(x_ref, xp_ref, g_ref, w_ref, cos_ref, sin_ref, kdec_ref,
                  st0_ref, hist_ref, mt_ref, qd_ref, gr_ref, bias_ref, sink_ref,
                  wro_ref, wso_ref, wmo_ref,
                  h_ref, st_ref, kvl_ref,
                  xn, rq, rk, rkd, rv, rgs, sq, gas, gbs, state, ka, kb, va, vb, ain, swo, mixs,
                  *, G, R, S, n_tiles, has_history):
    Rg = S * R
    t = pl.program_id(0)
    n_real = pl.num_programs(0) - 1
    i = lax.rem(jnp.minimum(t, n_real - 1), n_tiles)
    lane =lax.broadcasted_iota(jnp.int32, (1, LANES), 1)
    lo = lane < HALF

    def split_heads(x):
        r = pltpu.roll(x, HALF, axis=1)
        z = jnp.zeros_like(x)
        a0 = jnp.where(lo, x, z).astype(BF16)
        b0 = jnp.where(lo, z, r).astype(BF16)
        a1 = jnp.where(lo, r, z).astype(BF16)
        b1 = jnp.where(lo, z, x).astype(BF16)
        return (a0, a1), (b0, b1)

    @pl.when(t == 0)
    def _():
        ain[...] = jnp.zeros(ain.shape, BF16)
        swo[...] = jnp.zeros(swo.shape, BF16)
        gas[...] = jnp.zeros(gas.shape, BF16)
        gbs[...] = jnp.zeros(gbs.shape, BF16)

    @pl.when(i == 0)
    def _():
        state[...] = st0_ref[...] if has_history else jnp.zeros(state.shape, F32)
        for g in range(G):
            hist = hist_ref[g] if has_history else jnp.zeros((WINDOW, KV_COLS), F32)
            tail = slice(Rg, Rg + WINDOW)
            (a0, a1), (b0, b1) = split_heads(hist[:, :LANES])
            ka[g, 0, tail], ka[g, 1, tail], kb[g, 0, tail], kb[g, 1, tail] = a0, a1, b0, b1
            (a0, a1), (b0, b1) = split_heads(hist[:, LANES:])
            va[g, 0, tail], va[g, 1, tail], vb[g, 0, tail], vb[g, 1, tail] = a0, a1, b0, b1

    def norm():
        xn[...] = (_rms(x_ref[...]) * g_ref[...]).astype(BF16)

    def proj(c0, n):
        return jnp.dot(xn[...], w_ref[:, c0:c0 + n], preferred_element_type=F32)

    def proj_head(h):
        c = h * RET_DK
        cos = cos_ref[...]
        sin = sin_ref[...]
        acc = proj(OFF_RQ + c, RET_DK)
        x1, x2 = acc[:, :LANES], acc[:, LANES:]
        rq[:, c:c + LANES] = (x1 * cos - x2 * sin).astype(BF16)
        rq[:, c + LANES:c + RET_DK] = (x2 * cos + x1 * sin).astype(BF16)
        acc = proj(OFF_RK + c, RET_DK)
        x1, x2 = acc[:, :LANES], acc[:, LANES:]
        k1 = (x1 * cos - x2 * sin) * (RET_DK ** -0.5)
        k2 = (x2 * cos + x1 * sin) * (RET_DK ** -0.5)
        rk[:, c:c + LANES] = k1.astype(BF16)
        rk[:, c + LANES:c + RET_DK] = k2.astype(BF16)
        kd = jnp.concatenate([kdec_ref[:, h * LANES:(h + 1) * LANES]] * (G * S), axis=0)
        rkd[:, c:c + LANES] = (k1 * kd).astype(BF16)
        rkd[:, c + LANES:c + RET_DK] = (k2 * kd).astype(BF16)
        rv[:, c:c + RET_DV] = proj(OFF_RV + c, RET_DV).astype(BF16)
        rgs[:, c:c + RET_DV] = jax.nn.silu(proj(OFF_RG + c, RET_DV)).astype(BF16)

    def proj_sq(j):
        c = j * GATE_COLS
        sq[:, c:c + GATE_COLS] = proj(OFF_SQ + c, GATE_COLS).astype(BF16)

    def proj_kv():
        half = (G * Rg) // 2
        w_kv = w_ref[:, OFF_SKV:OFF_SKV + KV_COLS]
        kvt_all = jnp.concatenate([jnp.dot(xn[:half], w_kv, preferred_element_type=F32),
                                   jnp.dot(xn[half:], w_kv, preferred_element_type=F32)], axis=0)
        for g in range(G):
            kvl_ref[g] = kvt_all[(g + 1) * Rg - min(Rg, WINDOW):(g + 1) * Rg]
        for g in range(G):
            kvt = kvt_all[g * Rg:(g + 1) * Rg]
            for buf in (ka, kb, va, vb):
                for hk in range(SWA_KV):
                    buf[g, hk, :WINDOW] = buf[g, hk, Rg:Rg + WINDOW]
            (a0, a1), (b0, b1) = split_heads(kvt[:, :LANES])
            ka[g, 0, WINDOW:], ka[g, 1, WINDOW:], kb[g, 0, WINDOW:], kb[g, 1, WINDOW:] = a0, a1, b0, b1
            (a0, a1), (b0, b1) = split_heads(kvt[:, LANES:])
            va[g, 0, WINDOW:], va[g, 1, WINDOW:], vb[g, 0, WINDOW:], vb[g, 1, WINDOW:] = a0, a1, b0, b1

    def proj_gate(dst, off, j):
        c = j * GATE_COLS
        dst[:, c:c + GATE_COLS] = jax.nn.sigmoid(proj(off + c, GATE_COLS)).astype(BF16)

    def ret_front(g, sub, h):
        rows = slice(g * Rg + sub * R, g * Rg + (sub + 1) * R)
        cols = slice(h * RET_DK, (h + 1) * RET_DK)
        q = rq[rows, cols]
        v = rv[rows, cols]
        s_prev = state[g, h]
        sc = lax.dot_general(q, rk[rows, cols], (((1,), (1,)), ((), ())), preferred_element_type=F32)
        qs = jnp.dot(q, s_prev.astype(BF16), preferred_element_type=F32)
        kv_new = lax.dot_general(rkd[rows, cols], v, (((0,), (0,)), ((), ())), preferred_element_type=F32)
        return sc, qs, kv_new, s_prev, v

    def ret_back(g, sub, h, vals):
        sc, qs, kv_new, s_prev, v = vals
        rows = slice(g * Rg + sub * R, g * Rg + (sub + 1) * R)
        cols = slice(h * RET_DK, (h + 1) * RET_DK)
        qd = jnp.concatenate([qd_ref[h]] * (RET_DV // LANES), axis=1)
        o = jnp.dot((sc * mt_ref[h]).astype(BF16), v, preferred_element_type=F32) + qs * qd
        state[g, h] = s_prev * gr_ref[h] + kv_new
        ain[rows, cols] =(_rms(o) * rgs[rows, cols].astype(F32)).astype(BF16)

    ones_a = jnp.broadcast_to(jnp.where(lo, 1.0, 0.0).astype(BF16), (LANES, LANES))
    ones_b = jnp.broadcast_to(jnp.where(lo, 0.0, 1.0).astype(BF16), (LANES, LANES))
    lane3 = lax.broadcasted_iota(jnp.int32, (1, 3 * LANES), 1)
    key_of_col = jnp.where(lane3 < 2 * LANES, lane3 % LANES, LANES + lane3 % HALF)

    def swa_front(g, c, hk):
        w0 = c * CHUNK
        r0 = g * Rg + c * CHUNK
        q2 = jnp.concatenate(
            [sq[r0:r0 + CHUNK, hk * SWA_KV_COLS + gp * LANES: hk * SWA_KV_COLS + (gp + 1) * LANES]
             for gp in range(SWA_PAIRS)],
            axis=0)
        kbd = jnp.concatenate([ka[g, hk, w0:w0 + LANES], kb[g, hk, w0:w0 + LANES],
                               ka[g, hk, w0 + LANES:w0 + SWA_L], kb[g, hk, w0 + LANES:w0 + SWA_L]],
                              axis=0)
        return lax.dot_general(q2, kbd, (((1,), (1,)), ((), ())), preferred_element_type=F32)

    def swa_back(g, c, hk, s):
        w0 = c * CHUNK
        r0 = g * Rg + c * CHUNK
        s = s * (SWA_HD ** -0.5) + bias_ref[hk]
        if not has_history and c < WINDOW // CHUNK:
            kpos = key_of_col + (i * Rg + c * CHUNK - WINDOW)
            s = jnp.where(kpos >= 0, s, NEG_INF)
        t0, t1, t2 = s[:, :LANES], s[:, LANES:2 * LANES], s[:, 2 * LANES:]
        m_a = jnp.max(jnp.maximum(t0, jnp.where(lo, t2, NEG_INF)), axis=-1, keepdims=True)
        m_b = jnp.max(jnp.maximum(t1, jnp.where(lo, NEG_INF, t2)), axis=-1, keepdims=True)
        sk_a = jnp.concatenate([jnp.full((CHUNK, 1), sink_ref[hk * SWA_GROUP + 2 * gp], F32)
                                for gp in range(SWA_PAIRS)], axis=0)
        sk_b = jnp.concatenate([jnp.full((CHUNK, 1), sink_ref[hk * SWA_GROUP + 2 * gp + 1], F32)
                                for gp in range(SWA_PAIRS)], axis=0)
        m_a = jnp.maximum(m_a, sk_a)
        m_b = jnp.maximum(m_b, sk_b)
        e = jnp.concatenate([jnp.exp(t0 - m_a), jnp.exp(t1 - m_b),
                             jnp.exp(t2 - jnp.where(lo, m_a, m_b))], axis=1).astype(BF16)
        vbd = jnp.concatenate([
            jnp.concatenate([va[g, hk, w0:w0 + LANES], ones_a], axis=1),
            jnp.concatenate([vb[g, hk, w0:w0 + LANES], ones_b], axis=1),
            jnp.concatenate([va[g, hk, w0 + LANES:w0 + SWA_L], ones_a[:HALF]], axis=1),
            jnp.concatenate([vb[g, hk, w0 + LANES:w0 + SWA_L], ones_b[:HALF]], axis=1)], axis=0)
        oa = jnp.dot(e, vbd, preferred_element_type=F32)
        den = oa[:, LANES:] + jnp.where(lo, jnp.exp(sk_a - m_a), jnp.exp(sk_b - m_b))
        o2 = (oa[:, :LANES] / den).astype(BF16)
        for gp in range(SWA_PAIRS):
            c0 = hk * SWA_KV_COLS + gp * LANES
            swo[r0:r0 + CHUNK, c0:c0 + LANES] = o2[gp * CHUNK:(gp + 1) * CHUNK]

    def merge_ab(j):
        cols = slice(j * MERGE_COLS, (j + 1) * MERGE_COLS)
        a = jnp.dot(ain[...], wro_ref[:, cols], preferred_element_type=F32)
        b = jnp.dot(swo[...], wso_ref[:, cols], preferred_element_type=F32)
        mixs[:, cols] = (gas[:, cols].astype(F32) * a + gbs[:, cols].astype(F32) * b).astype(BF16)

    def merge_y(j):
        cols = slice(j * MERGE_COLS, (j + 1) * MERGE_COLS)
        h_ref[:, cols] = xp_ref[:, cols] + jnp.dot(mixs[...], wmo_ref[:, cols], preferred_element_type=F32)

    n_mc = D_MODEL // MERGE_COLS
    n_gc = D_MODEL // GATE_COLS
    dense = [(functools.partial(merge_ab, 0), None), (norm, None), (functools.partial(merge_ab, 1), None),
             (functools.partial(proj_head, 0), ("head", 0))]
    dense += [(functools.partial(merge_ab, j), ("merged", 0) if j == n_mc - 1 else None) for j in range(2, n_mc)]
    dense += [(functools.partial(proj_head, h), ("head", h)) for h in range(1, RET_HEADS)]
    dense += [(functools.partial(proj_sq, j), None) for j in range(n_gc)]
    dense.append((proj_kv, ("swa", 0)))
    dense += [(functools.partial(proj_gate, gas, OFF_GA, j), None) for j in range(n_gc)]
    dense += [(functools.partial(proj_gate, gbs, OFF_GB, j), None) for j in range(n_gc)]
    dense += [(functools.partial(merge_y, j), None) for j in range(n_mc)]

    assert S == 1 or PIPE_DEPTH < RET_HEADS * G
    items = [(ret_front, ret_back, (g, sub, h), ("head", h))
             for sub in range(S) for h in range(RET_HEADS) for g in range(G)]
    items += [(swa_front, swa_back, (g, c, hk), ("swa", 0))
              for g in range(G) for c in range(Rg // CHUNK) for hk in range(SWA_KV)]
    tasks = []
    for n in range(len(items) + PIPE_DEPTH):
        if n < len(items):
            tasks.append(("front", n))
        if n >= PIPE_DEPTH:
            tasks.append(("back", n - PIPE_DEPTH))

    pending, done, nxt = {}, set(), 0

    def run_task(kind, n):
        front, back, args, _ = items[n]
        if kind == "front":
            pending[n] = front(*args)
        else:
            back(*args, pending.pop(n))

    for k, (piece, tag) in enumerate(dense):
        piece()
        if tag is not None:
            done.add(tag)
        quota = -(-(len(tasks) - nxt) // (len(dense) - k))
        while nxt < len(tasks) and quota > 0:
            kind, n = tasks[nxt]
            if kind == "front" and items[n][3] not in done:
                break
            if kind == "back" and ("merged", 0) not in done:
                break
            run_task(kind, n)
            nxt += 1
            quota -= 1
    assert nxt == len(tasks)

    @pl.when((i == n_tiles - 1) & (t < n_real))
    def _():
        st_ref[...] = state[...]


def _front(x2d, g, w_in, cos_t, sin_t, kdec_t, st0, hist, mt, qd, gr, bias2, sinks, wro, wso, wmo,
           *, G, R, S, n_tiles, has_history):
    n = x2d.shape[0]
    Rg = S * R
    tm = G * Rg
    n_steps = n // tm
    n_rows = n // (Rg * n_tiles)
    keep = min(Rg, WINDOW)
    assert n_steps * tm == n and n_steps % n_tiles == 0 and (G == 1 or n_tiles == 1)
    assert cos_t.shape == (n_tiles * tm, LANES) and kdec_t.shape == (R, RET_HEADS * LANES)
    cur = lambda t: jnp.minimum(t, n_steps - 1)
    prev = lambda t: jnp.maximum(t - 1, 0)
    per_row = pl.BlockSpec((G, RET_HEADS, RET_DK, RET_DV), lambda t: (cur(t) // n_tiles, 0, 0, 0))
    rope = pl.BlockSpec((tm, LANES), lambda t: (cur(t) % n_tiles, 0))
    kw = dict(G=G, R=R, S=S, n_tiles=n_tiles, has_history=has_history)
    if has_history:
        kern = functools.partial(_front_kernel, **kw)
        carried = [st0, hist]
        carried_specs = [per_row, pl.BlockSpec((G, WINDOW, KV_COLS), lambda t: (cur(t) // n_tiles, 0, 0))]
    else:
        def kern(x_ref, xp_ref, g_ref, w_ref, cos_ref, sin_ref, kdec_ref, *rest):
            _front_kernel(x_ref, xp_ref, g_ref, w_ref, cos_ref, sin_ref, kdec_ref, None, None, *rest, **kw)
        carried, carried_specs = [], []
    tile_bf = pltpu.VMEM((tm, D_MODEL), BF16)
    return pl.pallas_call(
        kern,
        grid=(n_steps + 1,),
        in_specs=[pl.BlockSpec((tm, D_MODEL), lambda t: (cur(t), 0)),
                  pl.BlockSpec((tm, D_MODEL), lambda t: (prev(t), 0)),
                  _resident((1, D_MODEL)), _resident(w_in.shape), rope, rope, _resident(kdec_t.shape)]
                 + carried_specs
                 + [_resident(mt.shape), _resident(qd.shape),
                  pl.BlockSpec(memory_space=pltpu.SMEM),
                  _resident(bias2.shape),
                  pl.BlockSpec(memory_space=pltpu.SMEM),
                  _resident(wro.shape), _resident(wso.shape), _resident(wmo.shape)],
        out_specs=[pl.BlockSpec((tm, D_MODEL), lambda t: (prev(t), 0)), per_row,
                   pl.BlockSpec((G, keep, KV_COLS), lambda t: (cur(t) // n_tiles, 0, 0))],
        out_shape=[jax.ShapeDtypeStruct((n, D_MODEL), F32),
                   jax.ShapeDtypeStruct((n_rows, RET_HEADS, RET_DK, RET_DV), F32),
                   jax.ShapeDtypeStruct((n_rows, keep, KV_COLS), F32)],
        scratch_shapes=[tile_bf] * 9
                       + [pltpu.VMEM((G, RET_HEADS, RET_DK, RET_DV), F32)]
                       + [pltpu.VMEM((G, SWA_KV, WINDOW + Rg, LANES), BF16)] * 4
                       + [tile_bf] * 3,
        compiler_params=pltpu.CompilerParams(dimension_semantics=("arbitrary",), vmem_limit_bytes=VMEM_LIMIT),
        name="front",
    )(x2d, x2d, g, w_in, cos_t, sin_t, kdec_t, *carried, mt, qd, gr, bias2, sinks, wro, wso, wmo)


def _tail_kernel(h_ref, mk_ref, mv_ref, gc_ref, gf_ref, gl_ref, wcq_ref, wco_ref, wg_ref, wu_ref, wd_ref,
                 y_ref, qs, att, h2s, hn2s, h2_prev, hn2_prev, acts, h3s, *, G, R):
    @pl.when(pl.program_id(0) == 0)
    def _():
        h2s[...] = jnp.zeros(h2s.shape, F32)
        hn2s[...] = jnp.zeros(hn2s.shape, BF16)

    h2_prev[...] = h2s[...]
    hn2_prev[...] = hn2s[...]

    def q_proj():
        hn = (_rms(h_ref[...]) * gc_ref[...]).astype(BF16)
        qs[...] = jnp.dot(hn, wcq_ref[...], preferred_element_type=F32).astype(BF16)

    def head_front(g, hd):
        rows = slice(g * R, (g + 1) * R)
        cols = slice(hd * MEM_HD, (hd + 1) * MEM_HD)
        return lax.dot_general(qs[rows, cols], mk_ref[g, :, cols], (((1,), (1,)), ((), ())),
                               preferred_element_type=F32)

    def head_back(g, hd, s):
        rows = slice(g * R, (g + 1) * R)
        cols = slice(hd * MEM_HD, (hd + 1) * MEM_HD)
        s = s * (MEM_HD ** -0.5)
        e = jnp.exp(s - jnp.max(s, axis=-1, keepdims=True))
        den = jnp.sum(e, axis=-1, keepdims=True)
        o = jnp.dot(e.astype(BF16), mv_ref[g, :, cols], preferred_element_type=F32) / den
        att[rows, cols] = o.astype(BF16)

    def attn_out():
        h2 = h_ref[...] + jnp.dot(att[...], wco_ref[...], preferred_element_type=F32)
        h2s[...] = h2
        hn2s[...] = (_rms(h2) * gf_ref[...]).astype(BF16)

    def ff_chunk(j):
        cols = slice(j * FF_COLS, (j + 1) * FF_COLS)
        hn = hn2_prev[...]
        gate = jnp.dot(hn, wg_ref[:, cols], preferred_element_type=F32)
        up = jnp.dot(hn, wu_ref[:, cols], preferred_element_type=F32)
        acts[:, cols] = (jax.nn.silu(gate) * up).astype(BF16)

    sq_sums = []

    def down_chunk(j):
        cols = slice(j * FF_COLS, (j + 1) * FF_COLS)
        h3 = h2_prev[:, cols] + jnp.dot(acts[...], wd_ref[:, cols], preferred_element_type=F32)
        h3s[:, cols] = h3
        sq_sums.append(jnp.sum(h3 * h3, axis=-1, keepdims=True))

    def final_norm():
        mean_sq = functools.reduce(lambda a, b: a + b, sq_sums) * (1.0 / D_MODEL)
        y_ref[...] = h3s[...] * lax.rsqrt(mean_sq + EPS) * gl_ref[...]

    heads = [(g, hd) for g in range(G) for hd in range(MEM_HEADS)]
    pending = {}

    def front(n):
        pending[n] = head_front(*heads[n])

    def back(n):
        head_back(*heads[n], pending.pop(n))

    attn = [q_proj]
    for n in range(len(heads) + 1):
        if n < len(heads):
            attn.append(functools.partial(front, n))
        if n >= 1:
            attn.append(functools.partial(back, n - 1))
    attn.append(attn_out)
    ffn = [functools.partial(ff_chunk, j) for j in range(D_FF // FF_COLS)]
    ffn += [functools.partial(down_chunk, j) for j in range(D_MODEL // FF_COLS)]
    ffn.append(final_norm)
    for piece in _interleave(ffn, attn):
        piece()


def _tail(h2d, mk, mv, gc, gf, gl, wcq, wco, wg, wu, wd, *, G, R, tiles_per_mem):
    n = h2d.shape[0]
    tm = G * R
    n_steps = n // tm
    assert n_steps * tm == n and (G == 1 or tiles_per_mem == 1)
    a_tile = lambda t: jnp.minimum(t, n_steps - 1)
    mem = pl.BlockSpec((G, N_MEM, D_MODEL), lambda t: (a_tile(t) // tiles_per_mem, 0, 0))
    vec = _resident((1, D_MODEL))
    kern = functools.partial(_tail_kernel, G=G, R=R)
    return pl.pallas_call(
        kern,
        grid=(n_steps + 1,),
        in_specs=[pl.BlockSpec((tm, D_MODEL), lambda t: (a_tile(t), 0)), mem, mem, vec, vec, vec,
                  _resident(wcq.shape), _resident(wco.shape), _resident(wg.shape), _resident(wu.shape),
                  _resident(wd.shape)],
        out_specs=pl.BlockSpec((tm, D_MODEL), lambda t: (jnp.maximum(t - 1, 0), 0)),
        out_shape=jax.ShapeDtypeStruct((n, D_MODEL), F32),
        scratch_shapes=[pltpu.VMEM((tm, D_MODEL), BF16), pltpu.VMEM((tm, D_MODEL), BF16),
                        pltpu.VMEM((tm, D_MODEL), F32), pltpu.VMEM((tm, D_MODEL), BF16),
                        pltpu.VMEM((tm, D_MODEL), F32), pltpu.VMEM((tm, D_MODEL), BF16),
                        pltpu.VMEM((tm, D_FF), BF16), pltpu.VMEM((tm, D_MODEL), F32)],
        compiler_params=pltpu.CompilerParams(dimension_semantics=("arbitrary",), vmem_limit_bytes=VMEM_LIMIT),
        name="tail",
    )(h2d, mk, mv, gc, gf, gl, wcq, wco, wg, wu, wd)


def _mem_kv_kernel(m_ref, g_ref, wk_ref, wv_ref, k_ref, v_ref, kb_ref, vb_ref):
    mn = (_rms(m_ref[...]) * g_ref[...]).astype(BF16)
    k = jnp.dot(mn, wk_ref[...], preferred_element_type=F32)
    v = jnp.dot(mn, wv_ref[...], preferred_element_type=F32)
    for hd in range(MEM_HEADS):
        k_ref[:, hd, :] = k[:, hd * MEM_HD:(hd + 1) * MEM_HD]
        v_ref[:, hd, :] = v[:, hd * MEM_HD:(hd + 1) * MEM_HD]
    kb_ref[...] = k.astype(BF16)
    vb_ref[...] = v.astype(BF16)


def _mem_kv(mem2d, g, wk, wv, tm):
    n = mem2d.shape[0]
    assert n % tm == 0
    blk = pl.BlockSpec((tm, D_MODEL), lambda i: (i, 0))
    blk4 = pl.BlockSpec((tm, MEM_HEADS, MEM_HD), lambda i: (i, 0, 0))
    return pl.pallas_call(
        _mem_kv_kernel,
        grid=(n // tm,),
        in_specs=[blk, _resident((1, D_MODEL)), _resident(wk.shape), _resident(wv.shape)],
        out_specs=[blk4, blk4, blk, blk],
        out_shape=[jax.ShapeDtypeStruct((n, MEM_HEADS, MEM_HD), F32)] * 2
                  + [jax.ShapeDtypeStruct((n, D_MODEL), BF16)] * 2,
        compiler_params=pltpu.CompilerParams(dimension_semantics=("parallel",), vmem_limit_bytes=VMEM_LIMIT),
        name="mem_kv",
    )(mem2d, g, wk, wv)


def _log_gamma():
    return np.log(np.float32(1.0) - np.float32(2.0) ** (np.float32(-5.0) - np.arange(RET_HEADS, dtype=np.float32)))


def _rope_tables(pos):
    half = RET_DK // 2
    inv = np.float32(ROPE_BASE) ** (-np.arange(half, dtype=np.float32) / np.float32(half))
    ang = pos.astype(np.float32)[:, None] * inv[None, :]
    return np.cos(ang).astype(np.float32), np.sin(ang).astype(np.float32)


def _retention_tables(R):
    lg = _log_gamma()[:, None, None]
    idx = np.arange(R, dtype=np.float32)
    diff = idx[:, None] - idx[None, :]
    cn = (np.arange(R) // CHUNK)[:, None]
    cm = (np.arange(R) // CHUNK)[None, :]
    mask = np.where(cm == cn, np.exp(lg * np.abs(diff)), np.where(cm < cn, np.exp(lg * diff), np.float32(0.0)))
    qd = np.exp(_log_gamma()[:, None] * (idx[None, :] + np.float32(1.0)))
    kd = np.exp(_log_gamma()[:, None] * (np.float32(R - 1.0) - idx[None, :]))
    gr = np.exp(_log_gamma() * np.float32(R))
    qd_full = np.broadcast_to(qd[:, :, None], (RET_HEADS, R, LANES))
    kd_full = np.broadcast_to(kd.T[:, :, None], (R, RET_HEADS, LANES)).reshape(R, RET_HEADS * LANES)
    f32 = lambda a: jnp.asarray(np.ascontiguousarray(a, dtype=np.float32))
    return f32(mask), f32(qd_full), f32(kd_full), f32(gr)


def _rel_bucket():
    i = jnp.arange(CHUNK, dtype=jnp.int32)[:, None]
    j = jnp.arange(SWA_L, dtype=jnp.int32)[None, :]
    rel = (j - WINDOW) - i
    half = REL_BUCKETS // 2
    max_exact = half // 2
    n = jnp.abs(rel)
    large = max_exact + (jnp.log(jnp.maximum(n, 1).astype(F32) / max_exact)
                         / math.log(REL_MAX_DIST / max_exact) * (half - max_exact)).astype(jnp.int32)
    large = jnp.minimum(large, half - 1)
    return jnp.where(rel > 0, half, 0) + jnp.where(n < max_exact, n, large)


def _tile_rows(t, tm):
    reps = -(-tm // t.shape[0])
    return jnp.asarray(np.tile(t, (reps, 1)) if reps > 1 else t)


def kernel(x_prompt, x_sample, cache_ret_state, cache_swa_k, cache_swa_v, cache_mem_k, cache_mem_v, mem_prompt,
           rel_bias, g_attn, w_in, w_ret_out, w_swa_out, w_mix_out, swa_sinks, g_cross, g_mem, w_cq, w_mk, w_mv,
           w_co, g_ffn, w_gate, w_up, w_down, g_final):
    B, S, D = x_prompt.shape
    Bs, T, _ = x_sample.shape
    assert D == D_MODEL and T == CHUNK and S % R_PROMPT == 0 and cache_swa_k.shape[2] == WINDOW
    assert g_attn.shape[0] == 1, "single layer"
    bf = lambda w: w.astype(BF16)
    vec = lambda g: g.reshape(1, D_MODEL)
    w_in_b, wro, wso, wmo = bf(w_in[0]), bf(w_ret_out[0]), bf(w_swa_out[0]), bf(w_mix_out[0])
    wcq, wco, wmk, wmv = bf(w_cq[0]), bf(w_co[0]), bf(w_mk[0]), bf(w_mv[0])
    wg, wu, wd = bf(w_gate[0]), bf(w_up[0]), bf(w_down[0])

    key_of_col, sub_of_col = _swa_col_maps()
    bucket2 = _rel_bucket()[:, key_of_col]
    bias2 = _rel_bias(rel_bias, bucket2, jnp.asarray(sub_of_col)[None, :])
    sinks = swa_sinks[0]

    def layer(x, pos, R, sub, G, st0, hist, mk, mv, has_history):
        nb, seq, _ = x.shape
        x2d = x.reshape(nb * seq, D_MODEL)
        tm = G * sub * R
        cos_t, sin_t = _rope_tables(pos)
        mt, qd, kd, gr = _retention_tables(R)
        h1, st, kv = _front(x2d, vec(g_attn[0]), w_in_b, _tile_rows(cos_t, tm), _tile_rows(sin_t, tm),
                            kd, st0, hist, mt, qd, gr, bias2, sinks, wro, wso, wmo,
                            G=G, R=R, S=sub, n_tiles=seq // (sub * R), has_history=has_history)
        r_tail = TAIL_ROWS if (G == 1 and seq % TAIL_ROWS == 0) else R
        y = _tail(h1, mk, mv, vec(g_cross[0]), vec(g_ffn[0]), vec(g_final), wcq, wco, wg, wu, wd,
                  G=G, R=r_tail, tiles_per_mem=seq // r_tail)
        return y.reshape(nb, seq, D_MODEL), st, kv

    mk_f, mv_f, mk_b, mv_b = _mem_kv(mem_prompt.reshape(B * N_MEM, D_MODEL), vec(g_mem[0]), wmk, wmv, TAIL_ROWS)
    y_p, st_p, kv_p = layer(
        x_prompt, np.arange(S), R_PROMPT, FRONT_SUBTILES if S % (FRONT_SUBTILES * R_PROMPT) == 0 else 1, 1, None, None,
        mk_b.reshape(B, N_MEM, D_MODEL), mv_b.reshape(B, N_MEM, D_MODEL), False)

    Gs = 2 if Bs % 2 == 0 else 1
    hist_s = jnp.concatenate([cache_swa_k[0].reshape(Bs, WINDOW, LANES), cache_swa_v[0].reshape(Bs, WINDOW, LANES)],
                             axis=-1)
    y_s, st_s, kv_s = layer(
        x_sample, PAST_LEN + np.arange(T), CHUNK, 1, Gs,
        cache_ret_state[0].astype(F32), hist_s,
        bf(cache_mem_k[0].reshape(Bs, N_MEM, D_MODEL)), bf(cache_mem_v[0].reshape(Bs, N_MEM, D_MODEL)), True)

    kvshape = (1, B, WINDOW, SWA_KV, SWA_HD)
    k_p = kv_p[:, :, :LANES].reshape(kvshape)
    v_p = kv_p[:, :, LANES:].reshape(kvshape)
    k_s = jnp.concatenate([cache_swa_k[0][:, T:], kv_s[:, :, :LANES].reshape(Bs, T, SWA_KV, SWA_HD)], axis=1)[None]
    v_s = jnp.concatenate([cache_swa_v[0][:, T:], kv_s[:, :, LANES:].reshape(Bs, T, SWA_KV, SWA_HD)], axis=1)[None]
    mem_shape = (1, B, N_MEM, MEM_HEADS, MEM_HD)
    return (y_p, y_s, st_p[None], st_s[None], k_p, k_s, v_p, v_s, mk_f.reshape(mem_shape), mv_f.reshape(mem_shape))
```
